```python
import math
import jax, jax.numpy as jnp
from jax import lax
import numpy as np

D_MODEL = 2048
BATCH = 16
SEQ = 2048
DEPTH = 1
DEC_BATCH = 8
DEC_SEQ = 4096
PAST_LEN = 128

GRID_W = 64
N_HEADS_ATTN = 16
HEAD_DIM = 64
D_ATTN = N_HEADS_ATTN * HEAD_DIM
WIN_H_MAX = 8
WIN_W = 16
N_HEADS_SSM = 16
SSM_HEAD_DIM = 64
D_SSM = N_HEADS_SSM * SSM_HEAD_DIM
N_GROUPS_SSM = 2
D_STATE = 128
D_CONV = 5
CHUNK = 128
D_BC = N_GROUPS_SSM * D_STATE
D_XBC = D_SSM + 2 * D_BC
D_MIX = D_ATTN + D_SSM
D_IN_PROJ = 3 * D_ATTN + D_SSM + D_XBC + 2 * N_HEADS_SSM
N_EXPERT_GROUPS = 4
EXPERTS_PER_GROUP = 8
N_EXPERTS = N_EXPERT_GROUPS * EXPERTS_PER_GROUP
TOP_K = 2
D_EXPERT = 512
MOE_BLOCK = 128
EPS = 1e-6

kernel_name = "hymba_natten_ssd_hmoe_encoder"


def rms_norm(x, gain):
    xf = x.astype(jnp.float32)
    xf = xf * lax.rsqrt(jnp.mean(xf * xf, axis=-1, keepdims=True) + EPS)
    return (xf * gain.astype(jnp.float32)).astype(x.dtype)


def neighbourhood_attention(q, k, v, rpb):
    b, l, h, dh = q.shape
    rows = l // GRID_W
    win_h = min(WIN_H_MAX, rows)
    qg = q.reshape(b, rows, GRID_W, h, dh)
    kg = k.reshape(b, rows, GRID_W, h, dh)
    vg = v.reshape(b, rows, GRID_W, h, dh)
    cols = jnp.arange(GRID_W, dtype=jnp.int32)
    col_start = jnp.clip(cols - WIN_W // 2, 0, GRID_W - WIN_W)
    col_idx = col_start[:, None] + jnp.arange(WIN_W, dtype=jnp.int32)[None]
    col_onehot = (col_idx[:, :, None] == cols[None, None, :]).astype(v.dtype)
    dc = col_idx - cols[:, None] + (WIN_W - 1)
    gather_idx = jnp.broadcast_to(col_idx[None, None, :, None, :], (b, h, GRID_W, win_h, WIN_W))
    scale = HEAD_DIM ** -0.5

    def row_fn(r):
        rs = jnp.clip(r - win_h // 2, 0, rows - win_h)
        q_r = lax.dynamic_index_in_dim(qg, r, axis=1, keepdims=False)
        k_b = lax.dynamic_slice_in_dim(kg, rs, win_h, axis=1)
        v_b = lax.dynamic_slice_in_dim(vg, rs, win_h, axis=1)
        s = jnp.einsum('bqhd,bikhd->bhqik', q_r, k_b,
                       preferred_element_type=jnp.float32) * scale
        s = jnp.take_along_axis(s, gather_idx, axis=-1)
        dr = rs + jnp.arange(win_h, dtype=jnp.int32) - r + (WIN_H_MAX - 1)
        bias = rpb[:, dr[:, None, None], dc[None, :, :]]
        s = s + jnp.transpose(bias, (0, 2, 1, 3)).astype(jnp.float32)[None]
        p = jax.nn.softmax(s.reshape(b, h, GRID_W, win_h * WIN_W), axis=-1)
        p = p.reshape(b, h, GRID_W, win_h, WIN_W).astype(v.dtype)
        p_band = jnp.einsum('bhqij,qjk->bhqik', p, col_onehot)
        return jnp.einsum('bhqik,bikhd->bqhd', p_band, v_b)

    out = lax.map(row_fn, jnp.arange(rows, dtype=jnp.int32))
    return jnp.transpose(out, (1, 0, 2, 3, 4)).reshape(b, l, h * dh)


def segsum(a):
    n = a.shape[-1]
    cs = jnp.cumsum(a, axis=-1)
    seg = cs[..., :, None] - cs[..., None, :]
    mask = jnp.tril(jnp.ones((n, n), dtype=bool))
    return jnp.where(mask, seg, -jnp.inf)


def ssd_chunked(x, dt, a, bm, cm):
    b, l, h, p = x.shape
    g, n = bm.shape[-2:]
    e = h // g
    c = l // CHUNK
    xd = (x * dt[..., None]).reshape(b, c, CHUNK, g, e, p)
    adt = jnp.moveaxis((dt * a).reshape(b, c, CHUNK, g, e), 2, -1)
    bm = bm.reshape(b, c, CHUNK, g, n)
    cm = cm.reshape(b, c, CHUNK, g, n)
    a_cs = jnp.cumsum(adt, axis=-1)
    lmat = jnp.exp(segsum(adt))
    cb = jnp.einsum('bcqgn,bcsgn->bcgqs', cm, bm)
    y_diag = jnp.einsum('bcgeqs,bcsgep->bcqgep', cb[:, :, :, None] * lmat, xd)
    decay_states = jnp.moveaxis(jnp.exp(a_cs[..., -1:] - a_cs), -1, 2)
    states = jnp.einsum('bcsgn,bcsgep->bcgepn', bm, xd * decay_states[..., None])
    chunk_decay = jnp.exp(a_cs[..., -1])

    def step(carry, inp):
        st, dec = inp
        return carry * dec[..., None, None] + st, carry

    init = jnp.zeros((b, g, e, p, n), x.dtype)
    _, prev = lax.scan(step, init, (jnp.moveaxis(states, 1, 0), jnp.moveaxis(chunk_decay, 1, 0)))
    prev = jnp.moveaxis(prev, 0, 1)
    y_off = jnp.einsum('bcqgn,bcgepn->bcqgep', cm, prev)
    y_off = y_off * jnp.moveaxis(jnp.exp(a_cs), -1, 2)[..., None]
    return (y_diag + y_off).reshape(b, l, h, p)


def ssd_mixer(z, xbc, dt_raw, conv_w, conv_b, a_log_f, a_log_b, dt_bias_f, dt_bias_b, d_skip, ssm_norm_g):
    b, l, _ = xbc.shape
    xbc = lax.conv_general_dilated(xbc, conv_w, window_strides=(1,),
                                   padding=[(D_CONV // 2, D_CONV // 2)],
                                   dimension_numbers=('NWC', 'WIO', 'NWC'),
                                   feature_group_count=D_XBC) + conv_b
    xbc = jax.nn.silu(xbc.astype(jnp.float32))
    xs, bm, cm = jnp.split(xbc, [D_SSM, D_SSM + D_BC], axis=-1)
    xs = xs.reshape(b, l, N_HEADS_SSM, SSM_HEAD_DIM)
    bm = bm.reshape(b, l, N_GROUPS_SSM, D_STATE)
    cm = cm.reshape(b, l, N_GROUPS_SSM, D_STATE)
    dt_raw = dt_raw.astype(jnp.float32)
    dt_f = jax.nn.softplus(dt_raw[..., :N_HEADS_SSM] + dt_bias_f.astype(jnp.float32))
    dt_b = jax.nn.softplus(dt_raw[..., N_HEADS_SSM:] + dt_bias_b.astype(jnp.float32))
    a_f = -jnp.exp(a_log_f.astype(jnp.float32))
    a_b = -jnp.exp(a_log_b.astype(jnp.float32))
    y_f = ssd_chunked(xs, dt_f, a_f, bm, cm)
    flip = lambda t: jnp.flip(t, axis=1)
    y_b = flip(ssd_chunked(flip(xs), flip(dt_b), a_b, flip(bm), flip(cm)))
    y = y_f + y_b + xs * d_skip.astype(jnp.float32)[:, None]
    y = y.reshape(b, l, D_SSM) * jax.nn.silu(z.astype(jnp.float32))
    yg = y.reshape(b, l, N_GROUPS_SSM, D_SSM // N_GROUPS_SSM)
    yg = yg * lax.rsqrt(jnp.mean(yg * yg, axis=-1, keepdims=True) + EPS)
    return (yg.reshape(b, l, D_SSM) * ssm_norm_g.astype(jnp.float32)).astype(z.dtype)


def hierarchical_moe(h, w_router_group, b_router_group, w_router_expert, b_router_expert, w_gate, w_up, w_down):
    t, d = h.shape
    g_logits = jnp.dot(h, w_router_group, preferred_element_type=jnp.float32) + b_router_group.astype(jnp.float32)
    g_prob = jax.nn.softmax(g_logits, axis=-1)
    g_p, g_idx = lax.top_k(g_prob, 1)
    e_logits = jnp.dot(h, w_router_expert, preferred_element_type=jnp.float32) + b_router_expert.astype(jnp.float32)
    e_logits = e_logits.reshape(t, N_EXPERT_GROUPS, EXPERTS_PER_GROUP)
    e_logits = jnp.take_along_axis(e_logits, g_idx[:, :, None], axis=1)[:, 0]
    e_prob = jax.nn.softmax(e_logits, axis=-1)
    e_p, e_idx = lax.top_k(e_prob, TOP_K)
    gates = g_p * e_p / jnp.sum(e_p, axis=-1, keepdims=True)
    expert_id = (g_idx * EXPERTS_PER_GROUP + e_idx).astype(jnp.int32)
    n_assign = t * TOP_K
    flat_e = expert_id.reshape(n_assign)
    flat_tok = jnp.repeat(jnp.arange(t, dtype=jnp.int32), TOP_K)
    flat_w = gates.reshape(n_assign)
    order = jnp.argsort(flat_e)
    sorted_e = flat_e[order]
    counts = jnp.bincount(flat_e, length=N_EXPERTS).astype(jnp.int32)
    padded = (counts + MOE_BLOCK - 1) // MOE_BLOCK * MOE_BLOCK
    start = jnp.cumsum(counts) - counts
    pad_end = jnp.cumsum(padded)
    pad_start = pad_end - padded
    dest = pad_start[sorted_e] + jnp.arange(n_assign, dtype=jnp.int32) - start[sorted_e]
    n_blocks = -(-n_assign // MOE_BLOCK) + N_EXPERTS
    n_rows = n_blocks * MOE_BLOCK
    row_tok = jnp.full((n_rows,), t, jnp.int32).at[dest].set(flat_tok[order])
    row_w = jnp.zeros((n_rows,), jnp.float32).at[dest].set(flat_w[order])
    block_e = jnp.minimum(jnp.searchsorted(pad_end, jnp.arange(n_blocks, dtype=jnp.int32) * MOE_BLOCK,
                                           side='right'), N_EXPERTS - 1).astype(jnp.int32)
    h_pad = jnp.concatenate([h, jnp.zeros((1, d), h.dtype)], axis=0)
    xb = h_pad[row_tok].reshape(n_blocks, MOE_BLOCK, d)

    def expert_block(args):
        xblk, e = args
        hid = jax.nn.silu(xblk @ w_gate[e]) * (xblk @ w_up[e])
        return hid @ w_down[e]

    yb = lax.map(expert_block, (xb, block_e)).reshape(n_rows, d)
    out = jnp.zeros((t + 1, d), jnp.float32).at[row_tok].add(yb.astype(jnp.float32) * row_w[:, None])
    return out[:t].astype(h.dtype)


def encoder_layer(x, norm1_g, w_in, q_norm_g, k_norm_g, rpb, attn_out_g, conv_w, conv_b,
                  a_log_f, a_log_b, dt_bias_f, dt_bias_b, d_skip, ssm_norm_g, w_out, norm2_g,
                  w_router_group, b_router_group, w_router_expert, b_router_expert, w_gate, w_up, w_down):
    b, l, d = x.shape
    hn = rms_norm(x, norm1_g)
    proj = hn @ w_in
    q, k, v, z, xbc, dt_raw = jnp.split(
        proj, [D_ATTN, 2 * D_ATTN, 3 * D_ATTN, 3 * D_ATTN + D_SSM, 3 * D_ATTN + D_SSM + D_XBC], axis=-1)
    q = rms_norm(q.reshape(b, l, N_HEADS_ATTN, HEAD_DIM), q_norm_g)
    k = rms_norm(k.reshape(b, l, N_HEADS_ATTN, HEAD_DIM), k_norm_g)
    v = v.reshape(b, l, N_HEADS_ATTN, HEAD_DIM)
    attn = rms_norm(neighbourhood_attention(q, k, v, rpb), attn_out_g)
    ssm = ssd_mixer(z, xbc, dt_raw, conv_w, conv_b, a_log_f, a_log_b,
                    dt_bias_f, dt_bias_b, d_skip, ssm_norm_g)
    x = x + jnp.concatenate([attn, ssm], axis=-1) @ w_out
    hn = rms_norm(x, norm2_g)
    moe = hierarchical_moe(hn.reshape(b * l, d), w_router_group, b_router_group,
                           w_router_expert, b_router_expert, w_gate, w_up, w_down)
    return x + moe.reshape(b, l, d)


def trunk(x, norm1_g, w_in, q_norm_g, k_norm_g, rpb, attn_out_g, conv_w, conv_b,
          a_log_f, a_log_b, dt_bias_f, dt_bias_b, d_skip, ssm_norm_g, w_out, norm2_g,
          w_router_group, b_router_group, w_router_expert, b_router_expert, w_gate, w_up, w_down):
    for i in range(DEPTH):
        x = encoder_layer(x, norm1_g[i], w_in[i], q_norm_g[i], k_norm_g[i], rpb[i], attn_out_g[i],
                          conv_w[i], conv_b[i], a_log_f[i], a_log_b[i], dt_bias_f[i], dt_bias_b[i],
                          d_skip[i], ssm_norm_g[i], w_out[i], norm2_g[i], w_router_group[i],
                          b_router_group[i], w_router_expert[i], b_router_expert[i],
                          w_gate[i], w_up[i], w_down[i])
    return x


def setup_inputs(seed: int = 0) -> dict:
    key = jax.random.key(seed)
    ks = jax.random.split(key, 25)
    f32 = jnp.float32

    def nrm(k, shape, scale):
        return scale * jax.random.normal(k, shape, f32)

    def gain(k, shape):
        return 1.0 + 0.02 * jax.random.normal(k, shape, f32)

    def dt_bias(k):
        dt0 = jnp.exp(jax.random.uniform(k, (DEPTH, N_HEADS_SSM), f32, math.log(1e-3), math.log(1e-1)))
        return dt0 + jnp.log(-jnp.expm1(-dt0))

    def a_log(k):
        return jnp.log(jax.random.uniform(k, (DEPTH, N_HEADS_SSM), f32, 1.0, 16.0))

    return {
        "x_prompt": nrm(ks[0], (BATCH, SEQ, D_MODEL), 1.0),
        "x_sample": nrm(ks[1], (DEC_BATCH, DEC_SEQ, D_MODEL), 1.0),
        "norm1_g": gain(ks[2], (DEPTH, D_MODEL)),
        "w_in": nrm(ks[3], (DEPTH, D_MODEL, D_IN_PROJ), D_MODEL ** -0.5),
        "q_norm_g": gain(ks[4], (DEPTH, HEAD_DIM)),
        "k_norm_g": gain(ks[5], (DEPTH, HEAD_DIM)),
        "rpb": nrm(ks[6], (DEPTH, N_HEADS_ATTN, 2 * WIN_H_MAX - 1, 2 * WIN_W - 1), 0.02),
        "attn_out_g": gain(ks[7], (DEPTH, D_ATTN)),
        "conv_w": nrm(ks[8], (DEPTH, D_CONV, 1, D_XBC), D_CONV ** -0.5),
        "conv_b": nrm(ks[9], (DEPTH, D_XBC), 0.02),
        "a_log_f": a_log(ks[10]),
        "a_log_b": a_log(ks[11]),
        "dt_bias_f": dt_bias(ks[12]),
        "dt_bias_b": dt_bias(ks[13]),
        "d_skip": gain(ks[14], (DEPTH, N_HEADS_SSM)),
        "ssm_norm_g": gain(ks[15], (DEPTH, D_SSM)),
        "w_out": nrm(ks[16], (DEPTH, D_MIX, D_MODEL), D_MIX ** -0.5),
        "norm2_g": gain(ks[17], (DEPTH, D_MODEL)),
        "w_router_group": nrm(ks[18], (DEPTH, D_MODEL, N_EXPERT_GROUPS), D_MODEL ** -0.5),
        "b_router_group": nrm(ks[19], (DEPTH, N_EXPERT_GROUPS), 0.01),
        "w_router_expert": nrm(ks[20], (DEPTH, D_MODEL, N_EXPERTS), D_MODEL ** -0.5),
        "b_router_expert": nrm(ks[21], (DEPTH, N_EXPERTS), 0.01),
        "w_gate": nrm(ks[22], (DEPTH, N_EXPERTS, D_MODEL, D_EXPERT), D_MODEL ** -0.5),
        "w_up": nrm(ks[23], (DEPTH, N_EXPERTS, D_MODEL, D_EXPERT), D_MODEL ** -0.5),
        "w_down": nrm(ks[24], (DEPTH, N_EXPERTS, D_EXPERT, D_MODEL), D_EXPERT ** -0.5),
    }


def reference(x_prompt, x_sample, norm1_g, w_in, q_norm_g, k_norm_g, rpb, attn_out_g, conv_w, conv_b,
              a_log_f, a_log_b, dt_bias_f, dt_bias_b, d_skip, ssm_norm_g, w_out, norm2_g,
              w_router_group, b_router_group, w_router_expert, b_router_expert, w_gate, w_up, w_down):
    y_prompt = trunk(x_prompt, norm1_g, w_in, q_norm_g, k_norm_g, rpb, attn_out_g, conv_w, conv_b,
                     a_log_f, a_log_b, dt_bias_f, dt_bias_b, d_skip, ssm_norm_g, w_out, norm2_g,
                     w_router_group, b_router_group, w_router_expert, b_router_expert, w_gate, w_up, w_down)
    y_sample = trunk(x_sample, norm1_g, w_in, q_norm_g, k_norm_g, rpb, attn_out_g, conv_w, conv_b,
                     a_log_f, a_log_b, dt_bias_f, dt_bias_b, d_skip, ssm_norm_g, w_out, norm2_g,
                     w_router_group, b_router_group, w_router_expert, b_router_expert, w_gate, w_up, w_down)
    return (y_prompt, y_sample)
```

```python
import functools

import jax
import jax.numpy as jnp
from jax import lax
from jax.experimental import pallas as pl
from jax.experimental.pallas import tpu as pltpu

F32 = jnp.float32
BF16 = jnp.bfloat16
U32 = jnp.uint32
I32 = jnp.int32

EPS = 1e-6
GRID_W = 64
N_HEADS_ATTN = 16
HEAD_DIM = 64
D_ATTN = N_HEADS_ATTN * HEAD_DIM
WIN_H = 8
WIN_W = 16
N_HEADS_SSM = 16
SSM_HEAD_DIM = 64
D_SSM = N_HEADS_SSM * SSM_HEAD_DIM
N_GROUPS_SSM = 2
HEADS_PER_GROUP = N_HEADS_SSM // N_GROUPS_SSM
D_GROUP = D_SSM // N_GROUPS_SSM
D_STATE = 128
D_CONV = 5
CHUNK = 128
D_BC = N_GROUPS_SSM * D_STATE
N_EXPERT_GROUPS = 4
EXPERTS_PER_GROUP = 8
N_EXPERTS = N_EXPERT_GROUPS * EXPERTS_PER_GROUP
TOP_K = 2
D_EXPERT = 512

LANES = 128
BF16_ROWS = 16
CONV_HALO = BF16_ROWS
NEG = -1e30
VMEM_LIMIT_BYTES = 56 * 1024 * 1024
MOE_BLOCK = 256

COL_Q, COL_K, COL_V = 0, D_ATTN // LANES, 2 * D_ATTN // LANES
COL_Z = 3 * D_ATTN // D_GROUP
COL_XS = (3 * D_ATTN + D_SSM) // D_GROUP
COL_B = (3 * D_ATTN + 2 * D_SSM) // LANES
COL_C = COL_B + D_BC // LANES
D_PROJ_MAIN = 3 * D_ATTN + 2 * D_SSM + 2 * D_BC


def _cparams(n_axes):
    return pltpu.CompilerParams(dimension_semantics=("arbitrary",) * n_axes,
                                vmem_limit_bytes=VMEM_LIMIT_BYTES)


def _silu(x):
    return x * (1.0 / (1.0 + jnp.exp(-x)))


def _split3(x):
    hi = x.astype(BF16)
    r1 = x - hi.astype(F32)
    mid = r1.astype(BF16)
    lo = (r1 - mid.astype(F32)).astype(BF16)
    return hi, mid, lo


def _pack_bf16_pairs(x):
    n = x.shape[1] // 2
    u = lax.bitcast_convert_type(x.astype(BF16).astype(F32), U32)
    return (u[:, :n] >> 16) | u[:, n:]


def _unpack_lo(u):
    return lax.bitcast_convert_type(u << 16, F32)


def _unpack_hi(u):
    return lax.bitcast_convert_type(u & jnp.uint32(0xFFFF0000), F32)


def _inproj_body(x_ref, g_ref, w_ref, wdt_ref, o_ref, dt_ref, hn_ref):
    tm = x_ref.shape[0]
    rows = min(tm, 256)

    @pl.when(pl.program_id(1) == 0)
    def _():
        def norm_rows(i, c):
            r0 = pl.multiple_of(i * rows, rows)
            x = x_ref[pl.ds(r0, rows), :]
            ms = jnp.mean(x * x, axis=-1, keepdims=True)
            hn_ref[pl.ds(r0, rows), :] = (x * lax.rsqrt(ms + EPS) * g_ref[...]).astype(BF16)
            return c
        lax.fori_loop(0, tm // rows, norm_rows, 0)
        dt_ref[...] = jnp.dot(hn_ref[...], wdt_ref[...], preferred_element_type=F32)

    o_ref[...] = jnp.dot(hn_ref[...], w_ref[...], preferred_element_type=F32).astype(BF16)


def _inproj(x2d, gain, w_main, w_dt):
    t, d = x2d.shape
    n = w_main.shape[1]
    ndt = w_dt.shape[1]
    tm = min(1024, t)
    tn = 512
    return pl.pallas_call(
        _inproj_body,
        grid=(t // tm, n // tn),
        in_specs=[pl.BlockSpec((tm, d), lambda i, j: (i, 0)),
                  pl.BlockSpec((1, d), lambda i, j: (0, 0)),
                  pl.BlockSpec((d, tn), lambda i, j: (0, j)),
                  pl.BlockSpec((d, ndt), lambda i, j: (0, 0))],
        out_specs=[pl.BlockSpec((tm, tn), lambda i, j: (i, j)),
                   pl.BlockSpec((tm, ndt), lambda i, j: (i, 0))],
        out_shape=[jax.ShapeDtypeStruct((t, n), BF16), jax.ShapeDtypeStruct((t, ndt), F32)],
        scratch_shapes=[pltpu.VMEM((tm, d), BF16)],
        compiler_params=_cparams(2),
        name="inproj",
    )(x2d, gain, w_main, w_dt)


def _attn_body(q_ref, k_ref, v_ref, qg_ref, kg_ref, e_ref, bias_ref, o_ref, qs, ks):
    l = q_ref.shape[1]
    n_rows = l // GRID_W
    win_keys = WIN_H * GRID_W
    ch = min(l, 512)
    head_a = lax.broadcasted_iota(I32, (1, LANES), 1) < HEAD_DIM
    sel_a = jnp.where(head_a, 1.0, 0.0).astype(BF16)
    sel_b = jnp.where(head_a, 0.0, 1.0).astype(BF16)

    def norm_rows(i, c):
        r0 = pl.multiple_of(i * ch, ch)
        for src, gref, dst in ((q_ref, qg_ref, qs), (k_ref, kg_ref, ks)):
            x = src[0, pl.ds(r0, ch), :].astype(F32)
            ssq = jnp.dot((x * x).astype(BF16), e_ref[...], preferred_element_type=F32)
            dst[pl.ds(r0, ch), :] = (x * lax.rsqrt(ssq * (1.0 / HEAD_DIM) + EPS) * gref[...]).astype(BF16)
        return c
    lax.fori_loop(0, l // ch, norm_rows, 0)

    def one_row(r, c):
        rs = jnp.clip(r - WIN_H // 2, 0, n_rows - WIN_H)
        variant = rs - r + (WIN_H - 1)
        q_r = qs[pl.ds(pl.multiple_of(r * GRID_W, GRID_W), GRID_W), :]
        qm = jnp.concatenate([q_r * sel_a, q_r * sel_b], axis=0)
        k0 = pl.multiple_of(rs * GRID_W, GRID_W)
        kb = ks[pl.ds(k0, win_keys), :]
        vb = v_ref[0, pl.ds(k0, win_keys), :]
        s = lax.dot_general(qm, kb, (((1,), (1,)), ((), ())), preferred_element_type=F32)
        s = s + bias_ref[0, variant]
        m = jnp.max(s, axis=-1, keepdims=True)
        p = jnp.exp(s - m)
        den = jnp.sum(p, axis=-1, keepdims=True)
        o = jnp.dot(p.astype(BF16), vb, preferred_element_type=F32) * (1.0 / den)
        out = jnp.where(head_a, o[:GRID_W], o[GRID_W:])
        o_ref[0, pl.ds(pl.multiple_of(r * GRID_W, GRID_W), GRID_W), :] = out.astype(BF16)
        return c
    lax.fori_loop(0, n_rows, one_row, 0)


def _attention(proj3, qg2, kg2, e_mat, bias):
    b, l, _ = proj3.shape
    n_pairs = N_HEADS_ATTN // 2
    blk = (1, l, LANES)
    return pl.pallas_call(
        _attn_body,
        grid=(n_pairs, b),
        in_specs=[pl.BlockSpec(blk, lambda hp, bi: (bi, 0, COL_Q + hp)),
                  pl.BlockSpec(blk, lambda hp, bi: (bi, 0, COL_K + hp)),
                  pl.BlockSpec(blk, lambda hp, bi: (bi, 0, COL_V + hp)),
                  pl.BlockSpec((1, LANES), lambda hp, bi: (0, 0)),
                  pl.BlockSpec((1, LANES), lambda hp, bi: (0, 0)),
                  pl.BlockSpec((LANES, LANES), lambda hp, bi: (0, 0)),
                  pl.BlockSpec((1, WIN_H, LANES, WIN_H * GRID_W), lambda hp, bi: (hp, 0, 0, 0))],
        out_specs=pl.BlockSpec(blk, lambda hp, bi: (bi, 0, hp)),
        out_shape=jax.ShapeDtypeStruct((b, l, D_ATTN), BF16),
        scratch_shapes=[pltpu.VMEM((l, LANES), BF16), pltpu.VMEM((l, LANES), BF16)],
        compiler_params=_cparams(2),
        name="nbr_attention",
    )(proj3, proj3, proj3, qg2, kg2, e_mat, bias)


def _attention_bias(rpb):
    cols = jnp.arange(GRID_W, dtype=I32)
    col_start = jnp.clip(cols - WIN_W // 2, 0, GRID_W - WIN_W)
    keys = cols[None, :]
    valid = (keys >= col_start[:, None]) & (keys < col_start[:, None] + WIN_W)
    dc = jnp.clip(keys - cols[:, None] + (WIN_W - 1), 0, 2 * WIN_W - 2)
    tab = jnp.where(valid[None, None], rpb.astype(F32)[:, :, dc], NEG)
    per_variant = []
    for o in range(WIN_H):
        t = tab[:, o:o + WIN_H]
        per_variant.append(jnp.transpose(t, (0, 2, 1, 3)).reshape(N_HEADS_ATTN, GRID_W, WIN_H * GRID_W))
    full = jnp.stack(per_variant, axis=1)
    full = full.reshape(N_HEADS_ATTN // 2, 2, WIN_H, GRID_W, WIN_H * GRID_W)
    return jnp.transpose(full, (0, 2, 1, 3, 4)).reshape(N_HEADS_ATTN // 2, WIN_H, 2 * GRID_W, WIN_H * GRID_W)


def _ssd_body(z_ref, xs_ref, b_ref, c_ref, dt_ref, cw_ref, cb_ref, dtb_ref, arow_ref, dsk_ref, ng_ref,
              ef_ref, eb_ref, tri_ref, o_ref, xs_s, b_s, c_s, y_s, sf_s, sb_s):
    l = xs_ref.shape[1]
    n_chunks = l // CHUNK
    hpg = HEADS_PER_GROUP
    ii = lax.broadcasted_iota(I32, (CHUNK, CHUNK), 0)
    jj = lax.broadcasted_iota(I32, (CHUNK, CHUNK), 1)
    causal = ii >= jj
    anti = ii <= jj
    head_a = lax.broadcasted_iota(I32, (1, LANES), 1) < SSM_HEAD_DIM

    def conv_chunk(ref, r0, c0, width):
        cur = ref[0, pl.ds(r0, CHUNK), :].astype(F32)
        p0 = pl.multiple_of(jnp.maximum(r0 - CONV_HALO, 0), CONV_HALO)
        n0 = pl.multiple_of(jnp.minimum(r0 + CHUNK, l - CONV_HALO), CONV_HALO)
        prev = ref[0, pl.ds(p0, CONV_HALO), :].astype(F32) * jnp.where(r0 > 0, 1.0, 0.0)
        nxt = ref[0, pl.ds(n0, CONV_HALO), :].astype(F32) * jnp.where(r0 + CHUNK < l, 1.0, 0.0)
        ext = jnp.concatenate([prev, cur, nxt], axis=0)
        acc = jnp.broadcast_to(cb_ref[0, :, c0:c0 + width], (CHUNK, width))
        for k in range(D_CONV):
            off = CONV_HALO - D_CONV // 2 + k
            acc = acc + ext[off:off + CHUNK] * cw_ref[0, k:k + 1, c0:c0 + width]
        return _silu(acc)

    def dt_terms(r0):
        raw = dt_ref[0, pl.ds(r0, CHUNK), :] + dtb_ref[0]
        dtv = jnp.maximum(raw, 0.0) + jnp.log(1.0 + jnp.exp(-jnp.abs(raw)))
        adt = dtv * arow_ref[0]
        hi, mid, lo = _split3(adt)
        cs3 = jnp.dot(tri_ref[...], jnp.concatenate([hi, mid, lo], axis=1), preferred_element_type=F32)
        cs = cs3[:, :LANES] + cs3[:, LANES:2 * LANES] + cs3[:, 2 * LANES:]
        return dtv, cs[:CHUNK], cs[CHUNK:]

    def expand_exact(row, e_ref):
        hi, mid, lo = _split3(jnp.broadcast_to(row, (8, LANES)))
        e = e_ref[0]
        r = (jnp.dot(hi, e, preferred_element_type=F32) + jnp.dot(mid, e, preferred_element_type=F32)
             + jnp.dot(lo, e, preferred_element_type=F32))
        return r[0:1]

    def state_terms(cc, bt, x, dtv, cs, tot, e_ref, s_ref):
        e = e_ref[0]
        expcs = jnp.dot(jnp.exp(cs).astype(BF16), e, preferred_element_type=F32)
        y_off = jnp.dot(cc, s_ref[...].astype(BF16), preferred_element_type=F32) * expcs
        scl = jnp.dot((dtv * jnp.exp(tot - cs)).astype(BF16), e, preferred_element_type=F32)
        states_t = jnp.dot(bt, (x * scl).astype(BF16), preferred_element_type=F32)
        s_ref[...] = s_ref[...] * expand_exact(jnp.exp(tot), e_ref) + states_t
        return y_off

    sf_s[...] = jnp.zeros_like(sf_s)
    sb_s[...] = jnp.zeros_like(sb_s)

    def fwd_chunk(c, carry):
        r0 = pl.multiple_of(c * CHUNK, CHUNK)
        x = conv_chunk(xs_ref, r0, 0, D_GROUP)
        bm = conv_chunk(b_ref, r0, D_GROUP, D_STATE)
        cm = conv_chunk(c_ref, r0, D_GROUP + D_STATE, D_STATE)
        xb = x.astype(BF16)
        bb = bm.astype(BF16)
        cc = cm.astype(BF16)
        xs_s[pl.ds(r0, CHUNK), :] = xb
        b_s[pl.ds(r0, CHUNK), :] = bb
        c_s[pl.ds(r0, CHUNK), :] = cc
        dtv, cs_f, cs_b = dt_terms(r0)
        cst_f = cs_f.T
        cst_b = cs_b.T
        dtt = dtv.T
        cb = lax.dot_general(cc, bb, (((1,), (1,)), ((), ())), preferred_element_type=F32)
        pieces = []
        for pair in range(hpg // 2):
            xp = xb[:, pair * LANES:(pair + 1) * LANES]
            ys = []
            for hh in range(2):
                h = 2 * pair + hh
                df = cs_f[:, h:h + 1] - cst_f[h:h + 1, :]
                db = cs_b[:, hpg + h:hpg + h + 1] - cst_b[hpg + h:hpg + h + 1, :]
                lf = jnp.exp(jnp.where(causal, df, NEG)) * dtt[h:h + 1, :]
                lb = jnp.exp(jnp.where(anti, db, NEG)) * dtt[hpg + h:hpg + h + 1, :]
                m = (cb * (lf + lb)).astype(BF16)
                ys.append(jnp.dot(m, xp, preferred_element_type=F32))
            pieces.append(jnp.where(head_a, ys[0], ys[1]))
        y = jnp.concatenate(pieces, axis=1) + x * dsk_ref[0]
        y = y + state_terms(cc, bm.T.astype(BF16), x, dtv, cs_f, cs_f[CHUNK - 1:CHUNK, :], ef_ref, sf_s)
        y_s[pl.ds(r0, CHUNK), :] = y
        return carry
    lax.fori_loop(0, n_chunks, fwd_chunk, 0)

    def bwd_chunk(i, carry):
        c = n_chunks - 1 - i
        r0 = pl.multiple_of(c * CHUNK, CHUNK)
        x = xs_s[pl.ds(r0, CHUNK), :].astype(F32)
        bt = b_s[pl.ds(r0, CHUNK), :].astype(F32).T.astype(BF16)
        cc = c_s[pl.ds(r0, CHUNK), :]
        dtv, _, cs_b = dt_terms(r0)
        y = y_s[pl.ds(r0, CHUNK), :] + state_terms(cc, bt, x, dtv, cs_b, cs_b[0:1, :], eb_ref, sb_s)
        y = y * _silu(z_ref[0, pl.ds(r0, CHUNK), :].astype(F32))
        ms = jnp.mean(y * y, axis=-1, keepdims=True)
        o_ref[0, pl.ds(r0, CHUNK), :] = (y * lax.rsqrt(ms + EPS) * ng_ref[0]).astype(BF16)
        return carry
    lax.fori_loop(0, n_chunks, bwd_chunk, 0)


def _ssd(proj3, dt3, cw, cb, dtb, arow, dsk, ng, ef, eb, tri):
    b, l, _ = proj3.shape
    g = N_GROUPS_SSM
    wconv = D_GROUP + 2 * D_STATE
    per_group = lambda shape: pl.BlockSpec((1,) + shape, lambda bi, gi: (gi,) + (0,) * len(shape))
    return pl.pallas_call(
        _ssd_body,
        grid=(b, g),
        in_specs=[pl.BlockSpec((1, l, D_GROUP), lambda bi, gi: (bi, 0, COL_Z + gi)),
                  pl.BlockSpec((1, l, D_GROUP), lambda bi, gi: (bi, 0, COL_XS + gi)),
                  pl.BlockSpec((1, l, D_STATE), lambda bi, gi: (bi, 0, COL_B + gi)),
                  pl.BlockSpec((1, l, D_STATE), lambda bi, gi: (bi, 0, COL_C + gi)),
                  pl.BlockSpec((1, l, LANES), lambda bi, gi: (bi, 0, gi)),
                  per_group((D_CONV, wconv)), per_group((1, wconv)),
                  per_group((1, LANES)), per_group((1, LANES)),
                  per_group((1, D_GROUP)), per_group((1, D_GROUP)),
                  pl.BlockSpec((1, LANES, D_GROUP), lambda bi, gi: (0, 0, 0)),
                  pl.BlockSpec((1, LANES, D_GROUP), lambda bi, gi: (0, 0, 0)),
                  pl.BlockSpec((2 * CHUNK, CHUNK), lambda bi, gi: (0, 0))],
        out_specs=pl.BlockSpec((1, l, D_GROUP), lambda bi, gi: (bi, 0, gi)),
        out_shape=jax.ShapeDtypeStruct((b, l, D_SSM), BF16),
        scratch_shapes=[pltpu.VMEM((l, D_GROUP), BF16), pltpu.VMEM((l, D_STATE), BF16),
                        pltpu.VMEM((l, D_STATE), BF16), pltpu.VMEM((l, D_GROUP), F32),
                        pltpu.VMEM((D_STATE, D_GROUP), F32), pltpu.VMEM((D_STATE, D_GROUP), F32)],
        compiler_params=_cparams(2),
        name="ssd_mixer",
    )(proj3, proj3, proj3, proj3, dt3, cw, cb, dtb, arow, dsk, ng, ef, eb, tri)


def _outproj_body(x_ref, a_ref, s_ref, ag_ref, wa_ref, ws_ref, g2_ref, wr_ref, br_ref, x2_ref, hp_ref, lg_ref):
    a = a_ref[...].astype(F32)
    ms = jnp.mean(a * a, axis=-1, keepdims=True)
    an = (a * lax.rsqrt(ms + EPS) * ag_ref[...]).astype(BF16)
    y = (jnp.dot(an, wa_ref[...], preferred_element_type=F32)
         + jnp.dot(s_ref[...], ws_ref[...], preferred_element_type=F32))
    x2 = x_ref[...] + y
    x2_ref[...] = x2
    ms2 = jnp.mean(x2 * x2, axis=-1, keepdims=True)
    hn = x2 * lax.rsqrt(ms2 + EPS) * g2_ref[...]
    hi = hn.astype(BF16)
    lo = (hn - hi.astype(F32)).astype(BF16)
    l1 = jnp.dot(hi, wr_ref[...], preferred_element_type=F32)
    l2 = jnp.dot(lo, wr_ref[:, :LANES], preferred_element_type=F32)
    lg_ref[...] = l1[:, :LANES] + l1[:, LANES:] + l2 + br_ref[...]
    hp_ref[...] = _pack_bf16_pairs(hi.astype(F32))


def _outproj(x2d, attn2d, ssm2d, ag, wa, ws, g2, wr, br):
    t, d = x2d.shape
    tm = min(256, t)
    row = lambda w: pl.BlockSpec((tm, w), lambda i: (i, 0))
    full = lambda a: pl.BlockSpec(a.shape, lambda i: (0,) * a.ndim)
    return pl.pallas_call(
        _outproj_body,
        grid=(t // tm,),
        in_specs=[row(d), row(D_ATTN), row(D_SSM), full(ag), full(wa), full(ws), full(g2), full(wr), full(br)],
        out_specs=[row(d), row(d // 2), row(LANES)],
        out_shape=[jax.ShapeDtypeStruct((t, d), F32), jax.ShapeDtypeStruct((t, d // 2), U32),
                   jax.ShapeDtypeStruct((t, LANES), F32)],
        compiler_params=_cparams(1),
        name="outproj_router",
    )(x2d, attn2d, ssm2d, ag, wa, ws, g2, wr, br)


def _moe_body(be_ref, nused_ref, nvalid_ref, tok_ref, slot_ref, w_ref, x_hbm, wgu_ref, wd_ref, y_hbm,
              xbuf, ybuf, gsem, ssem):
    blk = xbuf.shape[0]
    half = xbuf.shape[1]
    n_valid = nvalid_ref[pl.program_id(0)]

    def gather_copy(i):
        return pltpu.make_async_copy(x_hbm.at[pl.ds(tok_ref[0, 0, i], 1)], xbuf.at[pl.ds(i, 1)], gsem)

    def scatter_copy(i):
        return pltpu.make_async_copy(ybuf.at[pl.ds(i, 1)], y_hbm.at[pl.ds(slot_ref[0, 0, i], 1)], ssem)

    @pl.when(pl.program_id(0) < nused_ref[0])
    def _():
        def start_gather(i, c):
            gather_copy(i).start()
            return c
        lax.fori_loop(0, blk, start_gather, 0)

        def wait_gather(i, c):
            gather_copy(i).wait()
            return c
        lax.fori_loop(0, blk, wait_gather, 0)

        xg = xbuf[...]
        xlo = _unpack_lo(xg).astype(BF16)
        xhi = _unpack_hi(xg).astype(BF16)
        gu = (jnp.dot(xlo, wgu_ref[0, :half, :], preferred_element_type=F32)
              + jnp.dot(xhi, wgu_ref[0, half:, :], preferred_element_type=F32))
        gate = gu[:, :D_EXPERT]
        hid = (_silu(gate) * gu[:, D_EXPERT:]).astype(BF16)
        y = jnp.dot(hid, wd_ref[0], preferred_element_type=F32) * w_ref[...]
        ybuf[...] = _pack_bf16_pairs(y)

        def start_scatter(i, c):
            scatter_copy(i).start()
            return c
        lax.fori_loop(0, n_valid, start_scatter, 0)

        def wait_scatter(i, c):
            scatter_copy(i).wait()
            return c
        lax.fori_loop(0, n_valid, wait_scatter, 0)


def _moe(block_e, n_used, n_valid, row_tok, row_slot, row_w, hn_packed, wgu, wd, n_slots):
    n_blocks = block_e.shape[0]
    blk = MOE_BLOCK
    half = hn_packed.shape[1]
    d = 2 * half
    smem_rows = pl.BlockSpec((1, 1, blk), lambda s, be, nu, nv: (s, 0, 0), memory_space=pltpu.SMEM)
    grid_spec = pltpu.PrefetchScalarGridSpec(
        num_scalar_prefetch=3,
        grid=(n_blocks,),
        in_specs=[smem_rows, smem_rows,
                  pl.BlockSpec((blk, 1), lambda s, be, nu, nv: (s, 0)),
                  pl.BlockSpec(memory_space=pl.ANY),
                  pl.BlockSpec((1, d, 2 * D_EXPERT), lambda s, be, nu, nv: (be[s], 0, 0)),
                  pl.BlockSpec((1, D_EXPERT, d), lambda s, be, nu, nv: (be[s], 0, 0))],
        out_specs=pl.BlockSpec(memory_space=pl.ANY),
        scratch_shapes=[pltpu.VMEM((blk, half), U32), pltpu.VMEM((blk, half), U32),
                        pltpu.SemaphoreType.DMA(()), pltpu.SemaphoreType.DMA(())],
    )
    return pl.pallas_call(
        _moe_body,
        grid_spec=grid_spec,
        out_shape=jax.ShapeDtypeStruct((n_slots, half), U32),
        compiler_params=_cparams(1),
        name="moe_experts",
    )(block_e, n_used, n_valid, row_tok.reshape(n_blocks, 1, blk), row_slot.reshape(n_blocks, 1, blk),
      row_w.reshape(n_blocks * blk, 1), hn_packed, wgu, wd)


def _route(logits, t):
    blk = MOE_BLOCK
    g_logits = logits[:, :N_EXPERT_GROUPS]
    e_logits = logits[:, N_EXPERT_GROUPS:N_EXPERT_GROUPS + N_EXPERTS].reshape(t, N_EXPERT_GROUPS, EXPERTS_PER_GROUP)
    g_prob = jax.nn.softmax(g_logits, axis=-1)
    g_p, g_idx = lax.top_k(g_prob, 1)
    e_sel = jnp.take_along_axis(e_logits, g_idx[:, :, None], axis=1)[:, 0]
    e_prob = jax.nn.softmax(e_sel, axis=-1)
    e_p, e_idx = lax.top_k(e_prob, TOP_K)
    gates = g_p * e_p / jnp.sum(e_p, axis=-1, keepdims=True)
    expert_id = (g_idx * EXPERTS_PER_GROUP + e_idx).astype(I32)
    n_assign = t * TOP_K
    flat_e = expert_id.reshape(n_assign)
    flat_slot = jnp.arange(n_assign, dtype=I32)
    flat_w = gates.reshape(n_assign)
    order = jnp.argsort(flat_e)
    sorted_e = flat_e[order]
    counts = jnp.bincount(flat_e, length=N_EXPERTS).astype(I32)
    padded = (counts + blk - 1) // blk * blk
    start = jnp.cumsum(counts) - counts
    pad_end = jnp.cumsum(padded)
    pad_start = pad_end - padded
    dest = pad_start[sorted_e] + jnp.arange(n_assign, dtype=I32) - start[sorted_e]
    n_blocks = n_assign // blk + N_EXPERTS
    n_rows = n_blocks * blk
    row_slot = jnp.zeros((n_rows,), I32).at[dest].set(flat_slot[order])
    row_tok = jnp.zeros((n_rows,), I32).at[dest].set(flat_slot[order] // TOP_K)
    row_w = jnp.zeros((n_rows,), F32).at[dest].set(flat_w[order])
    block_e = jnp.minimum(jnp.searchsorted(pad_end, jnp.arange(n_blocks, dtype=I32) * blk, side='right'),
                          N_EXPERTS - 1).astype(I32)
    n_used = (pad_end[-1:] // blk).astype(I32)
    n_valid = jnp.zeros((n_rows,), I32).at[dest].set(1).reshape(n_blocks, blk).sum(axis=1)
    return block_e, n_used, n_valid, row_tok, row_slot, row_w, n_assign


def _combine_body(x2_ref, y_ref, o_ref):
    half = o_ref.shape[1] // 2
    y0 = y_ref[:, :half]
    y1 = y_ref[:, half:]
    o_ref[:, :half] = x2_ref[:, :half] + _unpack_lo(y0) + _unpack_lo(y1)
    o_ref[:, half:] = x2_ref[:, half:] + _unpack_hi(y0) + _unpack_hi(y1)


def _combine(x2, y_slots):
    t, d = x2.shape
    tm = min(512, t)
    y2 = y_slots.reshape(y_slots.shape[0] // TOP_K, d)
    return pl.pallas_call(
        _combine_body,
        grid=(t // tm,),
        in_specs=[pl.BlockSpec((tm, d), lambda i: (i, 0)), pl.BlockSpec((tm, d), lambda i: (i, 0))],
        out_specs=pl.BlockSpec((tm, d), lambda i: (i, 0)),
        out_shape=jax.ShapeDtypeStruct((t, d), F32),
        compiler_params=_cparams(1),
        name="moe_combine",
    )(x2, y2)


def _prepare(norm1_g, w_in, q_norm_g, k_norm_g, rpb, attn_out_g, conv_w, conv_b, a_log_f, a_log_b,
             dt_bias_f, dt_bias_b, d_skip, ssm_norm_g, w_out, norm2_g, w_router_group, b_router_group,
             w_router_expert, b_router_expert, w_gate, w_up, w_down):
    d = w_in.shape[0]
    hpg = HEADS_PER_GROUP
    p = {}
    p["norm1_g"] = norm1_g.reshape(1, d).astype(F32)
    p["w_main"] = w_in[:, :D_PROJ_MAIN].astype(BF16)
    w_dt = w_in[:, D_PROJ_MAIN:]
    zeros = jnp.zeros((d, LANES - 2 * hpg), w_in.dtype)
    per_group = lambda v, g: v[..., g * hpg:(g + 1) * hpg]
    p["w_dt"] = jnp.concatenate(
        [jnp.concatenate([per_group(w_dt[:, :N_HEADS_SSM], g), per_group(w_dt[:, N_HEADS_SSM:], g), zeros], axis=1)
         for g in range(N_GROUPS_SSM)], axis=1).astype(BF16)
    lane_rows = lambda f, bwd: jnp.stack(
        [jnp.concatenate([per_group(f, g), per_group(bwd, g), jnp.zeros((LANES - 2 * hpg,), F32)])
         for g in range(N_GROUPS_SSM)])[:, None, :]
    p["dtb"] = lane_rows(dt_bias_f.astype(F32), dt_bias_b.astype(F32))
    p["arow"] = lane_rows(-jnp.exp(a_log_f.astype(F32)), -jnp.exp(a_log_b.astype(F32)))
    scale = HEAD_DIM ** -0.5
    p["qg2"] = (jnp.tile(q_norm_g.astype(F32), 2) * scale).reshape(1, LANES)
    p["kg2"] = jnp.tile(k_norm_g.astype(F32), 2).reshape(1, LANES)
    lane = jnp.arange(LANES)
    p["e_mat"] = (lane[:, None] // HEAD_DIM == lane[None, :] // HEAD_DIM).astype(BF16)
    p["bias"] = _attention_bias(rpb)
    p["attn_out_g"] = attn_out_g.reshape(1, D_ATTN).astype(F32)
    cw = conv_w.reshape(D_CONV, -1).astype(F32)
    cbias = conv_b.reshape(1, -1).astype(F32)
    group_cols = lambda a, g: jnp.concatenate(
        [a[:, g * D_GROUP:(g + 1) * D_GROUP],
         a[:, D_SSM + g * D_STATE:D_SSM + (g + 1) * D_STATE],
         a[:, D_SSM + D_BC + g * D_STATE:D_SSM + D_BC + (g + 1) * D_STATE]], axis=1)
    p["cw"] = jnp.stack([group_cols(cw, g) for g in range(N_GROUPS_SSM)])
    p["cb"] = jnp.stack([group_cols(cbias, g) for g in range(N_GROUPS_SSM)])
    p["dsk"] = jnp.repeat(d_skip.astype(F32), SSM_HEAD_DIM).reshape(N_GROUPS_SSM, 1, D_GROUP)
    p["ng"] = ssm_norm_g.astype(F32).reshape(N_GROUPS_SSM, 1, D_GROUP)
    col_head = jnp.arange(D_GROUP) // SSM_HEAD_DIM
    p["ef"] = (lane[:, None] == col_head[None, :]).astype(BF16)[None]
    p["eb"] = (lane[:, None] == col_head[None, :] + hpg).astype(BF16)[None]
    pos = jnp.arange(CHUNK)
    p["tri"] = jnp.concatenate([pos[:, None] >= pos[None, :], pos[:, None] <= pos[None, :]], axis=0).astype(BF16)
    p["wa"] = w_out[:D_ATTN].astype(BF16)
    p["ws"] = w_out[D_ATTN:].astype(BF16)
    p["norm2_g"] = norm2_g.reshape(1, d).astype(F32)
    n_r = N_EXPERT_GROUPS + N_EXPERTS
    wr = jnp.concatenate([w_router_group, w_router_expert, jnp.zeros((d, LANES - n_r), F32)], axis=1).astype(F32)
    wr_hi = wr.astype(BF16)
    wr_lo = (wr - wr_hi.astype(F32)).astype(BF16)
    p["wr"] = jnp.concatenate([wr_hi, wr_lo], axis=1)
    p["br"] = jnp.concatenate([b_router_group, b_router_expert, jnp.zeros((LANES - n_r,), F32)]).reshape(1, LANES)
    p["wgu"] = jnp.concatenate([w_gate.astype(BF16), w_up.astype(BF16)], axis=2)
    p["wd"] = w_down.astype(BF16)
    return p


def _layer(x, p):
    b, l, d = x.shape
    t = b * l
    x2d = x.reshape(t, d)
    proj, dt = _inproj(x2d, p["norm1_g"], p["w_main"], p["w_dt"])
    proj3 = proj.reshape(b, l, D_PROJ_MAIN)
    attn = _attention(proj3, p["qg2"], p["kg2"], p["e_mat"], p["bias"])
    ssm = _ssd(proj3, dt.reshape(b, l, N_GROUPS_SSM * LANES), p["cw"], p["cb"], p["dtb"], p["arow"],
               p["dsk"], p["ng"], p["ef"], p["eb"], p["tri"])
    x2, hn_packed, logits = _outproj(x2d, attn.reshape(t, D_ATTN), ssm.reshape(t, D_SSM), p["attn_out_g"],
                                     p["wa"], p["ws"], p["norm2_g"], p["wr"], p["br"])
    block_e, n_used, n_valid, row_tok, row_slot, row_w, n_slots = _route(logits, t)
    y_slots = _moe(block_e, n_used, n_valid, row_tok, row_slot, row_w, hn_packed, p["wgu"], p["wd"], n_slots)
    return _combine(x2, y_slots).reshape(b, l, d)


def kernel(x_prompt, x_sample, norm1_g, w_in, q_norm_g, k_norm_g, rpb, attn_out_g, conv_w, conv_b, a_log_f,
           a_log_b, dt_bias_f, dt_bias_b, d_skip, ssm_norm_g, w_out, norm2_g, w_router_group, b_router_group,
           w_router_expert, b_router_expert, w_gate, w_up, w_down):
    weights = (norm1_g, w_in, q_norm_g, k_norm_g, rpb, attn_out_g, conv_w, conv_b, a_log_f, a_log_b, dt_bias_f,
               dt_bias_b, d_skip, ssm_norm_g, w_out, norm2_g, w_router_group, b_router_group, w_router_expert,
               b_router_expert, w_gate, w_up, w_down)
    assert all(w.shape[0] == 1 for w in weights), "one layer of stacked weights expected"
    p = _prepare(*(w[0] for w in weights))
    return (_layer(x_prompt, p), _layer(x_sample, p))
```

```python
import jax
import jax.numpy as jnp
from jax import lax
from jax.experimental import pallas as pl
from jax.experimental.pallas import tpu as pltpu

F32 = jnp.float32
BF16 = jnp.bfloat16
U32 = jnp.uint32
I32 = jnp.int32

EPS = 1e-6
GRID_W = 64
N_HEADS_ATTN = 16
HEAD_DIM = 64
D_ATTN = N_HEADS_ATTN * HEAD_DIM
WIN_H = 8
WIN_W = 16
N_HEADS_SSM = 16
SSM_HEAD_DIM = 64
D_SSM = N_HEADS_SSM * SSM_HEAD_DIM
N_GROUPS_SSM = 2
HEADS_PER_GROUP = N_HEADS_SSM // N_GROUPS_SSM
D_GROUP = D_SSM // N_GROUPS_SSM
D_STATE = 128
D_CONV = 5
CHUNK = 128
D_BC = N_GROUPS_SSM * D_STATE
N_EXPERT_GROUPS = 4
EXPERTS_PER_GROUP = 8
N_EXPERTS = N_EXPERT_GROUPS * EXPERTS_PER_GROUP
TOP_K = 2
D_EXPERT = 512

LANES = 128
BF16_ROWS = 16
CONV_HALO = BF16_ROWS
NEG = -1e30
VMEM_LIMIT_BYTES = 56 * 1024 * 1024
MOE_BLOCK = 256

COL_Q, COL_K, COL_V = 0, D_ATTN // LANES, 2 * D_ATTN // LANES
COL_Z = 3 * D_ATTN // D_GROUP
COL_XS = (3 * D_ATTN + D_SSM) // D_GROUP
COL_B = (3 * D_ATTN + 2 * D_SSM) // LANES
COL_C = COL_B + D_BC // LANES
D_PROJ_MAIN = 3 * D_ATTN + 2 * D_SSM + 2 * D_BC


def _cparams(n_axes):
    return pltpu.CompilerParams(dimension_semantics=("arbitrary",) * n_axes,
                                vmem_limit_bytes=VMEM_LIMIT_BYTES)


def _silu(x):
    return x * (1.0 / (1.0 + jnp.exp(-x)))


def _split3(x):
    hi = x.astype(BF16)
    r1 = x - hi.astype(F32)
    mid = r1.astype(BF16)
    lo = (r1 - mid.astype(F32)).astype(BF16)
    return hi, mid, lo


def _pack_bf16_pairs(x):
    n = x.shape[1] // 2
    u = lax.bitcast_convert_type(x.astype(BF16).astype(F32), U32)
    return (u[:, :n] >> 16) | u[:, n:]


def _unpack_lo(u):
    return lax.bitcast_convert_type(u << 16, F32)


def _unpack_hi(u):
    return lax.bitcast_convert_type(u & jnp.uint32(0xFFFF0000), F32)


def _inproj_body(x_ref, g_ref, w_ref, wdt_ref, o_ref, dt_ref, hn_ref):
    tm = x_ref.shape[0]
    rows = min(tm, 256)

    @pl.when(pl.program_id(1) == 0)
    def _():
        def norm_rows(i, c):
            r0 = pl.multiple_of(i * rows, rows)
            x = x_ref[pl.ds(r0, rows), :]
            ms = jnp.mean(x * x, axis=-1, keepdims=True)
            hn_ref[pl.ds(r0, rows), :] = (x * lax.rsqrt(ms + EPS) * g_ref[...]).astype(BF16)
            return c
        lax.fori_loop(0, tm // rows, norm_rows, 0)
        dt_ref[...] = jnp.dot(hn_ref[...], wdt_ref[...], preferred_element_type=F32)

    o_ref[...] = jnp.dot(hn_ref[...], w_ref[...], preferred_element_type=F32).astype(BF16)


def _inproj(x2d, gain, w_main, w_dt):
    t, d = x2d.shape
    n = w_main.shape[1]
    ndt = w_dt.shape[1]
    tm = min(1024, t)
    tn = 512
    return pl.pallas_call(
        _inproj_body,
        grid=(t // tm, n // tn),
        in_specs=[pl.BlockSpec((tm, d), lambda i, j: (i, 0)),
                  pl.BlockSpec((1, d), lambda i, j: (0, 0)),
                  pl.BlockSpec((d, tn), lambda i, j: (0, j)),
                  pl.BlockSpec((d, ndt), lambda i, j: (0, 0))],
        out_specs=[pl.BlockSpec((tm, tn), lambda i, j: (i, j)),
                   pl.BlockSpec((tm, ndt), lambda i, j: (i, 0))],
        out_shape=[jax.ShapeDtypeStruct((t, n), BF16), jax.ShapeDtypeStruct((t, ndt), F32)],
        scratch_shapes=[pltpu.VMEM((tm, d), BF16)],
        compiler_params=_cparams(2),
        name="inproj",
    )(x2d, gain, w_main, w_dt)


def _attn_body(q_ref, k_ref, v_ref, qg_ref, kg_ref, e_ref, bias_ref, o_ref, qs, ks):
    l = q_ref.shape[1]
    n_rows = l // GRID_W
    win_keys = WIN_H * GRID_W
    ch = min(l, 512)
    head_a = lax.broadcasted_iota(I32, (1, LANES), 1) < HEAD_DIM
    sel_a = jnp.where(head_a, 1.0, 0.0).astype(BF16)
    sel_b = jnp.where(head_a, 0.0, 1.0).astype(BF16)

    def norm_rows(i, c):
        r0 = pl.multiple_of(i * ch, ch)
        for src, gref, dst in ((q_ref, qg_ref, qs), (k_ref, kg_ref, ks)):
            x = src[0, pl.ds(r0, ch), :].astype(F32)
            ssq = jnp.dot((x * x).astype(BF16), e_ref[...], preferred_element_type=F32)
            dst[pl.ds(r0, ch), :] = (x * lax.rsqrt(ssq * (1.0 / HEAD_DIM) + EPS) * gref[...]).astype(BF16)
        return c
    lax.fori_loop(0, l // ch, norm_rows, 0)

    def one_row(r, c):
        rs = jnp.clip(r - WIN_H // 2, 0, n_rows - WIN_H)
        variant = rs - r + (WIN_H - 1)
        q_r = qs[pl.ds(pl.multiple_of(r * GRID_W, GRID_W), GRID_W), :]
        qm = jnp.concatenate([q_r * sel_a, q_r * sel_b], axis=0)
        k0 = pl.multiple_of(rs * GRID_W, GRID_W)
        kb = ks[pl.ds(k0, win_keys), :]
        vb = v_ref[0, pl.ds(k0, win_keys), :]
        s = lax.dot_general(qm, kb, (((1,), (1,)), ((), ())), preferred_element_type=F32)
        s = s + bias_ref[0, variant]
        m = jnp.max(s, axis=-1, keepdims=True)
        p = jnp.exp(s - m)
        den = jnp.sum(p, axis=-1, keepdims=True)
        o = jnp.dot(p.astype(BF16), vb, preferred_element_type=F32) * (1.0 / den)
        out = jnp.where(head_a, o[:GRID_W], o[GRID_W:])
        o_ref[0, pl.ds(pl.multiple_of(r * GRID_W, GRID_W), GRID_W), :] = out.astype(BF16)
        return c
    lax.fori_loop(0, n_rows, one_row, 0)


def _attention(proj3, qg2, kg2, e_mat, bias):
    b, l, _ = proj3.shape
    n_pairs = N_HEADS_ATTN // 2
    blk = (1, l, LANES)
    return pl.pallas_call(
        _attn_body,
        grid=(n_pairs, b),
        in_specs=[pl.BlockSpec(blk, lambda hp, bi: (bi, 0, COL_Q + hp)),
                  pl.BlockSpec(blk, lambda hp, bi: (bi, 0, COL_K + hp)),
                  pl.BlockSpec(blk, lambda hp, bi: (bi, 0, COL_V + hp)),
                  pl.BlockSpec((1, LANES), lambda hp, bi: (0, 0)),
                  pl.BlockSpec((1, LANES), lambda hp, bi: (0, 0)),
                  pl.BlockSpec((LANES, LANES), lambda hp, bi: (0, 0)),
                  pl.BlockSpec((1, WIN_H, LANES, WIN_H * GRID_W), lambda hp, bi: (hp, 0, 0, 0))],
        out_specs=pl.BlockSpec(blk, lambda hp, bi: (bi, 0, hp)),
        out_shape=jax.ShapeDtypeStruct((b, l, D_ATTN), BF16),
        scratch_shapes=[pltpu.VMEM((l, LANES), BF16), pltpu.VMEM((l, LANES), BF16)],
        compiler_params=_cparams(2),
        name="nbr_attention",
    )(proj3, proj3, proj3, qg2, kg2, e_mat, bias)


def _attention_bias(rpb):
    cols = jnp.arange(GRID_W, dtype=I32)
    col_start = jnp.clip(cols - WIN_W // 2, 0, GRID_W - WIN_W)
    keys = cols[None, :]
    valid = (keys >= col_start[:, None]) & (keys < col_start[:, None] + WIN_W)
    dc = jnp.clip(keys - cols[:, None] + (WIN_W - 1), 0, 2 * WIN_W - 2)
    tab = jnp.where(valid[None, None], rpb.astype(F32)[:, :, dc], NEG)
    per_variant = []
    for o in range(WIN_H):
        t = tab[:, o:o + WIN_H]
        per_variant.append(jnp.transpose(t, (0, 2, 1, 3)).reshape(N_HEADS_ATTN, GRID_W, WIN_H * GRID_W))
    full = jnp.stack(per_variant, axis=1)
    full = full.reshape(N_HEADS_ATTN // 2, 2, WIN_H, GRID_W, WIN_H * GRID_W)
    return jnp.transpose(full, (0, 2, 1, 3, 4)).reshape(N_HEADS_ATTN // 2, WIN_H, 2 * GRID_W, WIN_H * GRID_W)


def _ssd_body(z_ref, xs_ref, b_ref, c_ref, dt_ref, cw_ref, cb_ref, dtb_ref, arow_ref, dsk_ref, ng_ref,
              ef_ref, eb_ref, tri_ref, o_ref, xs_s, b_s, c_s, y_s, sf_s, sb_s):
    l = xs_ref.shape[1]
    n_chunks = l // CHUNK
    hpg = HEADS_PER_GROUP
    ii = lax.broadcasted_iota(I32, (CHUNK, CHUNK), 0)
    jj = lax.broadcasted_iota(I32, (CHUNK, CHUNK), 1)
    causal = ii >= jj
    anti = ii <= jj
    head_a = lax.broadcasted_iota(I32, (1, LANES), 1) < SSM_HEAD_DIM

    def conv_chunk(ref, r0, c0, width):
        cur = ref[0, pl.ds(r0, CHUNK), :].astype(F32)
        p0 = pl.multiple_of(jnp.maximum(r0 - CONV_HALO, 0), CONV_HALO)
        n0 = pl.multiple_of(jnp.minimum(r0 + CHUNK, l - CONV_HALO), CONV_HALO)
        prev = ref[0, pl.ds(p0, CONV_HALO), :].astype(F32) * jnp.where(r0 > 0, 1.0, 0.0)
        nxt = ref[0, pl.ds(n0, CONV_HALO), :].astype(F32) * jnp.where(r0 + CHUNK < l, 1.0, 0.0)
        ext = jnp.concatenate([prev, cur, nxt], axis=0)
        acc = jnp.broadcast_to(cb_ref[0, :, c0:c0 + width], (CHUNK, width))
        for k in range(D_CONV):
            off = CONV_HALO - D_CONV // 2 + k
            acc = acc + ext[off:off + CHUNK] * cw_ref[0, k:k + 1, c0:c0 + width]
        return _silu(acc)

    def dt_terms(r0):
        raw = dt_ref[0, pl.ds(r0, CHUNK), :] + dtb_ref[0]
        dtv = jnp.maximum(raw, 0.0) + jnp.log(1.0 + jnp.exp(-jnp.abs(raw)))
        adt = dtv * arow_ref[0]
        hi, mid, lo = _split3(adt)
        cs3 = jnp.dot(tri_ref[...], jnp.concatenate([hi, mid, lo], axis=1), preferred_element_type=F32)
        cs = cs3[:, :LANES] + cs3[:, LANES:2 * LANES] + cs3[:, 2 * LANES:]
        return dtv, cs[:CHUNK], cs[CHUNK:]

    def expand_exact(row, e_ref):
        hi, mid, lo = _split3(jnp.broadcast_to(row, (8, LANES)))
        e = e_ref[0]
        r = (jnp.dot(hi, e, preferred_element_type=F32) + jnp.dot(mid, e, preferred_element_type=F32)
             + jnp.dot(lo, e, preferred_element_type=F32))
        return r[0:1]

    def state_terms(cc, bt, x, dtv, cs, tot, e_ref, s_ref):
        e = e_ref[0]
        expcs = jnp.dot(jnp.exp(cs).astype(BF16), e, preferred_element_type=F32)
        y_off = jnp.dot(cc, s_ref[...].astype(BF16), preferred_element_type=F32) * expcs
        scl = jnp.dot((dtv * jnp.exp(tot - cs)).astype(BF16), e, preferred_element_type=F32)
        states_t = jnp.dot(bt, (x * scl).astype(BF16), preferred_element_type=F32)
        s_ref[...] = s_ref[...] * expand_exact(jnp.exp(tot), e_ref) + states_t
        return y_off

    sf_s[...] = jnp.zeros_like(sf_s)
    sb_s[...] = jnp.zeros_like(sb_s)

    def fwd_chunk(c, carry):
        r0 = pl.multiple_of(c * CHUNK, CHUNK)
        x = conv_chunk(xs_ref, r0, 0, D_GROUP)
        bm = conv_chunk(b_ref, r0, D_GROUP, D_STATE)
        cm = conv_chunk(c_ref, r0, D_GROUP + D_STATE, D_STATE)
        xb = x.astype(BF16)
        bb = bm.astype(BF16)
        cc = cm.astype(BF16)
        xs_s[pl.ds(r0, CHUNK), :] = xb
        b_s[pl.ds(r0, CHUNK), :] = bb
        c_s[pl.ds(r0, CHUNK), :] = cc
        dtv, cs_f, cs_b = dt_terms(r0)
        cst_f = cs_f.T
        cst_b = cs_b.T
        dtt = dtv.T
        cb = lax.dot_general(cc, bb, (((1,), (1,)), ((), ())), preferred_element_type=F32)
        pieces = []
        for pair in range(hpg // 2):
            xp = xb[:, pair * LANES:(pair + 1) * LANES]
            ys = []
            for hh in range(2):
                h = 2 * pair + hh
                df = cs_f[:, h:h + 1] - cst_f[h:h + 1, :]
                db = cs_b[:, hpg + h:hpg + h + 1] - cst_b[hpg + h:hpg + h + 1, :]
                lf = jnp.exp(jnp.where(causal, df, NEG)) * dtt[h:h + 1, :]
                lb = jnp.exp(jnp.where(anti, db, NEG)) * dtt[hpg + h:hpg + h + 1, :]
                m = (cb * (lf + lb)).astype(BF16)
                ys.append(jnp.dot(m, xp, preferred_element_type=F32))
            pieces.append(jnp.where(head_a, ys[0], ys[1]))
        y = jnp.concatenate(pieces, axis=1) + x * dsk_ref[0]
        y = y + state_terms(cc, bm.T.astype(BF16), x, dtv, cs_f, cs_f[CHUNK - 1:CHUNK, :], ef_ref, sf_s)
        y_s[pl.ds(r0, CHUNK), :] = y
        return carry
    lax.fori_loop(0, n_chunks, fwd_chunk, 0)

    def bwd_chunk(i, carry):
        c = n_chunks - 1 - i
        r0 = pl.multiple_of(c * CHUNK, CHUNK)
        x = xs_s[pl.ds(r0, CHUNK), :].astype(F32)
        bt = b_s[pl.ds(r0, CHUNK), :].astype(F32).T.astype(BF16)
        cc = c_s[pl.ds(r0, CHUNK), :]
        dtv, _, cs_b = dt_terms(r0)
        y = y_s[pl.ds(r0, CHUNK), :] + state_terms(cc, bt, x, dtv, cs_b, cs_b[0:1, :], eb_ref, sb_s)
        y = y * _silu(z_ref[0, pl.ds(r0, CHUNK), :].astype(F32))
        ms = jnp.mean(y * y, axis=-1, keepdims=True)
        o_ref[0, pl.ds(r0, CHUNK), :] = (y * lax.rsqrt(ms + EPS) * ng_ref[0]).astype(BF16)
        return carry
    lax.fori_loop(0, n_chunks, bwd_chunk, 0)


def _ssd(proj3, dt3, cw, cb, dtb, arow, dsk, ng, ef, eb, tri):
    b, l, _ = proj3.shape
    g = N_GROUPS_SSM
    wconv = D_GROUP + 2 * D_STATE
    per_group = lambda shape: pl.BlockSpec((1,) + shape, lambda bi, gi: (gi,) + (0,) * len(shape))
    return pl.pallas_call(
        _ssd_body,
        grid=(b, g),
        in_specs=[pl.BlockSpec((1, l, D_GROUP), lambda bi, gi: (bi, 0, COL_Z + gi)),
                  pl.BlockSpec((1, l, D_GROUP), lambda bi, gi: (bi, 0, COL_XS + gi)),
                  pl.BlockSpec((1, l, D_STATE), lambda bi, gi: (bi, 0, COL_B + gi)),
                  pl.BlockSpec((1, l, D_STATE), lambda bi, gi: (bi, 0, COL_C + gi)),
                  pl.BlockSpec((1, l, LANES), lambda bi, gi: (bi, 0, gi)),
                  per_group((D_CONV, wconv)), per_group((1, wconv)),
                  per_group((1, LANES)), per_group((1, LANES)),
                  per_group((1, D_GROUP)), per_group((1, D_GROUP)),
                  pl.BlockSpec((1, LANES, D_GROUP), lambda bi, gi: (0, 0, 0)),
                  pl.BlockSpec((1, LANES, D_GROUP), lambda bi, gi: (0, 0, 0)),
                  pl.BlockSpec((2 * CHUNK, CHUNK), lambda bi, gi: (0, 0))],
        out_specs=pl.BlockSpec((1, l, D_GROUP), lambda bi, gi: (bi, 0, gi)),
        out_shape=jax.ShapeDtypeStruct((b, l, D_SSM), BF16),
        scratch_shapes=[pltpu.VMEM((l, D_GROUP), BF16), pltpu.VMEM((l, D_STATE), BF16),
                        pltpu.VMEM((l, D_STATE), BF16), pltpu.VMEM((l, D_GROUP), F32),
                        pltpu.VMEM((D_STATE, D_GROUP), F32), pltpu.VMEM((D_STATE, D_GROUP), F32)],
        compiler_params=_cparams(2),
        name="ssd_mixer",
    )(proj3, proj3, proj3, proj3, dt3, cw, cb, dtb, arow, dsk, ng, ef, eb, tri)


def _route_rows(lg):
    lane = lax.broadcasted_iota(I32, lg.shape, 1).astype(F32)
    n_g = float(N_EXPERT_GROUPS)
    n_e = float(EXPERTS_PER_GROUP)
    gl = jnp.where(lane < n_g, lg, NEG)
    gmax = jnp.max(gl, axis=-1, keepdims=True)
    g_idx = jnp.min(jnp.where(gl == gmax, lane, float(LANES)), axis=-1, keepdims=True)
    g_p = 1.0 / jnp.sum(jnp.exp(gl - gmax), axis=-1, keepdims=True)
    first = n_g + n_e * g_idx
    el = jnp.where((lane >= first) & (lane < first + n_e), lg, NEG)
    e1 = jnp.max(el, axis=-1, keepdims=True)
    i1 = jnp.min(jnp.where(el == e1, lane, float(LANES)), axis=-1, keepdims=True)
    el2 = jnp.where(lane == i1, NEG, el)
    e2 = jnp.max(el2, axis=-1, keepdims=True)
    i2 = jnp.min(jnp.where(el2 == e2, lane, float(LANES)), axis=-1, keepdims=True)
    r = jnp.exp(e2 - e1)
    gate1 = g_p / (1.0 + r)
    gate2 = gate1 * r
    return jnp.where(lane == 0.0, i1 - n_g,
                     jnp.where(lane == 1.0, i2 - n_g,
                               jnp.where(lane == 2.0, gate1, jnp.where(lane == 3.0, gate2, 0.0))))


def _outproj_body(x_ref, a_ref, s_ref, ag_ref, wa_ref, ws_ref, g2_ref, wr_ref, br_ref, x2_ref, hp_ref, rt_ref):
    a = a_ref[...].astype(F32)
    ms = jnp.mean(a * a, axis=-1, keepdims=True)
    an = (a * lax.rsqrt(ms + EPS) * ag_ref[...]).astype(BF16)
    y = (jnp.dot(an, wa_ref[...], preferred_element_type=F32)
         + jnp.dot(s_ref[...], ws_ref[...], preferred_element_type=F32))
    x2 = x_ref[...] + y
    x2_ref[...] = x2
    ms2 = jnp.mean(x2 * x2, axis=-1, keepdims=True)
    hn = x2 * lax.rsqrt(ms2 + EPS) * g2_ref[...]
    hi = hn.astype(BF16)
    lo = (hn - hi.astype(F32)).astype(BF16)
    l1 = jnp.dot(hi, wr_ref[...], preferred_element_type=F32)
    l2 = jnp.dot(lo, wr_ref[:, :LANES], preferred_element_type=F32)
    rt_ref[...] = _route_rows(l1[:, :LANES] + l1[:, LANES:] + l2 + br_ref[...])
    hp_ref[...] = _pack_bf16_pairs(hi.astype(F32))


def _outproj(x2d, attn2d, ssm2d, ag, wa, ws, g2, wr, br):
    t, d = x2d.shape
    tm = min(256, t)
    row = lambda w: pl.BlockSpec((tm, w), lambda i: (i, 0))
    full = lambda a: pl.BlockSpec(a.shape, lambda i: (0,) * a.ndim)
    return pl.pallas_call(
        _outproj_body,
        grid=(t // tm,),
        in_specs=[row(d), row(D_ATTN), row(D_SSM), full(ag), full(wa), full(ws), full(g2), full(wr), full(br)],
        out_specs=[row(d), row(d // 2), row(LANES)],
        out_shape=[jax.ShapeDtypeStruct((t, d), F32), jax.ShapeDtypeStruct((t, d // 2), U32),
                   jax.ShapeDtypeStruct((t, LANES), F32)],
        compiler_params=_cparams(1),
        name="outproj_router",
    )(x2d, attn2d, ssm2d, ag, wa, ws, g2, wr, br)


def _moe_body(vblk_ref, vexp_ref, vlo_ref, vhi_ref, tok_ref, slot_ref, x_hbm, wgu_ref, wd_ref, y_hbm,
              xbuf, ybuf, gsem, ssem):
    blk = xbuf.shape[0]
    half = xbuf.shape[1]
    v = pl.program_id(0)
    lo = vlo_ref[v]
    hi = vhi_ref[v]

    def gather_copy(i):
        return pltpu.make_async_copy(x_hbm.at[pl.ds(tok_ref[0, 0, i], 1)], xbuf.at[pl.ds(i, 1)], gsem)

    def scatter_copy(i):
        return pltpu.make_async_copy(ybuf.at[pl.ds(i, 1)], y_hbm.at[pl.ds(slot_ref[0, 0, i], 1)], ssem)

    def for_rows(fn):
        def body(i, c):
            fn(i)
            return c
        lax.fori_loop(0, blk, body, 0)

    @pl.when(hi > lo)
    def _():
        @pl.when(lo == 0)
        def _():
            for_rows(lambda i: gather_copy(i).start())
            for_rows(lambda i: gather_copy(i).wait())

        xg = xbuf[...]
        xlo = _unpack_lo(xg).astype(BF16)
        xhi = _unpack_hi(xg).astype(BF16)
        gu = (jnp.dot(xlo, wgu_ref[0, :half, :], preferred_element_type=F32)
              + jnp.dot(xhi, wgu_ref[0, half:, :], preferred_element_type=F32))
        gate = gu[:, :D_EXPERT]
        hid = (_silu(gate) * gu[:, D_EXPERT:]).astype(BF16)
        y = _pack_bf16_pairs(jnp.dot(hid, wd_ref[0], preferred_element_type=F32))

        @pl.when(lo == 0)
        def _():
            ybuf[...] = y

        @pl.when(lo > 0)
        def _():
            row = lax.broadcasted_iota(I32, (blk, 1), 0)
            ybuf[...] = jnp.where((row >= lo) & (row < hi), y, ybuf[...])

        @pl.when(hi == blk)
        def _():
            for_rows(lambda i: scatter_copy(i).start())
            for_rows(lambda i: scatter_copy(i).wait())


def _moe(visits, row_tok, row_slot, hn_packed, wgu, wd):
    blk = MOE_BLOCK
    n_assign = row_tok.shape[0]
    n_blocks = n_assign // blk
    n_visits = visits[0].shape[0]
    half = hn_packed.shape[1]
    d = 2 * half
    smem_rows = pl.BlockSpec((1, 1, blk), lambda v, vb, ve, vl, vh: (vb[v], 0, 0), memory_space=pltpu.SMEM)
    grid_spec = pltpu.PrefetchScalarGridSpec(
        num_scalar_prefetch=4,
        grid=(n_visits,),
        in_specs=[smem_rows, smem_rows,
                  pl.BlockSpec(memory_space=pl.ANY),
                  pl.BlockSpec((1, d, 2 * D_EXPERT), lambda v, vb, ve, vl, vh: (ve[v], 0, 0)),
                  pl.BlockSpec((1, D_EXPERT, d), lambda v, vb, ve, vl, vh: (ve[v], 0, 0))],
        out_specs=pl.BlockSpec(memory_space=pl.ANY),
        scratch_shapes=[pltpu.VMEM((blk, half), U32), pltpu.VMEM((blk, half), U32),
                        pltpu.SemaphoreType.DMA(()), pltpu.SemaphoreType.DMA(())],
    )
    return pl.pallas_call(
        _moe_body,
        grid_spec=grid_spec,
        out_shape=jax.ShapeDtypeStruct((n_assign, half), U32),
        compiler_params=_cparams(1),
        name="moe_experts",
    )(*visits, row_tok.reshape(n_blocks, 1, blk), row_slot.reshape(n_blocks, 1, blk), hn_packed, wgu, wd)


def _route_tables(route, t):
    blk = MOE_BLOCK
    n_assign = t * TOP_K
    n_blocks = n_assign // blk
    flat_e = route[:, :TOP_K].astype(I32).T.reshape(n_assign)
    order = jnp.argsort(flat_e, stable=True).astype(I32)
    counts = jnp.sum((flat_e[:, None] == jnp.arange(N_EXPERTS, dtype=I32)[None, :]).astype(I32), axis=0)
    ends = jnp.cumsum(counts)
    starts = ends - counts
    cuts = jnp.sort(jnp.concatenate([jnp.arange(n_blocks, dtype=I32) * blk, starts[1:]]))
    nxt = jnp.concatenate([cuts[1:], jnp.full((1,), n_assign, I32)])
    vblk = jnp.minimum(cuts // blk, n_blocks - 1)
    vexp = jnp.minimum(jnp.sum((ends[None, :] <= cuts[:, None]).astype(I32), axis=1), N_EXPERTS - 1)
    vlo = cuts - vblk * blk
    vhi = jnp.maximum(jnp.minimum(nxt, (vblk + 1) * blk) - vblk * blk, vlo)
    return (vblk, vexp, vlo, vhi), order % t, order


def _combine_body(x2_ref, y0_ref, y1_ref, rt_ref, o_ref):
    half = o_ref.shape[1] // 2
    g0 = rt_ref[:, TOP_K:TOP_K + 1]
    g1 = rt_ref[:, TOP_K + 1:TOP_K + 2]
    y0 = y0_ref[...]
    y1 = y1_ref[...]
    o_ref[:, :half] = x2_ref[:, :half] + g0 * _unpack_lo(y0) + g1 * _unpack_lo(y1)
    o_ref[:, half:] = x2_ref[:, half:] + g0 * _unpack_hi(y0) + g1 * _unpack_hi(y1)


def _combine(x2, y_slots, route):
    t, d = x2.shape
    tm = min(512, t)
    nb = t // tm
    return pl.pallas_call(
        _combine_body,
        grid=(nb,),
        in_specs=[pl.BlockSpec((tm, d), lambda i: (i, 0)),
                  pl.BlockSpec((tm, d // 2), lambda i: (i, 0)),
                  pl.BlockSpec((tm, d // 2), lambda i: (i + nb, 0)),
                  pl.BlockSpec((tm, LANES), lambda i: (i, 0))],
        out_specs=pl.BlockSpec((tm, d), lambda i: (i, 0)),
        out_shape=jax.ShapeDtypeStruct((t, d), F32),
        compiler_params=_cparams(1),
        name="moe_combine",
    )(x2, y_slots, y_slots, route)


def _prepare(norm1_g, w_in, q_norm_g, k_norm_g, rpb, attn_out_g, conv_w, conv_b, a_log_f, a_log_b,
             dt_bias_f, dt_bias_b, d_skip, ssm_norm_g, w_out, norm2_g, w_router_group, b_router_group,
             w_router_expert, b_router_expert, w_gate, w_up, w_down):
    d = w_in.shape[0]
    hpg = HEADS_PER_GROUP
    p = {}
    p["norm1_g"] = norm1_g.reshape(1, d).astype(F32)
    p["w_main"] = w_in[:, :D_PROJ_MAIN].astype(BF16)
    w_dt = w_in[:, D_PROJ_MAIN:]
    zeros = jnp.zeros((d, LANES - 2 * hpg), w_in.dtype)
    per_group = lambda v, g: v[..., g * hpg:(g + 1) * hpg]
    p["w_dt"] = jnp.concatenate(
        [jnp.concatenate([per_group(w_dt[:, :N_HEADS_SSM], g), per_group(w_dt[:, N_HEADS_SSM:], g), zeros], axis=1)
         for g in range(N_GROUPS_SSM)], axis=1).astype(BF16)
    lane_rows = lambda f, bwd: jnp.stack(
        [jnp.concatenate([per_group(f, g), per_group(bwd, g), jnp.zeros((LANES - 2 * hpg,), F32)])
         for g in range(N_GROUPS_SSM)])[:, None, :]
    p["dtb"] = lane_rows(dt_bias_f.astype(F32), dt_bias_b.astype(F32))
    p["arow"] = lane_rows(-jnp.exp(a_log_f.astype(F32)), -jnp.exp(a_log_b.astype(F32)))
    scale = HEAD_DIM ** -0.5
    p["qg2"] = (jnp.tile(q_norm_g.astype(F32), 2) * scale).reshape(1, LANES)
    p["kg2"] = jnp.tile(k_norm_g.astype(F32), 2).reshape(1, LANES)
    lane = jnp.arange(LANES)
    p["e_mat"] = (lane[:, None] // HEAD_DIM == lane[None, :] // HEAD_DIM).astype(BF16)
    p["bias"] = _attention_bias(rpb)
    p["attn_out_g"] = attn_out_g.reshape(1, D_ATTN).astype(F32)
    cw = conv_w.reshape(D_CONV, -1).astype(F32)
    cbias = conv_b.reshape(1, -1).astype(F32)
    group_cols = lambda a, g: jnp.concatenate(
        [a[:, g * D_GROUP:(g + 1) * D_GROUP],
         a[:, D_SSM + g * D_STATE:D_SSM + (g + 1) * D_STATE],
         a[:, D_SSM + D_BC + g * D_STATE:D_SSM + D_BC + (g + 1) * D_STATE]], axis=1)
    p["cw"] = jnp.stack([group_cols(cw, g) for g in range(N_GROUPS_SSM)])
    p["cb"] = jnp.stack([group_cols(cbias, g) for g in range(N_GROUPS_SSM)])
    p["dsk"] = jnp.repeat(d_skip.astype(F32), SSM_HEAD_DIM).reshape(N_GROUPS_SSM, 1, D_GROUP)
    p["ng"] = ssm_norm_g.astype(F32).reshape(N_GROUPS_SSM, 1, D_GROUP)
    col_head = jnp.arange(D_GROUP) // SSM_HEAD_DIM
    p["ef"] = (lane[:, None] == col_head[None, :]).astype(BF16)[None]
    p["eb"] = (lane[:, None] == col_head[None, :] + hpg).astype(BF16)[None]
    pos = jnp.arange(CHUNK)
    p["tri"] = jnp.concatenate([pos[:, None] >= pos[None, :], pos[:, None] <= pos[None, :]], axis=0).astype(BF16)
    p["wa"] = w_out[:D_ATTN].astype(BF16)
    p["ws"] = w_out[D_ATTN:].astype(BF16)
    p["norm2_g"] = norm2_g.reshape(1, d).astype(F32)
    n_r = N_EXPERT_GROUPS + N_EXPERTS
    wr = jnp.concatenate([w_router_group, w_router_expert, jnp.zeros((d, LANES - n_r), F32)], axis=1).astype(F32)
    wr_hi = wr.astype(BF16)
    wr_lo = (wr - wr_hi.astype(F32)).astype(BF16)
    p["wr"] = jnp.concatenate([wr_hi, wr_lo], axis=1)
    p["br"] = jnp.concatenate([b_router_group, b_router_expert, jnp.zeros((LANES - n_r,), F32)]).reshape(1, LANES)
    p["wgu"] = jnp.concatenate([w_gate.astype(BF16), w_up.astype(BF16)], axis=2)
    p["wd"] = w_down.astype(BF16)
    return p


def _layer(x, p):
    b, l, d = x.shape
    t = b * l
    x2d = x.reshape(t, d)
    proj, dt = _inproj(x2d, p["norm1_g"], p["w_main"], p["w_dt"])
    proj3 = proj.reshape(b, l, D_PROJ_MAIN)
    attn = _attention(proj3, p["qg2"], p["kg2"], p["e_mat"], p["bias"])
    ssm = _ssd(proj3, dt.reshape(b, l, N_GROUPS_SSM * LANES), p["cw"], p["cb"], p["dtb"], p["arow"],
               p["dsk"], p["ng"], p["ef"], p["eb"], p["tri"])
    x2, hn_packed, route = _outproj(x2d, attn.reshape(t, D_ATTN), ssm.reshape(t, D_SSM), p["attn_out_g"],
                                    p["wa"], p["ws"], p["norm2_g"], p["wr"], p["br"])
    visits, row_tok, row_slot = _route_tables(route, t)
    y_slots = _moe(visits, row_tok, row_slot, hn_packed, p["wgu"], p["wd"])
    return _combine(x2, y_slots, route).reshape(b, l, d)


def kernel(x_prompt, x_sample, norm1_g, w_in, q_norm_g, k_norm_g, rpb, attn_out_g, conv_w, conv_b, a_log_f,
           a_log_b, dt_bias_f, dt_bias_b, d_skip, ssm_norm_g, w_out, norm2_g, w_router_group, b_router_group,
           w_router_expert, b_router_expert, w_gate, w_up, w_down):
    weights = (norm1_g, w_in, q_norm_g, k_norm_g, rpb, attn_out_g, conv_w, conv_b, a_log_f, a_log_b, dt_bias_f,
               dt_bias_b, d_skip, ssm_norm_g, w_out, norm2_g, w_router_group, b_router_group, w_router_expert,
               b_router_expert, w_gate, w_up, w_down)
    assert all(w.shape[0] == 1 for w in weights), "one layer of stacked weights expected"
    p = _prepare(*(w[0] for w in weights))
    return (_layer(x_prompt, p), _layer(x_sample, p))
```

```python
import jax
import jax.numpy as jnp
from jax import lax
from jax.experimental import pallas as pl
from jax.experimental.pallas import tpu as pltpu

F32 = jnp.float32
BF16 = jnp.bfloat16
U32 = jnp.uint32
I32 = jnp.int32

EPS = 1e-6
GRID_W = 64
N_HEADS_ATTN = 16
HEAD_DIM = 64
D_ATTN = N_HEADS_ATTN * HEAD_DIM
WIN_H = 8
WIN_W = 16
N_HEADS_SSM = 16
SSM_HEAD_DIM = 64
D_SSM = N_HEADS_SSM * SSM_HEAD_DIM
N_GROUPS_SSM = 2
HEADS_PER_GROUP = N_HEADS_SSM // N_GROUPS_SSM
D_GROUP = D_SSM // N_GROUPS_SSM
D_STATE = 128
D_CONV = 5
CHUNK = 128
D_BC = N_GROUPS_SSM * D_STATE
N_EXPERT_GROUPS = 4
EXPERTS_PER_GROUP = 8
N_EXPERTS = N_EXPERT_GROUPS * EXPERTS_PER_GROUP
TOP_K = 2
D_EXPERT = 512

LANES = 128
BF16_ROWS = 16
CONV_HALO = BF16_ROWS
NEG = -1e30
VMEM_LIMIT_BYTES = 56 * 1024 * 1024
MOE_BLOCK = 256
ATTN_ROW_UNROLL = 8
MOE_DMA_UNROLL = 8

COL_Q, COL_K, COL_V = 0, D_ATTN // LANES, 2 * D_ATTN // LANES
COL_Z = 3 * D_ATTN // D_GROUP
COL_XS = (3 * D_ATTN + D_SSM) // D_GROUP
COL_B = (3 * D_ATTN + 2 * D_SSM) // LANES
COL_C = COL_B + D_BC // LANES
D_PROJ_MAIN = 3 * D_ATTN + 2 * D_SSM + 2 * D_BC


def _cparams(n_axes):
    return pltpu.CompilerParams(dimension_semantics=("arbitrary",) * n_axes,
                                vmem_limit_bytes=VMEM_LIMIT_BYTES)


def _silu(x):
    return x * (1.0 / (1.0 + jnp.exp(-x)))


def _split3(x):
    hi = x.astype(BF16)
    r1 = x - hi.astype(F32)
    mid = r1.astype(BF16)
    lo = (r1 - mid.astype(F32)).astype(BF16)
    return hi, mid, lo


def _pack_bf16_pairs(x):
    n = x.shape[1] // 2
    u = lax.bitcast_convert_type(x.astype(BF16).astype(F32), U32)
    return (u[:, :n] >> 16) | u[:, n:]


def _unpack_lo(u):
    return lax.bitcast_convert_type(u << 16, F32)


def _unpack_hi(u):
    return lax.bitcast_convert_type(u & jnp.uint32(0xFFFF0000), F32)


def _inproj_body(x_ref, g_ref, w_ref, wdt_ref, o_ref, dt_ref, hn_ref):
    tm = x_ref.shape[0]
    rows = min(tm, 256)

    @pl.when(pl.program_id(1) == 0)
    def _():
        def norm_rows(i, c):
            r0 = pl.multiple_of(i * rows, rows)
            x = x_ref[pl.ds(r0, rows), :]
            ms = jnp.mean(x * x, axis=-1, keepdims=True)
            hn_ref[pl.ds(r0, rows), :] = (x * lax.rsqrt(ms + EPS) * g_ref[...]).astype(BF16)
            return c
        lax.fori_loop(0, tm // rows, norm_rows, 0)
        dt_ref[...] = jnp.dot(hn_ref[...], wdt_ref[...], preferred_element_type=F32)

    o_ref[...] = jnp.dot(hn_ref[...], w_ref[...], preferred_element_type=F32).astype(BF16)


def _inproj(x2d, gain, w_main, w_dt):
    t, d = x2d.shape
    n = w_main.shape[1]
    ndt = w_dt.shape[1]
    tm = min(1024, t)
    tn = 512
    return pl.pallas_call(
        _inproj_body,
        grid=(t // tm, n // tn),
        in_specs=[pl.BlockSpec((tm, d), lambda i, j: (i, 0)),
                  pl.BlockSpec((1, d), lambda i, j: (0, 0)),
                  pl.BlockSpec((d, tn), lambda i, j: (0, j)),
                  pl.BlockSpec((d, ndt), lambda i, j: (0, 0))],
        out_specs=[pl.BlockSpec((tm, tn), lambda i, j: (i, j)),
                   pl.BlockSpec((tm, ndt), lambda i, j: (i, 0))],
        out_shape=[jax.ShapeDtypeStruct((t, n), BF16), jax.ShapeDtypeStruct((t, ndt), F32)],
        scratch_shapes=[pltpu.VMEM((tm, d), BF16)],
        compiler_params=_cparams(2),
        name="inproj",
    )(x2d, gain, w_main, w_dt)


def _attn_body(q_ref, k_ref, v_ref, qg_ref, kg_ref, e_ref, bias_ref, o_ref, qs, ks, s_scr, m_scr):
    l = q_ref.shape[1]
    n_rows = l // GRID_W
    win_keys = WIN_H * GRID_W
    ch = min(l, 512)
    head_a = lax.broadcasted_iota(I32, (1, LANES), 1) < HEAD_DIM
    sel_a = jnp.where(head_a, 1.0, 0.0).astype(BF16)
    sel_b = jnp.where(head_a, 0.0, 1.0).astype(BF16)

    def norm_rows(i, c):
        r0 = pl.multiple_of(i * ch, ch)
        for src, gref, dst in ((q_ref, qg_ref, qs), (k_ref, kg_ref, ks)):
            x = src[0, pl.ds(r0, ch), :].astype(F32)
            ssq = jnp.dot((x * x).astype(BF16), e_ref[...], preferred_element_type=F32)
            dst[pl.ds(r0, ch), :] = (x * lax.rsqrt(ssq * (1.0 / HEAD_DIM) + EPS) * gref[...]).astype(BF16)
        return c
    lax.fori_loop(0, l // ch, norm_rows, 0)

    def key_start(r):
        return jnp.clip(r - WIN_H // 2, 0, n_rows - WIN_H)

    def score_stage(r, slot):
        rs = key_start(r)
        q_r = qs[pl.ds(pl.multiple_of(r * GRID_W, GRID_W), GRID_W), :]
        qm = jnp.concatenate([q_r * sel_a, q_r * sel_b], axis=0)
        kb = ks[pl.ds(pl.multiple_of(rs * GRID_W, GRID_W), win_keys), :]
        s = lax.dot_general(qm, kb, (((1,), (1,)), ((), ())), preferred_element_type=F32)
        s = s + bias_ref[0, rs - r + (WIN_H - 1)]
        s_scr[slot] = s
        m_scr[slot] = jnp.max(s, axis=-1, keepdims=True)

    def value_stage(r, slot):
        rs = key_start(r)
        vb = v_ref[0, pl.ds(pl.multiple_of(rs * GRID_W, GRID_W), win_keys), :]
        p = jnp.exp(s_scr[slot] - m_scr[slot])
        den = jnp.sum(p, axis=-1, keepdims=True)
        o = jnp.dot(p.astype(BF16), vb, preferred_element_type=F32) * (1.0 / den)
        out = jnp.where(head_a, o[:GRID_W], o[GRID_W:])
        o_ref[0, pl.ds(pl.multiple_of(r * GRID_W, GRID_W), GRID_W), :] = out.astype(BF16)

    score_stage(0, 0)

    def row_group(i, c):
        r = ATTN_ROW_UNROLL * i
        for j in range(ATTN_ROW_UNROLL):
            score_stage(jnp.minimum(r + j + 1, n_rows - 1), (j + 1) % 2)
            value_stage(r + j, j % 2)
        return c
    lax.fori_loop(0, n_rows // ATTN_ROW_UNROLL, row_group, 0)


def _attention(proj3, qg2, kg2, e_mat, bias):
    b, l, _ = proj3.shape
    n_pairs = N_HEADS_ATTN // 2
    blk = (1, l, LANES)
    return pl.pallas_call(
        _attn_body,
        grid=(n_pairs, b),
        in_specs=[pl.BlockSpec(blk, lambda hp, bi: (bi, 0, COL_Q + hp)),
                  pl.BlockSpec(blk, lambda hp, bi: (bi, 0, COL_K + hp)),
                  pl.BlockSpec(blk, lambda hp, bi: (bi, 0, COL_V + hp)),
                  pl.BlockSpec((1, LANES), lambda hp, bi: (0, 0)),
                  pl.BlockSpec((1, LANES), lambda hp, bi: (0, 0)),
                  pl.BlockSpec((LANES, LANES), lambda hp, bi: (0, 0)),
                  pl.BlockSpec((1, WIN_H, LANES, WIN_H * GRID_W), lambda hp, bi: (hp, 0, 0, 0))],
        out_specs=pl.BlockSpec(blk, lambda hp, bi: (bi, 0, hp)),
        out_shape=jax.ShapeDtypeStruct((b, l, D_ATTN), BF16),
        scratch_shapes=[pltpu.VMEM((l, LANES), BF16), pltpu.VMEM((l, LANES), BF16),
                        pltpu.VMEM((2, LANES, WIN_H * GRID_W), F32), pltpu.VMEM((2, LANES, 1), F32)],
        compiler_params=_cparams(2),
        name="nbr_attention",
    )(proj3, proj3, proj3, qg2, kg2, e_mat, bias)


def _attention_bias(rpb):
    cols = jnp.arange(GRID_W, dtype=I32)
    col_start = jnp.clip(cols - WIN_W // 2, 0, GRID_W - WIN_W)
    keys = cols[None, :]
    valid = (keys >= col_start[:, None]) & (keys < col_start[:, None] + WIN_W)
    dc = jnp.clip(keys - cols[:, None] + (WIN_W - 1), 0, 2 * WIN_W - 2)
    tab = jnp.where(valid[None, None], rpb.astype(F32)[:, :, dc], NEG)
    per_variant = []
    for o in range(WIN_H):
        t = tab[:, o:o + WIN_H]
        per_variant.append(jnp.transpose(t, (0, 2, 1, 3)).reshape(N_HEADS_ATTN, GRID_W, WIN_H * GRID_W))
    full = jnp.stack(per_variant, axis=1)
    full = full.reshape(N_HEADS_ATTN // 2, 2, WIN_H, GRID_W, WIN_H * GRID_W)
    return jnp.transpose(full, (0, 2, 1, 3, 4)).reshape(N_HEADS_ATTN // 2, WIN_H, 2 * GRID_W, WIN_H * GRID_W)


def _ssd_body(z_ref, xs_ref, b_ref, c_ref, dt_ref, cw_ref, cb_ref, dtb_ref, arow_ref, dsk_ref, ng_ref,
              ef_ref, eb_ref, tri_ref, o_ref, xs_s, b_s, c_s, y_s, sf_s, sb_s):
    l = xs_ref.shape[1]
    n_chunks = l // CHUNK
    hpg = HEADS_PER_GROUP
    ii = lax.broadcasted_iota(I32, (CHUNK, CHUNK), 0)
    jj = lax.broadcasted_iota(I32, (CHUNK, CHUNK), 1)
    causal = ii >= jj
    anti = ii <= jj
    head_a = lax.broadcasted_iota(I32, (1, LANES), 1) < SSM_HEAD_DIM

    def conv_chunk(ref, r0, c0, width):
        cur = ref[0, pl.ds(r0, CHUNK), :].astype(F32)
        p0 = pl.multiple_of(jnp.maximum(r0 - CONV_HALO, 0), CONV_HALO)
        n0 = pl.multiple_of(jnp.minimum(r0 + CHUNK, l - CONV_HALO), CONV_HALO)
        prev = ref[0, pl.ds(p0, CONV_HALO), :].astype(F32) * jnp.where(r0 > 0, 1.0, 0.0)
        nxt = ref[0, pl.ds(n0, CONV_HALO), :].astype(F32) * jnp.where(r0 + CHUNK < l, 1.0, 0.0)
        ext = jnp.concatenate([prev, cur, nxt], axis=0)
        acc = jnp.broadcast_to(cb_ref[0, :, c0:c0 + width], (CHUNK, width))
        for k in range(D_CONV):
            off = CONV_HALO - D_CONV // 2 + k
            acc = acc + ext[off:off + CHUNK] * cw_ref[0, k:k + 1, c0:c0 + width]
        return _silu(acc)

    def dt_terms(r0):
        raw = dt_ref[0, pl.ds(r0, CHUNK), :] + dtb_ref[0]
        dtv = jnp.maximum(raw, 0.0) + jnp.log(1.0 + jnp.exp(-jnp.abs(raw)))
        adt = dtv * arow_ref[0]
        hi, mid, lo = _split3(adt)
        cs3 = jnp.dot(tri_ref[...], jnp.concatenate([hi, mid, lo], axis=1), preferred_element_type=F32)
        cs = cs3[:, :LANES] + cs3[:, LANES:2 * LANES] + cs3[:, 2 * LANES:]
        return dtv, cs[:CHUNK], cs[CHUNK:]

    def expand_exact(row, e_ref):
        hi, mid, lo = _split3(jnp.broadcast_to(row, (8, LANES)))
        e = e_ref[0]
        r = (jnp.dot(hi, e, preferred_element_type=F32) + jnp.dot(mid, e, preferred_element_type=F32)
             + jnp.dot(lo, e, preferred_element_type=F32))
        return r[0:1]

    def state_terms(cc, bt, x, dtv, cs, tot, e_ref, s_ref):
        e = e_ref[0]
        expcs = jnp.dot(jnp.exp(cs).astype(BF16), e, preferred_element_type=F32)
        y_off = jnp.dot(cc, s_ref[...].astype(BF16), preferred_element_type=F32) * expcs
        scl = jnp.dot((dtv * jnp.exp(tot - cs)).astype(BF16), e, preferred_element_type=F32)
        states_t = jnp.dot(bt, (x * scl).astype(BF16), preferred_element_type=F32)
        s_ref[...] = s_ref[...] * expand_exact(jnp.exp(tot), e_ref) + states_t
        return y_off

    sf_s[...] = jnp.zeros_like(sf_s)
    sb_s[...] = jnp.zeros_like(sb_s)

    def fwd_chunk(c, carry):
        r0 = pl.multiple_of(c * CHUNK, CHUNK)
        x = conv_chunk(xs_ref, r0, 0, D_GROUP)
        bm = conv_chunk(b_ref, r0, D_GROUP, D_STATE)
        cm = conv_chunk(c_ref, r0, D_GROUP + D_STATE, D_STATE)
        xb = x.astype(BF16)
        bb = bm.astype(BF16)
        cc = cm.astype(BF16)
        xs_s[pl.ds(r0, CHUNK), :] = xb
        b_s[pl.ds(r0, CHUNK), :] = bb
        c_s[pl.ds(r0, CHUNK), :] = cc
        dtv, cs_f, cs_b = dt_terms(r0)
        cst_f = cs_f.T
        cst_b = cs_b.T
        dtt = dtv.T
        cb = lax.dot_general(cc, bb, (((1,), (1,)), ((), ())), preferred_element_type=F32)
        pieces = []
        for pair in range(hpg // 2):
            xp = xb[:, pair * LANES:(pair + 1) * LANES]
            ys = []
            for hh in range(2):
                h = 2 * pair + hh
                df = cs_f[:, h:h + 1] - cst_f[h:h + 1, :]
                db = cs_b[:, hpg + h:hpg + h + 1] - cst_b[hpg + h:hpg + h + 1, :]
                lf = jnp.exp(jnp.where(causal, df, NEG)) * dtt[h:h + 1, :]
                lb = jnp.exp(jnp.where(anti, db, NEG)) * dtt[hpg + h:hpg + h + 1, :]
                m = (cb * (lf + lb)).astype(BF16)
                ys.append(jnp.dot(m, xp, preferred_element_type=F32))
            pieces.append(jnp.where(head_a, ys[0], ys[1]))
        y = jnp.concatenate(pieces, axis=1) + x * dsk_ref[0]
        y = y + state_terms(cc, bm.T.astype(BF16), x, dtv, cs_f, cs_f[CHUNK - 1:CHUNK, :], ef_ref, sf_s)
        y_s[pl.ds(r0, CHUNK), :] = y
        return carry
    lax.fori_loop(0, n_chunks, fwd_chunk, 0)

    def bwd_chunk(i, carry):
        c = n_chunks - 1 - i
        r0 = pl.multiple_of(c * CHUNK, CHUNK)
        x = xs_s[pl.ds(r0, CHUNK), :].astype(F32)
        bt = b_s[pl.ds(r0, CHUNK), :].astype(F32).T.astype(BF16)
        cc = c_s[pl.ds(r0, CHUNK), :]
        dtv, _, cs_b = dt_terms(r0)
        y = y_s[pl.ds(r0, CHUNK), :] + state_terms(cc, bt, x, dtv, cs_b, cs_b[0:1, :], eb_ref, sb_s)
        y = y * _silu(z_ref[0, pl.ds(r0, CHUNK), :].astype(F32))
        ms = jnp.mean(y * y, axis=-1, keepdims=True)
        o_ref[0, pl.ds(r0, CHUNK), :] = (y * lax.rsqrt(ms + EPS) * ng_ref[0]).astype(BF16)
        return carry
    lax.fori_loop(0, n_chunks, bwd_chunk, 0)


def _ssd(proj3, dt3, cw, cb, dtb, arow, dsk, ng, ef, eb, tri):
    b, l, _ = proj3.shape
    g = N_GROUPS_SSM
    wconv = D_GROUP + 2 * D_STATE
    per_group = lambda shape: pl.BlockSpec((1,) + shape, lambda bi, gi: (gi,) + (0,) * len(shape))
    return pl.pallas_call(
        _ssd_body,
        grid=(b, g),
        in_specs=[pl.BlockSpec((1, l, D_GROUP), lambda bi, gi: (bi, 0, COL_Z + gi)),
                  pl.BlockSpec((1, l, D_GROUP), lambda bi, gi: (bi, 0, COL_XS + gi)),
                  pl.BlockSpec((1, l, D_STATE), lambda bi, gi: (bi, 0, COL_B + gi)),
                  pl.BlockSpec((1, l, D_STATE), lambda bi, gi: (bi, 0, COL_C + gi)),
                  pl.BlockSpec((1, l, LANES), lambda bi, gi: (bi, 0, gi)),
                  per_group((D_CONV, wconv)), per_group((1, wconv)),
                  per_group((1, LANES)), per_group((1, LANES)),
                  per_group((1, D_GROUP)), per_group((1, D_GROUP)),
                  pl.BlockSpec((1, LANES, D_GROUP), lambda bi, gi: (0, 0, 0)),
                  pl.BlockSpec((1, LANES, D_GROUP), lambda bi, gi: (0, 0, 0)),
                  pl.BlockSpec((2 * CHUNK, CHUNK), lambda bi, gi: (0, 0))],
        out_specs=pl.BlockSpec((1, l, D_GROUP), lambda bi, gi: (bi, 0, gi)),
        out_shape=jax.ShapeDtypeStruct((b, l, D_SSM), BF16),
        scratch_shapes=[pltpu.VMEM((l, D_GROUP), BF16), pltpu.VMEM((l, D_STATE), BF16),
                        pltpu.VMEM((l, D_STATE), BF16), pltpu.VMEM((l, D_GROUP), F32),
                        pltpu.VMEM((D_STATE, D_GROUP), F32), pltpu.VMEM((D_STATE, D_GROUP), F32)],
        compiler_params=_cparams(2),
        name="ssd_mixer",
    )(proj3, proj3, proj3, proj3, dt3, cw, cb, dtb, arow, dsk, ng, ef, eb, tri)


def _route_rows(lg):
    lane = lax.broadcasted_iota(I32, lg.shape, 1).astype(F32)
    n_g = float(N_EXPERT_GROUPS)
    n_e = float(EXPERTS_PER_GROUP)
    gl = jnp.where(lane < n_g, lg, NEG)
    gmax = jnp.max(gl, axis=-1, keepdims=True)
    g_idx = jnp.min(jnp.where(gl == gmax, lane, float(LANES)), axis=-1, keepdims=True)
    g_p = 1.0 / jnp.sum(jnp.exp(gl - gmax), axis=-1, keepdims=True)
    first = n_g + n_e * g_idx
    el = jnp.where((lane >= first) & (lane < first + n_e), lg, NEG)
    e1 = jnp.max(el, axis=-1, keepdims=True)
    i1 = jnp.min(jnp.where(el == e1, lane, float(LANES)), axis=-1, keepdims=True)
    el2 = jnp.where(lane == i1, NEG, el)
    e2 = jnp.max(el2, axis=-1, keepdims=True)
    i2 = jnp.min(jnp.where(el2 == e2, lane, float(LANES)), axis=-1, keepdims=True)
    r = jnp.exp(e2 - e1)
    gate1 = g_p / (1.0 + r)
    gate2 = gate1 * r
    return jnp.where(lane == 0.0, i1 - n_g,
                     jnp.where(lane == 1.0, i2 - n_g,
                               jnp.where(lane == 2.0, gate1, jnp.where(lane == 3.0, gate2, 0.0))))


def _outproj_body(x_ref, a_ref, s_ref, ag_ref, wa_ref, ws_ref, g2_ref, wr_ref, br_ref, x2_ref, hp_ref, rt_ref):
    a = a_ref[...].astype(F32)
    ms = jnp.mean(a * a, axis=-1, keepdims=True)
    an = (a * lax.rsqrt(ms + EPS) * ag_ref[...]).astype(BF16)
    y = (jnp.dot(an, wa_ref[...], preferred_element_type=F32)
         + jnp.dot(s_ref[...], ws_ref[...], preferred_element_type=F32))
    x2 = x_ref[...] + y
    x2_ref[...] = x2
    ms2 = jnp.mean(x2 * x2, axis=-1, keepdims=True)
    hn = x2 * lax.rsqrt(ms2 + EPS) * g2_ref[...]
    hi = hn.astype(BF16)
    lo = (hn - hi.astype(F32)).astype(BF16)
    l1 = jnp.dot(hi, wr_ref[...], preferred_element_type=F32)
    l2 = jnp.dot(lo, wr_ref[:, :LANES], preferred_element_type=F32)
    rt_ref[...] = _route_rows(l1[:, :LANES] + l1[:, LANES:] + l2 + br_ref[...])
    hp_ref[...] = _pack_bf16_pairs(hi.astype(F32))


def _outproj(x2d, attn2d, ssm2d, ag, wa, ws, g2, wr, br):
    t, d = x2d.shape
    tm = min(256, t)
    row = lambda w: pl.BlockSpec((tm, w), lambda i: (i, 0))
    full = lambda a: pl.BlockSpec(a.shape, lambda i: (0,) * a.ndim)
    return pl.pallas_call(
        _outproj_body,
        grid=(t // tm,),
        in_specs=[row(d), row(D_ATTN), row(D_SSM), full(ag), full(wa), full(ws), full(g2), full(wr), full(br)],
        out_specs=[row(d), row(d // 2), row(LANES)],
        out_shape=[jax.ShapeDtypeStruct((t, d), F32), jax.ShapeDtypeStruct((t, d // 2), U32),
                   jax.ShapeDtypeStruct((t, LANES), F32)],
        compiler_params=_cparams(1),
        name="outproj_router",
    )(x2d, attn2d, ssm2d, ag, wa, ws, g2, wr, br)


def _moe_body(vblk_ref, vexp_ref, vlo_ref, vhi_ref, tok_ref, tok_next_ref, slot_ref, x_hbm, wgu_ref, wd_ref, y_hbm,
              xbuf, ybuf, gsem, ssem):
    blk = xbuf.shape[1]
    half = xbuf.shape[2]
    v = pl.program_id(0)
    lo = vlo_ref[v]
    hi = vhi_ref[v]
    s = vblk_ref[v]
    n_blocks = y_hbm.shape[0] // blk
    slot = s % 2
    other = 1 - slot

    def start_gather(table_ref, buf_slot):
        def body(i, c):
            pltpu.make_async_copy(x_hbm.at[pl.ds(table_ref[0, 0, i], 1)], xbuf.at[buf_slot, pl.ds(i, 1)],
                                  gsem.at[buf_slot]).start()
            return c
        lax.fori_loop(0, blk, body, 0, unroll=MOE_DMA_UNROLL)

    def wait_gather(buf_slot):
        pltpu.make_async_copy(x_hbm.at[pl.ds(0, blk)], xbuf.at[buf_slot], gsem.at[buf_slot]).wait()

    def start_scatter(buf_slot):
        def body(i, c):
            pltpu.make_async_copy(ybuf.at[buf_slot, pl.ds(i, 1)], y_hbm.at[pl.ds(slot_ref[0, 0, i], 1)],
                                  ssem.at[buf_slot]).start()
            return c
        lax.fori_loop(0, blk, body, 0, unroll=MOE_DMA_UNROLL)

    def wait_scatter(buf_slot):
        pltpu.make_async_copy(ybuf.at[buf_slot], y_hbm.at[pl.ds(0, blk)], ssem.at[buf_slot]).wait()

    @pl.when(hi > lo)
    def _():
        @pl.when(lo == 0)
        def _():
            @pl.when(s == 0)
            def _():
                start_gather(tok_ref, slot)
            wait_gather(slot)

            @pl.when(s + 1 < n_blocks)
            def _():
                start_gather(tok_next_ref, other)

        xg = xbuf[slot]
        xlo = _unpack_lo(xg).astype(BF16)
        xhi = _unpack_hi(xg).astype(BF16)
        gu = (jnp.dot(xlo, wgu_ref[0, :half, :], preferred_element_type=F32)
              + jnp.dot(xhi, wgu_ref[0, half:, :], preferred_element_type=F32))
        gate = gu[:, :D_EXPERT]
        hid = (_silu(gate) * gu[:, D_EXPERT:]).astype(BF16)
        y = _pack_bf16_pairs(jnp.dot(hid, wd_ref[0], preferred_element_type=F32))

        @pl.when(lo == 0)
        def _():
            ybuf[slot] = y

        @pl.when(lo > 0)
        def _():
            row = lax.broadcasted_iota(I32, (blk, 1), 0)
            ybuf[slot] = jnp.where((row >= lo) & (row < hi), y, ybuf[slot])

        @pl.when(hi == blk)
        def _():
            @pl.when(s > 0)
            def _():
                wait_scatter(other)
            start_scatter(slot)

            @pl.when(s == n_blocks - 1)
            def _():
                wait_scatter(slot)


def _moe(visits, row_tok, row_slot, hn_packed, wgu, wd):
    blk = MOE_BLOCK
    n_assign = row_tok.shape[0]
    n_blocks = n_assign // blk
    n_visits = visits[0].shape[0]
    half = hn_packed.shape[1]
    d = 2 * half
    smem_rows = pl.BlockSpec((1, 1, blk), lambda v, vb, ve, vl, vh: (vb[v], 0, 0), memory_space=pltpu.SMEM)
    smem_next = pl.BlockSpec((1, 1, blk), lambda v, vb, ve, vl, vh: (jnp.minimum(vb[v] + 1, n_blocks - 1), 0, 0),
                             memory_space=pltpu.SMEM)
    grid_spec = pltpu.PrefetchScalarGridSpec(
        num_scalar_prefetch=4,
        grid=(n_visits,),
        in_specs=[smem_rows, smem_next, smem_rows,
                  pl.BlockSpec(memory_space=pl.ANY),
                  pl.BlockSpec((1, d, 2 * D_EXPERT), lambda v, vb, ve, vl, vh: (ve[v], 0, 0)),
                  pl.BlockSpec((1, D_EXPERT, d), lambda v, vb, ve, vl, vh: (ve[v], 0, 0))],
        out_specs=pl.BlockSpec(memory_space=pl.ANY),
        scratch_shapes=[pltpu.VMEM((2, blk, half), U32), pltpu.VMEM((2, blk, half), U32),
                        pltpu.SemaphoreType.DMA((2,)), pltpu.SemaphoreType.DMA((2,))],
    )
    tok3 = row_tok.reshape(n_blocks, 1, blk)
    return pl.pallas_call(
        _moe_body,
        grid_spec=grid_spec,
        out_shape=jax.ShapeDtypeStruct((n_assign, half), U32),
        compiler_params=_cparams(1),
        name="moe_experts",
    )(*visits, tok3, tok3, row_slot.reshape(n_blocks, 1, blk), hn_packed, wgu, wd)


def _route_tables(route, t):
    blk = MOE_BLOCK
    n_assign = t * TOP_K
    n_blocks = n_assign // blk
    flat_e = route[:, :TOP_K].astype(I32).T.reshape(n_assign)
    order = jnp.argsort(flat_e, stable=True).astype(I32)
    counts = jnp.sum((flat_e[:, None] == jnp.arange(N_EXPERTS, dtype=I32)[None, :]).astype(I32), axis=0)
    ends = jnp.cumsum(counts)
    starts = ends - counts
    cuts = jnp.sort(jnp.concatenate([jnp.arange(n_blocks, dtype=I32) * blk, starts[1:]]))
    nxt = jnp.concatenate([cuts[1:], jnp.full((1,), n_assign, I32)])
    vblk = jnp.minimum(cuts // blk, n_blocks - 1)
    vexp = jnp.minimum(jnp.sum((ends[None, :] <= cuts[:, None]).astype(I32), axis=1), N_EXPERTS - 1)
    vlo = cuts - vblk * blk
    vhi = jnp.maximum(jnp.minimum(nxt, (vblk + 1) * blk) - vblk * blk, vlo)
    return (vblk, vexp, vlo, vhi), order % t, order


def _combine_body(x2_ref, y0_ref, y1_ref, rt_ref, o_ref):
    half = o_ref.shape[1] // 2
    g0 = rt_ref[:, TOP_K:TOP_K + 1]
    g1 = rt_ref[:, TOP_K + 1:TOP_K + 2]
    y0 = y0_ref[...]
    y1 = y1_ref[...]
    o_ref[:, :half] = x2_ref[:, :half] + g0 * _unpack_lo(y0) + g1 * _unpack_lo(y1)
    o_ref[:, half:] = x2_ref[:, half:] + g0 * _unpack_hi(y0) + g1 * _unpack_hi(y1)


def _combine(x2, y_slots, route):
    t, d = x2.shape
    tm = min(512, t)
    nb = t // tm
    return pl.pallas_call(
        _combine_body,
        grid=(nb,),
        in_specs=[pl.BlockSpec((tm, d), lambda i: (i, 0)),
                  pl.BlockSpec((tm, d // 2), lambda i: (i, 0)),
                  pl.BlockSpec((tm, d // 2), lambda i: (i + nb, 0)),
                  pl.BlockSpec((tm, LANES), lambda i: (i, 0))],
        out_specs=pl.BlockSpec((tm, d), lambda i: (i, 0)),
        out_shape=jax.ShapeDtypeStruct((t, d), F32),
        compiler_params=_cparams(1),
        name="moe_combine",
    )(x2, y_slots, y_slots, route)


def _prepare(norm1_g, w_in, q_norm_g, k_norm_g, rpb, attn_out_g, conv_w, conv_b, a_log_f, a_log_b,
             dt_bias_f, dt_bias_b, d_skip, ssm_norm_g, w_out, norm2_g, w_router_group, b_router_group,
             w_router_expert, b_router_expert, w_gate, w_up, w_down):
    d = w_in.shape[0]
    hpg = HEADS_PER_GROUP
    p = {}
    p["norm1_g"] = norm1_g.reshape(1, d).astype(F32)
    p["w_main"] = w_in[:, :D_PROJ_MAIN].astype(BF16)
    w_dt = w_in[:, D_PROJ_MAIN:]
    zeros = jnp.zeros((d, LANES - 2 * hpg), w_in.dtype)
    per_group = lambda v, g: v[..., g * hpg:(g + 1) * hpg]
    p["w_dt"] = jnp.concatenate(
        [jnp.concatenate([per_group(w_dt[:, :N_HEADS_SSM], g), per_group(w_dt[:, N_HEADS_SSM:], g), zeros], axis=1)
         for g in range(N_GROUPS_SSM)], axis=1).astype(BF16)
    lane_rows = lambda f, bwd: jnp.stack(
        [jnp.concatenate([per_group(f, g), per_group(bwd, g), jnp.zeros((LANES - 2 * hpg,), F32)])
         for g in range(N_GROUPS_SSM)])[:, None, :]
    p["dtb"] = lane_rows(dt_bias_f.astype(F32), dt_bias_b.astype(F32))
    p["arow"] = lane_rows(-jnp.exp(a_log_f.astype(F32)), -jnp.exp(a_log_b.astype(F32)))
    scale = HEAD_DIM ** -0.5
    p["qg2"] = (jnp.tile(q_norm_g.astype(F32), 2) * scale).reshape(1, LANES)
    p["kg2"] = jnp.tile(k_norm_g.astype(F32), 2).reshape(1, LANES)
    lane = jnp.arange(LANES)
    p["e_mat"] = (lane[:, None] // HEAD_DIM == lane[None, :] // HEAD_DIM).astype(BF16)
    p["bias"] = _attention_bias(rpb)
    p["attn_out_g"] = attn_out_g.reshape(1, D_ATTN).astype(F32)
    cw = conv_w.reshape(D_CONV, -1).astype(F32)
    cbias = conv_b.reshape(1, -1).astype(F32)
    group_cols = lambda a, g: jnp.concatenate(
        [a[:, g * D_GROUP:(g + 1) * D_GROUP],
         a[:, D_SSM + g * D_STATE:D_SSM + (g + 1) * D_STATE],
         a[:, D_SSM + D_BC + g * D_STATE:D_SSM + D_BC + (g + 1) * D_STATE]], axis=1)
    p["cw"] = jnp.stack([group_cols(cw, g) for g in range(N_GROUPS_SSM)])
    p["cb"] = jnp.stack([group_cols(cbias, g) for g in range(N_GROUPS_SSM)])
    p["dsk"] = jnp.repeat(d_skip.astype(F32), SSM_HEAD_DIM).reshape(N_GROUPS_SSM, 1, D_GROUP)
    p["ng"] = ssm_norm_g.astype(F32).reshape(N_GROUPS_SSM, 1, D_GROUP)
    col_head = jnp.arange(D_GROUP) // SSM_HEAD_DIM
    p["ef"] = (lane[:, None] == col_head[None, :]).astype(BF16)[None]
    p["eb"] = (lane[:, None] == col_head[None, :] + hpg).astype(BF16)[None]
    pos = jnp.arange(CHUNK)
    p["tri"] = jnp.concatenate([pos[:, None] >= pos[None, :], pos[:, None] <= pos[None, :]], axis=0).astype(BF16)
    p["wa"] = w_out[:D_ATTN].astype(BF16)
    p["ws"] = w_out[D_ATTN:].astype(BF16)
    p["norm2_g"] = norm2_g.reshape(1, d).astype(F32)
    n_r = N_EXPERT_GROUPS + N_EXPERTS
    wr = jnp.concatenate([w_router_group, w_router_expert, jnp.zeros((d, LANES - n_r), F32)], axis=1).astype(F32)
    wr_hi = wr.astype(BF16)
    wr_lo = (wr - wr_hi.astype(F32)).astype(BF16)
    p["wr"] = jnp.concatenate([wr_hi, wr_lo], axis=1)
    p["br"] = jnp.concatenate([b_router_group, b_router_expert, jnp.zeros((LANES - n_r,), F32)]).reshape(1, LANES)
    p["wgu"] = jnp.concatenate([w_gate.astype(BF16), w_up.astype(BF16)], axis=2)
    p["wd"] = w_down.astype(BF16)
    return p


def _layer(x, p):
    b, l, d = x.shape
    t = b * l
    x2d = x.reshape(t, d)
    proj, dt = _inproj(x2d, p["norm1_g"], p["w_main"], p["w_dt"])
    proj3 = proj.reshape(b, l, D_PROJ_MAIN)
    attn = _attention(proj3, p["qg2"], p["kg2"], p["e_mat"], p["bias"])
    ssm = _ssd(proj3, dt.reshape(b, l, N_GROUPS_SSM * LANES), p["cw"], p["cb"], p["dtb"], p["arow"],
               p["dsk"], p["ng"], p["ef"], p["eb"], p["tri"])
    x2, hn_packed, route = _outproj(x2d, attn.reshape(t, D_ATTN), ssm.reshape(t, D_SSM), p["attn_out_g"],
                                    p["wa"], p["ws"], p["norm2_g"], p["wr"], p["br"])
    visits, row_tok, row_slot = _route_tables(route, t)
    y_slots = _moe(visits, row_tok, row_slot, hn_packed, p["wgu"], p["wd"])
    return _combine(x2, y_slots, route).reshape(b, l, d)


def kernel(x_prompt, x_sample, norm1_g, w_in, q_norm_g, k_norm_g, rpb, attn_out_g, conv_w, conv_b, a_log_f,
           a_log_b, dt_bias_f, dt_bias_b, d_skip, ssm_norm_g, w_out, norm2_g, w_router_group, b_router_group,
           w_router_expert, b_router_expert, w_gate, w_up, w_down):
    weights = (norm1_g, w_in, q_norm_g, k_norm_g, rpb, attn_out_g, conv_w, conv_b, a_log_f, a_log_b, dt_bias_f,
               dt_bias_b, d_skip, ssm_norm_g, w_out, norm2_g, w_router_group, b_router_group, w_router_expert,
               b_router_expert, w_gate, w_up, w_down)
    assert all(w.shape[0] == 1 for w in weights), "one layer of stacked weights expected"
    p = _prepare(*(w[0] for w in weights))
    return (_layer(x_prompt, p), _layer(x_sample, p))
```

```python
import jax
import jax.numpy as jnp
from jax import lax
from jax.experimental import pallas as pl
from jax.experimental.pallas import tpu as pltpu

F32 = jnp.float32
BF16 = jnp.bfloat16
U32 = jnp.uint32
I32 = jnp.int32

EPS = 1e-6
GRID_W = 64
N_HEADS_ATTN = 16
HEAD_DIM = 64
D_ATTN = N_HEADS_ATTN * HEAD_DIM
WIN_H = 8
WIN_W = 16
N_HEADS_SSM = 16
SSM_HEAD_DIM = 64
D_SSM = N_HEADS_SSM * SSM_HEAD_DIM
N_GROUPS_SSM = 2
HEADS_PER_GROUP = N_HEADS_SSM // N_GROUPS_SSM
D_GROUP = D_SSM // N_GROUPS_SSM
D_STATE = 128
D_CONV = 5
CHUNK = 128
D_BC = N_GROUPS_SSM * D_STATE
N_EXPERT_GROUPS = 4
EXPERTS_PER_GROUP = 8
N_EXPERTS = N_EXPERT_GROUPS * EXPERTS_PER_GROUP
TOP_K = 2
D_EXPERT = 512

LANES = 128
BF16_ROWS = 16
CONV_HALO = BF16_ROWS
NEG = -1e30
VMEM_LIMIT_BYTES = 56 * 1024 * 1024
MOE_BLOCK = 256
ATTN_ROW_UNROLL = 8
MOE_DMA_UNROLL = 8
ROW_TILE = 8

COL_Q, COL_K, COL_V = 0, D_ATTN // LANES, 2 * D_ATTN // LANES
COL_Z = 3 * D_ATTN // D_GROUP
COL_XS = (3 * D_ATTN + D_SSM) // D_GROUP
COL_B = (3 * D_ATTN + 2 * D_SSM) // LANES
COL_C = COL_B + D_BC // LANES
D_PROJ_MAIN = 3 * D_ATTN + 2 * D_SSM + 2 * D_BC


def _cparams(n_axes):
    return pltpu.CompilerParams(dimension_semantics=("arbitrary",) * n_axes,
                                vmem_limit_bytes=VMEM_LIMIT_BYTES)


def _silu(x):
    return x * (0.5 * jnp.tanh(0.5 * x) + 0.5)


def _split3(x):
    hi = x.astype(BF16)
    r1 = x - hi.astype(F32)
    mid = r1.astype(BF16)
    lo = (r1 - mid.astype(F32)).astype(BF16)
    return hi, mid, lo


def _pack_bf16_pairs(x):
    n = x.shape[1] // 2
    u = lax.bitcast_convert_type(x.astype(BF16).astype(F32), U32)
    return (u[:, :n] >> 16) | u[:, n:]


def _unpack_lo(u):
    return lax.bitcast_convert_type(u << 16, F32)


def _unpack_hi(u):
    return lax.bitcast_convert_type(u & jnp.uint32(0xFFFF0000), F32)


def _store_tile_rows(ref, lead, packed):
    m = packed.shape[0]
    for c in range(ROW_TILE):
        ref[lead + (pl.ds(c, m, stride=ROW_TILE), slice(None))] = packed[:, c * LANES:(c + 1) * LANES]


def _load_tile_rows(ref, lead, m):
    return [ref[lead + (pl.ds(c, m, stride=ROW_TILE), slice(None))] for c in range(ROW_TILE)]


def _inproj_body(x_ref, g_ref, w_ref, wdt_ref, o_ref, dt_ref, hn_ref):
    tm = x_ref.shape[0]
    rows = min(tm, 256)

    @pl.when(pl.program_id(1) == 0)
    def _():
        def norm_rows(i, c):
            r0 = pl.multiple_of(i * rows, rows)
            x = x_ref[pl.ds(r0, rows), :]
            ms = jnp.mean(x * x, axis=-1, keepdims=True)
            hn_ref[pl.ds(r0, rows), :] = (x * lax.rsqrt(ms + EPS) * g_ref[...]).astype(BF16)
            return c
        lax.fori_loop(0, tm // rows, norm_rows, 0)
        dt_ref[...] = jnp.dot(hn_ref[...], wdt_ref[...], preferred_element_type=F32)

    o_ref[...] = jnp.dot(hn_ref[...], w_ref[...], preferred_element_type=F32).astype(BF16)


def _inproj(x2d, gain, w_main, w_dt):
    t, d = x2d.shape
    n = w_main.shape[1]
    ndt = w_dt.shape[1]
    tm = min(1024, t)
    tn = 512
    return pl.pallas_call(
        _inproj_body,
        grid=(t // tm, n // tn),
        in_specs=[pl.BlockSpec((tm, d), lambda i, j: (i, 0)),
                  pl.BlockSpec((1, d), lambda i, j: (0, 0)),
                  pl.BlockSpec((d, tn), lambda i, j: (0, j)),
                  pl.BlockSpec((d, ndt), lambda i, j: (0, 0))],
        out_specs=[pl.BlockSpec((tm, tn), lambda i, j: (i, j)),
                   pl.BlockSpec((tm, ndt), lambda i, j: (i, 0))],
        out_shape=[jax.ShapeDtypeStruct((t, n), BF16), jax.ShapeDtypeStruct((t, ndt), F32)],
        scratch_shapes=[pltpu.VMEM((tm, d), BF16)],
        compiler_params=_cparams(2),
        name="inproj",
    )(x2d, gain, w_main, w_dt)


def _attn_body(q_ref, k_ref, v_ref, qg_ref, kg_ref, e_ref, bias_ref, o_ref, qs, ks, s_scr, m_scr):
    l = q_ref.shape[1]
    n_rows = l // GRID_W
    win_keys = WIN_H * GRID_W
    ch = min(l, 512)
    head_a = lax.broadcasted_iota(I32, (1, LANES), 1) < HEAD_DIM
    sel_a = jnp.where(head_a, 1.0, 0.0).astype(BF16)
    sel_b = jnp.where(head_a, 0.0, 1.0).astype(BF16)

    def norm_rows(i, c):
        r0 = pl.multiple_of(i * ch, ch)
        for src, gref, dst in ((q_ref, qg_ref, qs), (k_ref, kg_ref, ks)):
            x = src[0, pl.ds(r0, ch), :].astype(F32)
            ssq = jnp.dot((x * x).astype(BF16), e_ref[...], preferred_element_type=F32)
            dst[pl.ds(r0, ch), :] = (x * lax.rsqrt(ssq * (1.0 / HEAD_DIM) + EPS) * gref[...]).astype(BF16)
        return c
    lax.fori_loop(0, l // ch, norm_rows, 0)

    def key_start(r):
        return jnp.clip(r - WIN_H // 2, 0, n_rows - WIN_H)

    def score_stage(r, slot):
        rs = key_start(r)
        q_r = qs[pl.ds(pl.multiple_of(r * GRID_W, GRID_W), GRID_W), :]
        qm = jnp.concatenate([q_r * sel_a, q_r * sel_b], axis=0)
        kb = ks[pl.ds(pl.multiple_of(rs * GRID_W, GRID_W), win_keys), :]
        s = lax.dot_general(qm, kb, (((1,), (1,)), ((), ())), preferred_element_type=F32)
        s = s + bias_ref[0, rs - r + (WIN_H - 1)]
        s_scr[slot] = s
        m_scr[slot] = jnp.max(s, axis=-1, keepdims=True)

    def value_stage(r, slot):
        rs = key_start(r)
        vb = v_ref[0, pl.ds(pl.multiple_of(rs * GRID_W, GRID_W), win_keys), :]
        p = jnp.exp(s_scr[slot] - m_scr[slot])
        den = jnp.sum(p, axis=-1, keepdims=True)
        o = jnp.dot(p.astype(BF16), vb, preferred_element_type=F32) * (1.0 / den)
        out = jnp.where(head_a, o[:GRID_W], o[GRID_W:])
        o_ref[0, pl.ds(pl.multiple_of(r * GRID_W, GRID_W), GRID_W), :] = out.astype(BF16)

    score_stage(0, 0)

    def row_group(i, c):
        r = ATTN_ROW_UNROLL * i
        for j in range(ATTN_ROW_UNROLL):
            score_stage(jnp.minimum(r + j + 1, n_rows - 1), (j + 1) % 2)
            value_stage(r + j, j % 2)
        return c
    lax.fori_loop(0, n_rows // ATTN_ROW_UNROLL, row_group, 0)


def _attention(proj3, qg2, kg2, e_mat, bias):
    b, l, _ = proj3.shape
    n_pairs = N_HEADS_ATTN // 2
    blk = (1, l, LANES)
    return pl.pallas_call(
        _attn_body,
        grid=(n_pairs, b),
        in_specs=[pl.BlockSpec(blk, lambda hp, bi: (bi, 0, COL_Q + hp)),
                  pl.BlockSpec(blk, lambda hp, bi: (bi, 0, COL_K + hp)),
                  pl.BlockSpec(blk, lambda hp, bi: (bi, 0, COL_V + hp)),
                  pl.BlockSpec((1, LANES), lambda hp, bi: (0, 0)),
                  pl.BlockSpec((1, LANES), lambda hp, bi: (0, 0)),
                  pl.BlockSpec((LANES, LANES), lambda hp, bi: (0, 0)),
                  pl.BlockSpec((1, WIN_H, LANES, WIN_H * GRID_W), lambda hp, bi: (hp, 0, 0, 0))],
        out_specs=pl.BlockSpec(blk, lambda hp, bi: (bi, 0, hp)),
        out_shape=jax.ShapeDtypeStruct((b, l, D_ATTN), BF16),
        scratch_shapes=[pltpu.VMEM((l, LANES), BF16), pltpu.VMEM((l, LANES), BF16),
                        pltpu.VMEM((2, LANES, WIN_H * GRID_W), F32), pltpu.VMEM((2, LANES, 1), F32)],
        compiler_params=_cparams(2),
        name="nbr_attention",
    )(proj3, proj3, proj3, qg2, kg2, e_mat, bias)


def _attention_bias(rpb):
    cols = jnp.arange(GRID_W, dtype=I32)
    col_start = jnp.clip(cols - WIN_W // 2, 0, GRID_W - WIN_W)
    keys = cols[None, :]
    valid = (keys >= col_start[:, None]) & (keys < col_start[:, None] + WIN_W)
    dc = jnp.clip(keys - cols[:, None] + (WIN_W - 1), 0, 2 * WIN_W - 2)
    tab = jnp.where(valid[None, None], rpb.astype(F32)[:, :, dc], NEG)
    per_variant = []
    for o in range(WIN_H):
        t = tab[:, o:o + WIN_H]
        per_variant.append(jnp.transpose(t, (0, 2, 1, 3)).reshape(N_HEADS_ATTN, GRID_W, WIN_H * GRID_W))
    full = jnp.stack(per_variant, axis=1)
    full = full.reshape(N_HEADS_ATTN // 2, 2, WIN_H, GRID_W, WIN_H * GRID_W)
    return jnp.transpose(full, (0, 2, 1, 3, 4)).reshape(N_HEADS_ATTN // 2, WIN_H, 2 * GRID_W, WIN_H * GRID_W)


def _ssd_body(z_ref, xs_ref, b_ref, c_ref, dt_ref, cw_ref, cb_ref, dtb_ref, arow_ref, dsk_ref, ng_ref,
              ef_ref, eb_ref, tri_ref, shift_ref, o_ref, xs_s, b_s, c_s, y_s, sf_s, sb_s, x_f, pre_s):
    l = xs_ref.shape[1]
    n_chunks = l // CHUNK
    hpg = HEADS_PER_GROUP
    ii = lax.broadcasted_iota(I32, (CHUNK, CHUNK), 0)
    jj = lax.broadcasted_iota(I32, (CHUNK, CHUNK), 1)
    causal = ii >= jj
    anti = ii <= jj
    head_a = lax.broadcasted_iota(I32, (1, LANES), 1) < SSM_HEAD_DIM

    def conv_chunk(r0):
        p0 = pl.multiple_of(jnp.maximum(r0 - CONV_HALO, 0), CONV_HALO)
        n0 = pl.multiple_of(jnp.minimum(r0 + CHUNK, l - CONV_HALO), CONV_HALO)
        has_prev = r0 > 0
        has_next = r0 + CHUNK < l
        parts = []
        for ref in (xs_ref, b_ref, c_ref):
            prev = ref[0, pl.ds(p0, CONV_HALO), :]
            nxt = ref[0, pl.ds(n0, CONV_HALO), :]
            parts.append(jnp.concatenate([jnp.where(has_prev, prev, jnp.zeros_like(prev)),
                                          ref[0, pl.ds(r0, CHUNK), :],
                                          jnp.where(has_next, nxt, jnp.zeros_like(nxt))], axis=0))
        ext = jnp.concatenate(parts, axis=1)
        shifted = jnp.dot(shift_ref[...], ext, preferred_element_type=F32)
        mid = D_CONV // 2
        acc = cb_ref[0] + ext[CONV_HALO:CONV_HALO + CHUNK].astype(F32) * cw_ref[0, mid:mid + 1, :]
        for n, k in enumerate(k for k in range(D_CONV) if k != mid):
            acc = acc + shifted[n * CHUNK:(n + 1) * CHUNK] * cw_ref[0, k:k + 1, :]
        return _silu(acc)

    def dt_terms(r0):
        raw = dt_ref[0, pl.ds(r0, CHUNK), :] + dtb_ref[0]
        dtv = jnp.maximum(raw, 0.0) + jnp.log(1.0 + jnp.exp(-jnp.abs(raw)))
        adt = dtv * arow_ref[0]
        hi, mid, lo = _split3(adt)
        cs3 = jnp.dot(tri_ref[...], jnp.concatenate([hi, mid, lo], axis=1), preferred_element_type=F32)
        cs = cs3[:, :LANES] + cs3[:, LANES:2 * LANES] + cs3[:, 2 * LANES:]
        return dtv, cs[:CHUNK], cs[CHUNK:]

    def expand_exact(row, e_ref):
        hi, mid, lo = _split3(jnp.broadcast_to(row, (8, LANES)))
        e = e_ref[0]
        r = (jnp.dot(hi, e, preferred_element_type=F32) + jnp.dot(mid, e, preferred_element_type=F32)
             + jnp.dot(lo, e, preferred_element_type=F32))
        return r[0:1]

    def state_terms(cc, bt, x, dtv, cs, tot, e_ref, s_ref):
        e = e_ref[0]
        expcs = jnp.dot(jnp.exp(cs).astype(BF16), e, preferred_element_type=F32)
        y_off = jnp.dot(cc, s_ref[...].astype(BF16), preferred_element_type=F32) * expcs
        scl = jnp.dot((dtv * jnp.exp(tot - cs)).astype(BF16), e, preferred_element_type=F32)
        states_t = jnp.dot(bt, (x * scl).astype(BF16), preferred_element_type=F32)
        s_ref[...] = s_ref[...] * expand_exact(jnp.exp(tot), e_ref) + states_t
        return y_off

    sf_s[...] = jnp.zeros_like(sf_s)
    sb_s[...] = jnp.zeros_like(sb_s)

    def prep_stage(c, slot):
        r0 = pl.multiple_of(c * CHUNK, CHUNK)
        xbc = conv_chunk(r0)
        x = xbc[:, :D_GROUP]
        bm = xbc[:, D_GROUP:D_GROUP + D_STATE]
        xs_s[pl.ds(r0, CHUNK), :] = x.astype(BF16)
        b_s[pl.ds(r0, CHUNK), :] = bm.astype(BF16)
        c_s[pl.ds(r0, CHUNK), :] = xbc[:, D_GROUP + D_STATE:].astype(BF16)
        x_f[slot] = x
        dtv, cs_f, cs_b = dt_terms(r0)
        for n, a in enumerate((dtv, cs_f, cs_b, cs_f.T, cs_b.T, dtv.T, bm.T)):
            pre_s[slot, n] = a

    def mix_stage(c, slot):
        r0 = pl.multiple_of(c * CHUNK, CHUNK)
        x = x_f[slot]
        xb = xs_s[pl.ds(r0, CHUNK), :]
        bb = b_s[pl.ds(r0, CHUNK), :]
        cc = c_s[pl.ds(r0, CHUNK), :]
        dtv, cs_f, cs_b, cst_f, cst_b, dtt, bmt = (pre_s[slot, n] for n in range(7))
        cb = lax.dot_general(cc, bb, (((1,), (1,)), ((), ())), preferred_element_type=F32)
        pieces = []
        for pair in range(hpg // 2):
            xp = xb[:, pair * LANES:(pair + 1) * LANES]
            ys = []
            for hh in range(2):
                h = 2 * pair + hh
                df = cs_f[:, h:h + 1] - cst_f[h:h + 1, :]
                db = cs_b[:, hpg + h:hpg + h + 1] - cst_b[hpg + h:hpg + h + 1, :]
                lf = jnp.exp(jnp.where(causal, df, NEG)) * dtt[h:h + 1, :]
                lb = jnp.exp(jnp.where(anti, db, NEG)) * dtt[hpg + h:hpg + h + 1, :]
                m = (cb * (lf + lb)).astype(BF16)
                ys.append(jnp.dot(m, xp, preferred_element_type=F32))
            pieces.append(jnp.where(head_a, ys[0], ys[1]))
        y = jnp.concatenate(pieces, axis=1) + x * dsk_ref[0]
        y = y + state_terms(cc, bmt.astype(BF16), x, dtv, cs_f, cs_f[CHUNK - 1:CHUNK, :], ef_ref, sf_s)
        y_s[pl.ds(r0, CHUNK), :] = y

    prep_stage(0, 0)

    def fwd_pair(i, carry):
        c = 2 * i
        prep_stage(c + 1, 1)
        mix_stage(c, 0)
        prep_stage(jnp.minimum(c + 2, n_chunks - 1), 0)
        mix_stage(c + 1, 1)
        return carry
    lax.fori_loop(0, n_chunks // 2, fwd_pair, 0)

    def bwd_chunk(i, carry):
        c = n_chunks - 1 - i
        r0 = pl.multiple_of(c * CHUNK, CHUNK)
        x = xs_s[pl.ds(r0, CHUNK), :].astype(F32)
        bt = b_s[pl.ds(r0, CHUNK), :].astype(F32).T.astype(BF16)
        cc = c_s[pl.ds(r0, CHUNK), :]
        dtv, _, cs_b = dt_terms(r0)
        y = y_s[pl.ds(r0, CHUNK), :] + state_terms(cc, bt, x, dtv, cs_b, cs_b[0:1, :], eb_ref, sb_s)
        y = y * _silu(z_ref[0, pl.ds(r0, CHUNK), :].astype(F32))
        ms = jnp.mean(y * y, axis=-1, keepdims=True)
        o_ref[0, pl.ds(r0, CHUNK), :] = (y * lax.rsqrt(ms + EPS) * ng_ref[0]).astype(BF16)
        return carry
    lax.fori_loop(0, n_chunks, bwd_chunk, 0)


def _ssd(proj3, dt3, cw, cb, dtb, arow, dsk, ng, ef, eb, tri, shift):
    b, l, _ = proj3.shape
    g = N_GROUPS_SSM
    wconv = D_GROUP + 2 * D_STATE
    per_group = lambda shape: pl.BlockSpec((1,) + shape, lambda bi, gi: (gi,) + (0,) * len(shape))
    return pl.pallas_call(
        _ssd_body,
        grid=(b, g),
        in_specs=[pl.BlockSpec((1, l, D_GROUP), lambda bi, gi: (bi, 0, COL_Z + gi)),
                  pl.BlockSpec((1, l, D_GROUP), lambda bi, gi: (bi, 0, COL_XS + gi)),
                  pl.BlockSpec((1, l, D_STATE), lambda bi, gi: (bi, 0, COL_B + gi)),
                  pl.BlockSpec((1, l, D_STATE), lambda bi, gi: (bi, 0, COL_C + gi)),
                  pl.BlockSpec((1, l, LANES), lambda bi, gi: (bi, 0, gi)),
                  per_group((D_CONV, wconv)), per_group((1, wconv)),
                  per_group((1, LANES)), per_group((1, LANES)),
                  per_group((1, D_GROUP)), per_group((1, D_GROUP)),
                  pl.BlockSpec((1, LANES, D_GROUP), lambda bi, gi: (0, 0, 0)),
                  pl.BlockSpec((1, LANES, D_GROUP), lambda bi, gi: (0, 0, 0)),
                  pl.BlockSpec((2 * CHUNK, CHUNK), lambda bi, gi: (0, 0)),
                  pl.BlockSpec(shift.shape, lambda bi, gi: (0, 0))],
        out_specs=pl.BlockSpec((1, l, D_GROUP), lambda bi, gi: (bi, 0, gi)),
        out_shape=jax.ShapeDtypeStruct((b, l, D_SSM), BF16),
        scratch_shapes=[pltpu.VMEM((l, D_GROUP), BF16), pltpu.VMEM((l, D_STATE), BF16),
                        pltpu.VMEM((l, D_STATE), BF16), pltpu.VMEM((l, D_GROUP), F32),
                        pltpu.VMEM((D_STATE, D_GROUP), F32), pltpu.VMEM((D_STATE, D_GROUP), F32),
                        pltpu.VMEM((2, CHUNK, D_GROUP), F32), pltpu.VMEM((2, 7, CHUNK, LANES), F32)],
        compiler_params=_cparams(2),
        name="ssd_mixer",
    )(proj3, proj3, proj3, proj3, dt3, cw, cb, dtb, arow, dsk, ng, ef, eb, tri, shift)


def _route_rows(lg):
    lane = lax.broadcasted_iota(I32, lg.shape, 1).astype(F32)
    n_g = float(N_EXPERT_GROUPS)
    n_e = float(EXPERTS_PER_GROUP)
    gl = jnp.where(lane < n_g, lg, NEG)
    gmax = jnp.max(gl, axis=-1, keepdims=True)
    g_idx = jnp.min(jnp.where(gl == gmax, lane, float(LANES)), axis=-1, keepdims=True)
    g_p = 1.0 / jnp.sum(jnp.exp(gl - gmax), axis=-1, keepdims=True)
    first = n_g + n_e * g_idx
    el = jnp.where((lane >= first) & (lane < first + n_e), lg, NEG)
    e1 = jnp.max(el, axis=-1, keepdims=True)
    i1 = jnp.min(jnp.where(el == e1, lane, float(LANES)), axis=-1, keepdims=True)
    el2 = jnp.where(lane == i1, NEG, el)
    e2 = jnp.max(el2, axis=-1, keepdims=True)
    i2 = jnp.min(jnp.where(el2 == e2, lane, float(LANES)), axis=-1, keepdims=True)
    r = jnp.exp(e2 - e1)
    gate1 = g_p / (1.0 + r)
    gate2 = gate1 * r
    return jnp.where(lane == 0.0, i1 - n_g,
                     jnp.where(lane == 1.0, i2 - n_g,
                               jnp.where(lane == 2.0, gate1, jnp.where(lane == 3.0, gate2, 0.0))))


def _outproj_body(x_ref, a_ref, s_ref, ag_ref, wa_ref, ws_ref, g2_ref, wr_ref, br_ref, x2_ref, hp_ref, rt_ref):
    a = a_ref[...].astype(F32)
    ms = jnp.mean(a * a, axis=-1, keepdims=True)
    an = (a * lax.rsqrt(ms + EPS) * ag_ref[...]).astype(BF16)
    y = (jnp.dot(an, wa_ref[...], preferred_element_type=F32)
         + jnp.dot(s_ref[...], ws_ref[...], preferred_element_type=F32))
    x2 = x_ref[...] + y
    x2_ref[...] = x2
    ms2 = jnp.mean(x2 * x2, axis=-1, keepdims=True)
    hn = x2 * lax.rsqrt(ms2 + EPS) * g2_ref[...]
    hi = hn.astype(BF16)
    lo = (hn - hi.astype(F32)).astype(BF16)
    l1 = jnp.dot(hi, wr_ref[...], preferred_element_type=F32)
    l2 = jnp.dot(lo, wr_ref[:, :LANES], preferred_element_type=F32)
    rt_ref[...] = _route_rows(l1[:, :LANES] + l1[:, LANES:] + l2 + br_ref[...])
    _store_tile_rows(hp_ref, (), _pack_bf16_pairs(hi.astype(F32)))


def _outproj(x2d, attn2d, ssm2d, ag, wa, ws, g2, wr, br):
    t, d = x2d.shape
    tm = min(256, t)
    row = lambda w: pl.BlockSpec((tm, w), lambda i: (i, 0))
    full = lambda a: pl.BlockSpec(a.shape, lambda i: (0,) * a.ndim)
    return pl.pallas_call(
        _outproj_body,
        grid=(t // tm,),
        in_specs=[row(d), row(D_ATTN), row(D_SSM), full(ag), full(wa), full(ws), full(g2), full(wr), full(br)],
        out_specs=[row(d), pl.BlockSpec((tm * ROW_TILE, LANES), lambda i: (i, 0)), row(LANES)],
        out_shape=[jax.ShapeDtypeStruct((t, d), F32), jax.ShapeDtypeStruct((t * ROW_TILE, LANES), U32),
                   jax.ShapeDtypeStruct((t, LANES), F32)],
        compiler_params=_cparams(1),
        name="outproj_router",
    )(x2d, attn2d, ssm2d, ag, wa, ws, g2, wr, br)


def _moe_body(vblk_ref, vexp_ref, vlo_ref, vhi_ref, tok_ref, tok_next_ref, slot_ref, x_hbm, wg_ref, wu_ref, wd_ref,
              y_hbm, xbuf, ybuf, wgu_bf, wd_bf, cached_ref, gsem, ssem):
    blk = xbuf.shape[1] // ROW_TILE
    d_half = wg_ref.shape[1] // 2
    v = pl.program_id(0)
    lo = vlo_ref[v]
    hi = vhi_ref[v]
    s = vblk_ref[v]
    expert = vexp_ref[v]
    n_blocks = y_hbm.shape[0] // (blk * ROW_TILE)
    slot = s % 2
    other = 1 - slot

    def tile_rows(i):
        return pl.ds(pl.multiple_of(i * ROW_TILE, ROW_TILE), ROW_TILE)

    def start_gather(table_ref, buf_slot):
        def body(i, c):
            pltpu.make_async_copy(x_hbm.at[tile_rows(table_ref[0, 0, i])], xbuf.at[buf_slot, tile_rows(i)],
                                  gsem.at[buf_slot]).start()
            return c
        lax.fori_loop(0, blk, body, 0, unroll=MOE_DMA_UNROLL)

    def wait_gather(buf_slot):
        pltpu.make_async_copy(x_hbm.at[pl.ds(0, blk * ROW_TILE)], xbuf.at[buf_slot], gsem.at[buf_slot]).wait()

    def start_scatter(buf_slot):
        def body(i, c):
            pltpu.make_async_copy(ybuf.at[buf_slot, tile_rows(i)], y_hbm.at[tile_rows(slot_ref[0, 0, i])],
                                  ssem.at[buf_slot]).start()
            return c
        lax.fori_loop(0, blk, body, 0, unroll=MOE_DMA_UNROLL)

    def wait_scatter(buf_slot):
        pltpu.make_async_copy(ybuf.at[buf_slot], y_hbm.at[pl.ds(0, blk * ROW_TILE)], ssem.at[buf_slot]).wait()

    @pl.when(v == 0)
    def _():
        cached_ref[0] = -1

    @pl.when(hi > lo)
    def _():
        @pl.when(lo == 0)
        def _():
            @pl.when(s == 0)
            def _():
                start_gather(tok_ref, slot)
            wait_gather(slot)

            @pl.when(s + 1 < n_blocks)
            def _():
                start_gather(tok_next_ref, other)

        @pl.when(cached_ref[0] != expert)
        def _():
            rows = 256

            def cast_rows(i, c):
                r0 = pl.multiple_of(i * rows, rows)
                wgu_bf[pl.ds(r0, rows), :D_EXPERT] = wg_ref[0, pl.ds(r0, rows), :].astype(BF16)
                wgu_bf[pl.ds(r0, rows), D_EXPERT:] = wu_ref[0, pl.ds(r0, rows), :].astype(BF16)
                return c
            lax.fori_loop(0, 2 * d_half // rows, cast_rows, 0)
            wd_bf[...] = wd_ref[0].astype(BF16)
            cached_ref[0] = expert

        words = _load_tile_rows(xbuf, (slot,), blk)
        xlo = jnp.concatenate([_unpack_lo(w).astype(BF16) for w in words], axis=1)
        xhi = jnp.concatenate([_unpack_hi(w).astype(BF16) for w in words], axis=1)
        gu = (jnp.dot(xlo, wgu_bf[:d_half, :], preferred_element_type=F32)
              + jnp.dot(xhi, wgu_bf[d_half:, :], preferred_element_type=F32))
        gate = gu[:, :D_EXPERT]
        hid = (_silu(gate) * gu[:, D_EXPERT:]).astype(BF16)
        y = _pack_bf16_pairs(jnp.dot(hid, wd_bf[...], preferred_element_type=F32))

        @pl.when(lo == 0)
        def _():
            _store_tile_rows(ybuf, (slot,), y)

        @pl.when(lo > 0)
        def _():
            row = lax.broadcasted_iota(I32, (blk, 1), 0)
            old = jnp.concatenate(_load_tile_rows(ybuf, (slot,), blk), axis=1)
            _store_tile_rows(ybuf, (slot,), jnp.where((row >= lo) & (row < hi), y, old))

        @pl.when(hi == blk)
        def _():
            @pl.when(s > 0)
            def _():
                wait_scatter(other)
            start_scatter(slot)

            @pl.when(s == n_blocks - 1)
            def _():
                wait_scatter(slot)


def _moe(visits, row_tok, row_slot, hn_tiles, w_gate, w_up, w_down):
    blk = MOE_BLOCK
    n_assign = row_tok.shape[0]
    n_blocks = n_assign // blk
    n_visits = visits[0].shape[0]
    d = w_gate.shape[1]
    smem_rows = pl.BlockSpec((1, 1, blk), lambda v, vb, ve, vl, vh: (vb[v], 0, 0), memory_space=pltpu.SMEM)
    smem_next = pl.BlockSpec((1, 1, blk), lambda v, vb, ve, vl, vh: (jnp.minimum(vb[v] + 1, n_blocks - 1), 0, 0),
                             memory_space=pltpu.SMEM)
    by_expert = lambda shape: pl.BlockSpec((1,) + shape, lambda v, vb, ve, vl, vh: (ve[v], 0, 0))
    grid_spec = pltpu.PrefetchScalarGridSpec(
        num_scalar_prefetch=4,
        grid=(n_visits,),
        in_specs=[smem_rows, smem_next, smem_rows,
                  pl.BlockSpec(memory_space=pl.ANY),
                  by_expert((d, D_EXPERT)), by_expert((d, D_EXPERT)), by_expert((D_EXPERT, d))],
        out_specs=pl.BlockSpec(memory_space=pl.ANY),
        scratch_shapes=[pltpu.VMEM((2, blk * ROW_TILE, LANES), U32), pltpu.VMEM((2, blk * ROW_TILE, LANES), U32),
                        pltpu.VMEM((d, 2 * D_EXPERT), BF16), pltpu.VMEM((D_EXPERT, d), BF16),
                        pltpu.SMEM((1,), I32),
                        pltpu.SemaphoreType.DMA((2,)), pltpu.SemaphoreType.DMA((2,))],
    )
    tok3 = row_tok.reshape(n_blocks, 1, blk)
    return pl.pallas_call(
        _moe_body,
        grid_spec=grid_spec,
        out_shape=jax.ShapeDtypeStruct((n_assign * ROW_TILE, LANES), U32),
        compiler_params=_cparams(1),
        name="moe_experts",
    )(*visits, tok3, tok3, row_slot.reshape(n_blocks, 1, blk), hn_tiles, w_gate, w_up, w_down)


def _route_tables(route, t):
    blk = MOE_BLOCK
    n_assign = t * TOP_K
    n_blocks = n_assign // blk
    flat_e = route[:, :TOP_K].astype(I32).T.reshape(n_assign)
    order = jnp.argsort(flat_e, stable=True).astype(I32)
    counts = jnp.sum((flat_e[:, None] == jnp.arange(N_EXPERTS, dtype=I32)[None, :]).astype(I32), axis=0)
    ends = jnp.cumsum(counts)
    starts = ends - counts
    cuts = jnp.sort(jnp.concatenate([jnp.arange(n_blocks, dtype=I32) * blk, starts[1:]]))
    nxt = jnp.concatenate([cuts[1:], jnp.full((1,), n_assign, I32)])
    vblk = jnp.minimum(cuts // blk, n_blocks - 1)
    vexp = jnp.minimum(jnp.sum((ends[None, :] <= cuts[:, None]).astype(I32), axis=1), N_EXPERTS - 1)
    vlo = cuts - vblk * blk
    vhi = jnp.maximum(jnp.minimum(nxt, (vblk + 1) * blk) - vblk * blk, vlo)
    return (vblk, vexp, vlo, vhi), order % t, order


def _combine_body(x2_ref, y0_ref, y1_ref, rt_ref, o_ref):
    tm = o_ref.shape[0]
    half = o_ref.shape[1] // 2
    g0 = rt_ref[:, TOP_K:TOP_K + 1]
    g1 = rt_ref[:, TOP_K + 1:TOP_K + 2]
    y0 = _load_tile_rows(y0_ref, (), tm)
    y1 = _load_tile_rows(y1_ref, (), tm)
    for c in range(ROW_TILE):
        lo_cols = slice(c * LANES, (c + 1) * LANES)
        hi_cols = slice(half + c * LANES, half + (c + 1) * LANES)
        o_ref[:, lo_cols] = x2_ref[:, lo_cols] + g0 * _unpack_lo(y0[c]) + g1 * _unpack_lo(y1[c])
        o_ref[:, hi_cols] = x2_ref[:, hi_cols] + g0 * _unpack_hi(y0[c]) + g1 * _unpack_hi(y1[c])


def _combine(x2, y_slots, route):
    t, d = x2.shape
    tm = min(512, t)
    nb = t // tm
    return pl.pallas_call(
        _combine_body,
        grid=(nb,),
        in_specs=[pl.BlockSpec((tm, d), lambda i: (i, 0)),
                  pl.BlockSpec((tm * ROW_TILE, LANES), lambda i: (i, 0)),
                  pl.BlockSpec((tm * ROW_TILE, LANES), lambda i: (i + nb, 0)),
                  pl.BlockSpec((tm, LANES), lambda i: (i, 0))],
        out_specs=pl.BlockSpec((tm, d), lambda i: (i, 0)),
        out_shape=jax.ShapeDtypeStruct((t, d), F32),
        compiler_params=_cparams(1),
        name="moe_combine",
    )(x2, y_slots, y_slots, route)


def _prepare(norm1_g, w_in, q_norm_g, k_norm_g, rpb, attn_out_g, conv_w, conv_b, a_log_f, a_log_b,
             dt_bias_f, dt_bias_b, d_skip, ssm_norm_g, w_out, norm2_g, w_router_group, b_router_group,
             w_router_expert, b_router_expert, w_gate, w_up, w_down):
    d = w_in.shape[0]
    hpg = HEADS_PER_GROUP
    p = {}
    p["norm1_g"] = norm1_g.reshape(1, d).astype(F32)
    p["w_main"] = w_in[:, :D_PROJ_MAIN].astype(BF16)
    w_dt = w_in[:, D_PROJ_MAIN:]
    zeros = jnp.zeros((d, LANES - 2 * hpg), w_in.dtype)
    per_group = lambda v, g: v[..., g * hpg:(g + 1) * hpg]
    p["w_dt"] = jnp.concatenate(
        [jnp.concatenate([per_group(w_dt[:, :N_HEADS_SSM], g), per_group(w_dt[:, N_HEADS_SSM:], g), zeros], axis=1)
         for g in range(N_GROUPS_SSM)], axis=1).astype(BF16)
    lane_rows = lambda f, bwd: jnp.stack(
        [jnp.concatenate([per_group(f, g), per_group(bwd, g), jnp.zeros((LANES - 2 * hpg,), F32)])
         for g in range(N_GROUPS_SSM)])[:, None, :]
    p["dtb"] = lane_rows(dt_bias_f.astype(F32), dt_bias_b.astype(F32))
    p["arow"] = lane_rows(-jnp.exp(a_log_f.astype(F32)), -jnp.exp(a_log_b.astype(F32)))
    scale = HEAD_DIM ** -0.5
    p["qg2"] = (jnp.tile(q_norm_g.astype(F32), 2) * scale).reshape(1, LANES)
    p["kg2"] = jnp.tile(k_norm_g.astype(F32), 2).reshape(1, LANES)
    lane = jnp.arange(LANES)
    p["e_mat"] = (lane[:, None] // HEAD_DIM == lane[None, :] // HEAD_DIM).astype(BF16)
    p["bias"] = _attention_bias(rpb)
    p["attn_out_g"] = attn_out_g.reshape(1, D_ATTN).astype(F32)
    cw = conv_w.reshape(D_CONV, -1).astype(F32)
    cbias = conv_b.reshape(1, -1).astype(F32)
    group_cols = lambda a, g: jnp.concatenate(
        [a[:, g * D_GROUP:(g + 1) * D_GROUP],
         a[:, D_SSM + g * D_STATE:D_SSM + (g + 1) * D_STATE],
         a[:, D_SSM + D_BC + g * D_STATE:D_SSM + D_BC + (g + 1) * D_STATE]], axis=1)
    p["cw"] = jnp.stack([group_cols(cw, g) for g in range(N_GROUPS_SSM)])
    p["cb"] = jnp.stack([group_cols(cbias, g) for g in range(N_GROUPS_SSM)])
    p["dsk"] = jnp.repeat(d_skip.astype(F32), SSM_HEAD_DIM).reshape(N_GROUPS_SSM, 1, D_GROUP)
    p["ng"] = ssm_norm_g.astype(F32).reshape(N_GROUPS_SSM, 1, D_GROUP)
    col_head = jnp.arange(D_GROUP) // SSM_HEAD_DIM
    p["ef"] = (lane[:, None] == col_head[None, :]).astype(BF16)[None]
    p["eb"] = (lane[:, None] == col_head[None, :] + hpg).astype(BF16)[None]
    pos = jnp.arange(CHUNK)
    p["tri"] = jnp.concatenate([pos[:, None] >= pos[None, :], pos[:, None] <= pos[None, :]], axis=0).astype(BF16)
    src = jnp.arange(CHUNK + 2 * CONV_HALO)
    p["shift"] = jnp.concatenate(
        [src[None, :] == pos[:, None] + (CONV_HALO - D_CONV // 2 + k) for k in range(D_CONV) if k != D_CONV // 2],
        axis=0).astype(BF16)
    p["wa"] = w_out[:D_ATTN].astype(BF16)
    p["ws"] = w_out[D_ATTN:].astype(BF16)
    p["norm2_g"] = norm2_g.reshape(1, d).astype(F32)
    n_r = N_EXPERT_GROUPS + N_EXPERTS
    wr = jnp.concatenate([w_router_group, w_router_expert, jnp.zeros((d, LANES - n_r), F32)], axis=1).astype(F32)
    wr_hi = wr.astype(BF16)
    wr_lo = (wr - wr_hi.astype(F32)).astype(BF16)
    p["wr"] = jnp.concatenate([wr_hi, wr_lo], axis=1)
    p["br"] = jnp.concatenate([b_router_group, b_router_expert, jnp.zeros((LANES - n_r,), F32)]).reshape(1, LANES)
    p["w_gate"], p["w_up"], p["w_down"] = w_gate, w_up, w_down
    return p


def _layer(x, p):
    b, l, d = x.shape
    t = b * l
    x2d = x.reshape(t, d)
    proj, dt = _inproj(x2d, p["norm1_g"], p["w_main"], p["w_dt"])
    proj3 = proj.reshape(b, l, D_PROJ_MAIN)
    attn = _attention(proj3, p["qg2"], p["kg2"], p["e_mat"], p["bias"])
    ssm = _ssd(proj3, dt.reshape(b, l, N_GROUPS_SSM * LANES), p["cw"], p["cb"], p["dtb"], p["arow"],
               p["dsk"], p["ng"], p["ef"], p["eb"], p["tri"], p["shift"])
    x2, hn_packed, route = _outproj(x2d, attn.reshape(t, D_ATTN), ssm.reshape(t, D_SSM), p["attn_out_g"],
                                    p["wa"], p["ws"], p["norm2_g"], p["wr"], p["br"])
    visits, row_tok, row_slot = _route_tables(route, t)
    y_slots = _moe(visits, row_tok, row_slot, hn_packed, p["w_gate"], p["w_up"], p["w_down"])
    return _combine(x2, y_slots, route).reshape(b, l, d)


def kernel(x_prompt, x_sample, norm1_g, w_in, q_norm_g, k_norm_g, rpb, attn_out_g, conv_w, conv_b, a_log_f,
           a_log_b, dt_bias_f, dt_bias_b, d_skip, ssm_norm_g, w_out, norm2_g, w_router_group, b_router_group,
           w_router_expert, b_router_expert, w_gate, w_up, w_down):
    weights = (norm1_g, w_in, q_norm_g, k_norm_g, rpb, attn_out_g, conv_w, conv_b, a_log_f, a_log_b, dt_bias_f,
               dt_bias_b, d_skip, ssm_norm_g, w_out, norm2_g, w_router_group, b_router_group, w_router_expert,
               b_router_expert, w_gate, w_up, w_down)
    assert all(w.shape[0] == 1 for w in weights), "one layer of stacked weights expected"
    p = _prepare(*(w[0] for w in weights))
    return (_layer(x_prompt, p), _layer(x_sample, p))
```

```python
import jax
import jax.numpy as jnp
from jax import lax
from jax.experimental import pallas as pl
from jax.experimental.pallas import tpu as pltpu

F32 = jnp.float32
BF16 = jnp.bfloat16
U32 = jnp.uint32
I32 = jnp.int32

EPS = 1e-6
GRID_W = 64
N_HEADS_ATTN = 16
HEAD_DIM = 64
D_ATTN = N_HEADS_ATTN * HEAD_DIM
WIN_H = 8
WIN_W = 16
N_HEADS_SSM = 16
SSM_HEAD_DIM = 64
D_SSM = N_HEADS_SSM * SSM_HEAD_DIM
N_GROUPS_SSM = 2
HEADS_PER_GROUP = N_HEADS_SSM // N_GROUPS_SSM
D_GROUP = D_SSM // N_GROUPS_SSM
D_STATE = 128
D_CONV = 5
CHUNK = 128
D_BC = N_GROUPS_SSM * D_STATE
N_EXPERT_GROUPS = 4
EXPERTS_PER_GROUP = 8
N_EXPERTS = N_EXPERT_GROUPS * EXPERTS_PER_GROUP
TOP_K = 2
D_EXPERT = 512

LANES = 128
BF16_ROWS = 16
CONV_HALO = BF16_ROWS
NEG = -1e30
VMEM_LIMIT_BYTES = 56 * 1024 * 1024
MOE_BLOCK = 256
ATTN_ROW_UNROLL = 8
MOE_DMA_UNROLL = 8
ROW_TILE = 8
OUTPROJ_SUB_ROWS = 256

COL_Q, COL_K, COL_V = 0, D_ATTN // LANES, 2 * D_ATTN // LANES
COL_Z = 3 * D_ATTN // D_GROUP
COL_XS = (3 * D_ATTN + D_SSM) // D_GROUP
COL_B = (3 * D_ATTN + 2 * D_SSM) // LANES
COL_C = COL_B + D_BC // LANES
D_PROJ_MAIN = 3 * D_ATTN + 2 * D_SSM + 2 * D_BC


def _cparams(n_axes):
    return pltpu.CompilerParams(dimension_semantics=("arbitrary",) * n_axes,
                                vmem_limit_bytes=VMEM_LIMIT_BYTES)


def _silu(x):
    return x * (0.5 * jnp.tanh(0.5 * x) + 0.5)


def _split3(x):
    hi = x.astype(BF16)
    r1 = x - hi.astype(F32)
    mid = r1.astype(BF16)
    lo = (r1 - mid.astype(F32)).astype(BF16)
    return hi, mid, lo


def _pack_bf16_pairs(x):
    n = x.shape[1] // 2
    u = lax.bitcast_convert_type(x.astype(BF16).astype(F32), U32)
    return (u[:, :n] >> 16) | u[:, n:]


def _unpack_lo(u):
    return lax.bitcast_convert_type(u << 16, F32)


def _unpack_hi(u):
    return lax.bitcast_convert_type(u & jnp.uint32(0xFFFF0000), F32)


def _store_tile_rows(ref, lead, packed):
    m = packed.shape[0]
    for c in range(ROW_TILE):
        ref[lead + (pl.ds(c, m, stride=ROW_TILE), slice(None))] = packed[:, c * LANES:(c + 1) * LANES]


def _load_tile_rows(ref, lead, m):
    return [ref[lead + (pl.ds(c, m, stride=ROW_TILE), slice(None))] for c in range(ROW_TILE)]


def _inproj_body(x_ref, g_ref, w_ref, wdt_ref, o_ref, dt_ref, hn_ref):
    tm = x_ref.shape[0]
    rows = min(tm, 256)

    @pl.when(pl.program_id(1) == 0)
    def _():
        def norm_rows(i, c):
            r0 = pl.multiple_of(i * rows, rows)
            x = x_ref[pl.ds(r0, rows), :]
            ms = jnp.mean(x * x, axis=-1, keepdims=True)
            hn_ref[pl.ds(r0, rows), :] = (x * lax.rsqrt(ms + EPS) * g_ref[...]).astype(BF16)
            return c
        lax.fori_loop(0, tm // rows, norm_rows, 0)
        dt_ref[...] = jnp.dot(hn_ref[...], wdt_ref[...], preferred_element_type=F32)

    o_ref[...] = jnp.dot(hn_ref[...], w_ref[...], preferred_element_type=F32).astype(BF16)


def _inproj(x2d, gain, w_main, w_dt):
    t, d = x2d.shape
    n = w_main.shape[1]
    ndt = w_dt.shape[1]
    tm = min(1024, t)
    tn = 512
    return pl.pallas_call(
        _inproj_body,
        grid=(t // tm, n // tn),
        in_specs=[pl.BlockSpec((tm, d), lambda i, j: (i, 0)),
                  pl.BlockSpec((1, d), lambda i, j: (0, 0)),
                  pl.BlockSpec((d, tn), lambda i, j: (0, j)),
                  pl.BlockSpec((d, ndt), lambda i, j: (0, 0))],
        out_specs=[pl.BlockSpec((tm, tn), lambda i, j: (i, j)),
                   pl.BlockSpec((tm, ndt), lambda i, j: (i, 0))],
        out_shape=[jax.ShapeDtypeStruct((t, n), BF16), jax.ShapeDtypeStruct((t, ndt), F32)],
        scratch_shapes=[pltpu.VMEM((tm, d), BF16)],
        compiler_params=_cparams(2),
        name="inproj",
    )(x2d, gain, w_main, w_dt)


def _attn_body(q_ref, k_ref, v_ref, qg_ref, kg_ref, e_ref, bias_ref, o_ref, qs, ks, s_scr, m_scr):
    l = q_ref.shape[1]
    n_rows = l // GRID_W
    win_keys = WIN_H * GRID_W
    ch = min(l, 512)
    head_a = lax.broadcasted_iota(I32, (1, LANES), 1) < HEAD_DIM
    sel_a = jnp.where(head_a, 1.0, 0.0).astype(BF16)
    sel_b = jnp.where(head_a, 0.0, 1.0).astype(BF16)

    def norm_rows(i, c):
        r0 = pl.multiple_of(i * ch, ch)
        for src, gref, dst in ((q_ref, qg_ref, qs), (k_ref, kg_ref, ks)):
            x = src[0, pl.ds(r0, ch), :].astype(F32)
            ssq = jnp.dot((x * x).astype(BF16), e_ref[...], preferred_element_type=F32)
            dst[pl.ds(r0, ch), :] = (x * lax.rsqrt(ssq * (1.0 / HEAD_DIM) + EPS) * gref[...]).astype(BF16)
        return c
    lax.fori_loop(0, l // ch, norm_rows, 0)

    def key_start(r):
        return jnp.clip(r - WIN_H // 2, 0, n_rows - WIN_H)

    def score_stage(r, slot):
        rs = key_start(r)
        q_r = qs[pl.ds(pl.multiple_of(r * GRID_W, GRID_W), GRID_W), :]
        qm = jnp.concatenate([q_r * sel_a, q_r * sel_b], axis=0)
        kb = ks[pl.ds(pl.multiple_of(rs * GRID_W, GRID_W), win_keys), :]
        s = lax.dot_general(qm, kb, (((1,), (1,)), ((), ())), preferred_element_type=F32)
        dr0 = rs - r + (WIN_H - 1)
        lane0 = pl.multiple_of((dr0 // 2) * LANES, LANES)
        s = s + bias_ref[0, dr0 % 2, :, pl.ds(lane0, win_keys)]
        s_scr[slot] = s
        m_scr[slot] = jnp.max(s, axis=-1, keepdims=True)

    def value_stage(r, slot):
        rs = key_start(r)
        vb = v_ref[0, pl.ds(pl.multiple_of(rs * GRID_W, GRID_W), win_keys), :]
        p = jnp.exp(s_scr[slot] - m_scr[slot])
        den = jnp.sum(p, axis=-1, keepdims=True)
        o = jnp.dot(p.astype(BF16), vb, preferred_element_type=F32) * (1.0 / den)
        out = jnp.where(head_a, o[:GRID_W], o[GRID_W:])
        o_ref[0, pl.ds(pl.multiple_of(r * GRID_W, GRID_W), GRID_W), :] = out.astype(BF16)

    score_stage(0, 0)

    def row_group(i, c):
        r = ATTN_ROW_UNROLL * i
        for j in range(ATTN_ROW_UNROLL):
            score_stage(jnp.minimum(r + j + 1, n_rows - 1), (j + 1) % 2)
            value_stage(r + j, j % 2)
        return c
    lax.fori_loop(0, n_rows // ATTN_ROW_UNROLL, row_group, 0)


def _attention(proj3, qg2, kg2, e_mat, bias):
    b, l, _ = proj3.shape
    n_pairs = N_HEADS_ATTN // 2
    blk = (1, l, LANES)
    return pl.pallas_call(
        _attn_body,
        grid=(n_pairs, b),
        in_specs=[pl.BlockSpec(blk, lambda hp, bi: (bi, 0, COL_Q + hp)),
                  pl.BlockSpec(blk, lambda hp, bi: (bi, 0, COL_K + hp)),
                  pl.BlockSpec(blk, lambda hp, bi: (bi, 0, COL_V + hp)),
                  pl.BlockSpec((1, LANES), lambda hp, bi: (0, 0)),
                  pl.BlockSpec((1, LANES), lambda hp, bi: (0, 0)),
                  pl.BlockSpec((LANES, LANES), lambda hp, bi: (0, 0)),
                  pl.BlockSpec((1,) + bias.shape[1:], lambda hp, bi: (hp, 0, 0, 0))],
        out_specs=pl.BlockSpec(blk, lambda hp, bi: (bi, 0, hp)),
        out_shape=jax.ShapeDtypeStruct((b, l, D_ATTN), BF16),
        scratch_shapes=[pltpu.VMEM((l, LANES), BF16), pltpu.VMEM((l, LANES), BF16),
                        pltpu.VMEM((2, LANES, WIN_H * GRID_W), F32), pltpu.VMEM((2, LANES, 1), F32)],
        compiler_params=_cparams(2),
        name="nbr_attention",
    )(proj3, proj3, proj3, qg2, kg2, e_mat, bias)


def _attention_bias(rpb):
    n_dr = 2 * WIN_H - 1
    cols = jnp.arange(GRID_W, dtype=I32)
    col_start = jnp.clip(cols - WIN_W // 2, 0, GRID_W - WIN_W)
    keys = cols[None, :]
    valid = (keys >= col_start[:, None]) & (keys < col_start[:, None] + WIN_W)
    dc = jnp.clip(keys - cols[:, None] + (WIN_W - 1), 0, 2 * WIN_W - 2)
    tab = jnp.where(valid[None, None], rpb.astype(F32)[:, :, dc], NEG)
    flat = jnp.transpose(tab, (0, 2, 1, 3)).reshape(N_HEADS_ATTN // 2, 2 * GRID_W, n_dr * GRID_W)
    moved = jnp.concatenate([flat[:, :, GRID_W:], jnp.full(flat.shape[:2] + (GRID_W,), NEG, F32)], axis=2)
    return jnp.stack([flat, moved], axis=1)


def _ssd_body(z_ref, xs_ref, b_ref, c_ref, dt_ref, cw_ref, cb_ref, dtb_ref, arow_ref, dsk_ref, ng_ref,
              ef_ref, eb_ref, tri_ref, shift_ref, o_ref, xs_s, b_s, c_s, y_s, sf_s, sb_s, x_f, pre_s):
    l = xs_ref.shape[1]
    n_chunks = l // CHUNK
    hpg = HEADS_PER_GROUP
    ii = lax.broadcasted_iota(I32, (CHUNK, CHUNK), 0)
    jj = lax.broadcasted_iota(I32, (CHUNK, CHUNK), 1)
    causal = ii >= jj
    anti = ii <= jj
    head_a = lax.broadcasted_iota(I32, (1, LANES), 1) < SSM_HEAD_DIM

    def conv_chunk(r0):
        p0 = pl.multiple_of(jnp.maximum(r0 - CONV_HALO, 0), CONV_HALO)
        n0 = pl.multiple_of(jnp.minimum(r0 + CHUNK, l - CONV_HALO), CONV_HALO)
        has_prev = r0 > 0
        has_next = r0 + CHUNK < l
        parts = []
        for ref in (xs_ref, b_ref, c_ref):
            prev = ref[0, pl.ds(p0, CONV_HALO), :]
            nxt = ref[0, pl.ds(n0, CONV_HALO), :]
            parts.append(jnp.concatenate([jnp.where(has_prev, prev, jnp.zeros_like(prev)),
                                          ref[0, pl.ds(r0, CHUNK), :],
                                          jnp.where(has_next, nxt, jnp.zeros_like(nxt))], axis=0))
        ext = jnp.concatenate(parts, axis=1)
        shifted = jnp.dot(shift_ref[...], ext, preferred_element_type=F32)
        mid = D_CONV // 2
        acc = cb_ref[0] + ext[CONV_HALO:CONV_HALO + CHUNK].astype(F32) * cw_ref[0, mid:mid + 1, :]
        for n, k in enumerate(k for k in range(D_CONV) if k != mid):
            acc = acc + shifted[n * CHUNK:(n + 1) * CHUNK] * cw_ref[0, k:k + 1, :]
        return _silu(acc)

    def dt_terms(r0):
        raw = dt_ref[0, pl.ds(r0, CHUNK), :] + dtb_ref[0]
        dtv = jnp.maximum(raw, 0.0) + jnp.log(1.0 + jnp.exp(-jnp.abs(raw)))
        adt = dtv * arow_ref[0]
        hi, mid, lo = _split3(adt)
        cs3 = jnp.dot(tri_ref[...], jnp.concatenate([hi, mid, lo], axis=1), preferred_element_type=F32)
        cs = cs3[:, :LANES] + cs3[:, LANES:2 * LANES] + cs3[:, 2 * LANES:]
        return dtv, cs[:CHUNK], cs[CHUNK:]

    def expand_exact(row, e_ref):
        hi, mid, lo = _split3(jnp.broadcast_to(row, (8, LANES)))
        e = e_ref[0]
        r = (jnp.dot(hi, e, preferred_element_type=F32) + jnp.dot(mid, e, preferred_element_type=F32)
             + jnp.dot(lo, e, preferred_element_type=F32))
        return r[0:1]

    def state_terms(cc, bt, x, dtv, cs, tot, e_ref, s_ref):
        e = e_ref[0]
        expcs = jnp.dot(jnp.exp(cs).astype(BF16), e, preferred_element_type=F32)
        y_off = jnp.dot(cc, s_ref[...].astype(BF16), preferred_element_type=F32) * expcs
        scl = jnp.dot((dtv * jnp.exp(tot - cs)).astype(BF16), e, preferred_element_type=F32)
        states_t = jnp.dot(bt, (x * scl).astype(BF16), preferred_element_type=F32)
        s_ref[...] = s_ref[...] * expand_exact(jnp.exp(tot), e_ref) + states_t
        return y_off

    sf_s[...] = jnp.zeros_like(sf_s)
    sb_s[...] = jnp.zeros_like(sb_s)

    def prep_stage(c, slot):
        r0 = pl.multiple_of(c * CHUNK, CHUNK)
        xbc = conv_chunk(r0)
        x = xbc[:, :D_GROUP]
        bm = xbc[:, D_GROUP:D_GROUP + D_STATE]
        xs_s[pl.ds(r0, CHUNK), :] = x.astype(BF16)
        b_s[pl.ds(r0, CHUNK), :] = bm.astype(BF16)
        c_s[pl.ds(r0, CHUNK), :] = xbc[:, D_GROUP + D_STATE:].astype(BF16)
        x_f[slot] = x
        dtv, cs_f, cs_b = dt_terms(r0)
        for n, a in enumerate((dtv, cs_f, cs_b, cs_f.T, cs_b.T, dtv.T, bm.T)):
            pre_s[slot, n] = a

    def mix_stage(c, slot):
        r0 = pl.multiple_of(c * CHUNK, CHUNK)
        x = x_f[slot]
        xb = xs_s[pl.ds(r0, CHUNK), :]
        bb = b_s[pl.ds(r0, CHUNK), :]
        cc = c_s[pl.ds(r0, CHUNK), :]
        dtv, cs_f, cs_b, cst_f, cst_b, dtt, bmt = (pre_s[slot, n] for n in range(7))
        cb = lax.dot_general(cc, bb, (((1,), (1,)), ((), ())), preferred_element_type=F32)
        pieces = []
        for pair in range(hpg // 2):
            xp = xb[:, pair * LANES:(pair + 1) * LANES]
            ys = []
            for hh in range(2):
                h = 2 * pair + hh
                df = cs_f[:, h:h + 1] - cst_f[h:h + 1, :]
                db = cs_b[:, hpg + h:hpg + h + 1] - cst_b[hpg + h:hpg + h + 1, :]
                lf = jnp.exp(jnp.where(causal, df, NEG)) * dtt[h:h + 1, :]
                lb = jnp.exp(jnp.where(anti, db, NEG)) * dtt[hpg + h:hpg + h + 1, :]
                m = (cb * (lf + lb)).astype(BF16)
                ys.append(jnp.dot(m, xp, preferred_element_type=F32))
            pieces.append(jnp.where(head_a, ys[0], ys[1]))
        y = jnp.concatenate(pieces, axis=1) + x * dsk_ref[0]
        y = y + state_terms(cc, bmt.astype(BF16), x, dtv, cs_f, cs_f[CHUNK - 1:CHUNK, :], ef_ref, sf_s)
        y_s[pl.ds(r0, CHUNK), :] = y

    prep_stage(0, 0)

    def fwd_pair(i, carry):
        c = 2 * i
        prep_stage(c + 1, 1)
        mix_stage(c, 0)
        prep_stage(jnp.minimum(c + 2, n_chunks - 1), 0)
        mix_stage(c + 1, 1)
        return carry
    lax.fori_loop(0, n_chunks // 2, fwd_pair, 0)

    def bwd_prep_stage(c, slot):
        r0 = pl.multiple_of(c * CHUNK, CHUNK)
        dtv, _, cs_b = dt_terms(r0)
        pre_s[slot, 0] = dtv
        pre_s[slot, 2] = cs_b
        pre_s[slot, 6] = b_s[pl.ds(r0, CHUNK), :].astype(F32).T

    def bwd_mix_stage(c, slot):
        r0 = pl.multiple_of(c * CHUNK, CHUNK)
        x = xs_s[pl.ds(r0, CHUNK), :].astype(F32)
        cc = c_s[pl.ds(r0, CHUNK), :]
        dtv, cs_b, bt = pre_s[slot, 0], pre_s[slot, 2], pre_s[slot, 6].astype(BF16)
        y = y_s[pl.ds(r0, CHUNK), :] + state_terms(cc, bt, x, dtv, cs_b, cs_b[0:1, :], eb_ref, sb_s)
        y = y * _silu(z_ref[0, pl.ds(r0, CHUNK), :].astype(F32))
        ms = jnp.mean(y * y, axis=-1, keepdims=True)
        o_ref[0, pl.ds(r0, CHUNK), :] = (y * lax.rsqrt(ms + EPS) * ng_ref[0]).astype(BF16)

    bwd_prep_stage(n_chunks - 1, 0)

    def bwd_pair(i, carry):
        c = n_chunks - 1 - 2 * i
        bwd_prep_stage(c - 1, 1)
        bwd_mix_stage(c, 0)
        bwd_prep_stage(jnp.maximum(c - 2, 0), 0)
        bwd_mix_stage(c - 1, 1)
        return carry
    lax.fori_loop(0, n_chunks // 2, bwd_pair, 0)


def _ssd(proj3, dt3, cw, cb, dtb, arow, dsk, ng, ef, eb, tri, shift):
    b, l, _ = proj3.shape
    g = N_GROUPS_SSM
    wconv = D_GROUP + 2 * D_STATE
    per_group = lambda shape: pl.BlockSpec((1,) + shape, lambda bi, gi: (gi,) + (0,) * len(shape))
    return pl.pallas_call(
        _ssd_body,
        grid=(b, g),
        in_specs=[pl.BlockSpec((1, l, D_GROUP), lambda bi, gi: (bi, 0, COL_Z + gi)),
                  pl.BlockSpec((1, l, D_GROUP), lambda bi, gi: (bi, 0, COL_XS + gi)),
                  pl.BlockSpec((1, l, D_STATE), lambda bi, gi: (bi, 0, COL_B + gi)),
                  pl.BlockSpec((1, l, D_STATE), lambda bi, gi: (bi, 0, COL_C + gi)),
                  pl.BlockSpec((1, l, LANES), lambda bi, gi: (bi, 0, gi)),
                  per_group((D_CONV, wconv)), per_group((1, wconv)),
                  per_group((1, LANES)), per_group((1, LANES)),
                  per_group((1, D_GROUP)), per_group((1, D_GROUP)),
                  pl.BlockSpec((1, LANES, D_GROUP), lambda bi, gi: (0, 0, 0)),
                  pl.BlockSpec((1, LANES, D_GROUP), lambda bi, gi: (0, 0, 0)),
                  pl.BlockSpec((2 * CHUNK, CHUNK), lambda bi, gi: (0, 0)),
                  pl.BlockSpec(shift.shape, lambda bi, gi: (0, 0))],
        out_specs=pl.BlockSpec((1, l, D_GROUP), lambda bi, gi: (bi, 0, gi)),
        out_shape=jax.ShapeDtypeStruct((b, l, D_SSM), BF16),
        scratch_shapes=[pltpu.VMEM((l, D_GROUP), BF16), pltpu.VMEM((l, D_STATE), BF16),
                        pltpu.VMEM((l, D_STATE), BF16), pltpu.VMEM((l, D_GROUP), F32),
                        pltpu.VMEM((D_STATE, D_GROUP), F32), pltpu.VMEM((D_STATE, D_GROUP), F32),
                        pltpu.VMEM((2, CHUNK, D_GROUP), F32), pltpu.VMEM((2, 7, CHUNK, LANES), F32)],
        compiler_params=_cparams(2),
        name="ssd_mixer",
    )(proj3, proj3, proj3, proj3, dt3, cw, cb, dtb, arow, dsk, ng, ef, eb, tri, shift)


def _route_rows(lg):
    lane = lax.broadcasted_iota(I32, lg.shape, 1).astype(F32)
    n_g = float(N_EXPERT_GROUPS)
    n_e = float(EXPERTS_PER_GROUP)
    gl = jnp.where(lane < n_g, lg, NEG)
    gmax = jnp.max(gl, axis=-1, keepdims=True)
    g_idx = jnp.min(jnp.where(gl == gmax, lane, float(LANES)), axis=-1, keepdims=True)
    g_p = 1.0 / jnp.sum(jnp.exp(gl - gmax), axis=-1, keepdims=True)
    first = n_g + n_e * g_idx
    el = jnp.where((lane >= first) & (lane < first + n_e), lg, NEG)
    e1 = jnp.max(el, axis=-1, keepdims=True)
    i1 = jnp.min(jnp.where(el == e1, lane, float(LANES)), axis=-1, keepdims=True)
    el2 = jnp.where(lane == i1, NEG, el)
    e2 = jnp.max(el2, axis=-1, keepdims=True)
    i2 = jnp.min(jnp.where(el2 == e2, lane, float(LANES)), axis=-1, keepdims=True)
    r = jnp.exp(e2 - e1)
    gate1 = g_p / (1.0 + r)
    gate2 = gate1 * r
    return jnp.where(lane == 0.0, i1 - n_g,
                     jnp.where(lane == 1.0, i2 - n_g,
                               jnp.where(lane == 2.0, gate1, jnp.where(lane == 3.0, gate2, 0.0))))


def _outproj_body(x_ref, a_ref, s_ref, ag_ref, wa_ref, ws_ref, g2_ref, wr_ref, br_ref, x2_ref, hp_ref, rt_ref):
    tm = x_ref.shape[0]
    sub = min(tm, OUTPROJ_SUB_ROWS)
    for r0 in range(0, tm, sub):
        rows = slice(r0, r0 + sub)
        a = a_ref[rows, :].astype(F32)
        ms = jnp.mean(a * a, axis=-1, keepdims=True)
        an = (a * lax.rsqrt(ms + EPS) * ag_ref[...]).astype(BF16)
        y = (jnp.dot(an, wa_ref[...], preferred_element_type=F32)
             + jnp.dot(s_ref[rows, :], ws_ref[...], preferred_element_type=F32))
        x2 = x_ref[rows, :] + y
        x2_ref[rows, :] = x2
        ms2 = jnp.mean(x2 * x2, axis=-1, keepdims=True)
        hn = x2 * lax.rsqrt(ms2 + EPS) * g2_ref[...]
        hi = hn.astype(BF16)
        lo = (hn - hi.astype(F32)).astype(BF16)
        l1 = jnp.dot(hi, wr_ref[...], preferred_element_type=F32)
        l2 = jnp.dot(lo, wr_ref[:, :LANES], preferred_element_type=F32)
        rt_ref[rows, :] = _route_rows(l1[:, :LANES] + l1[:, LANES:] + l2 + br_ref[...])
        packed = _pack_bf16_pairs(hi.astype(F32))
        for c in range(ROW_TILE):
            hp_ref[pl.ds(r0 * ROW_TILE + c, sub, stride=ROW_TILE), :] = packed[:, c * LANES:(c + 1) * LANES]


def _outproj(x2d, attn2d, ssm2d, ag, wa, ws, g2, wr, br):
    t, d = x2d.shape
    tm = min(2 * OUTPROJ_SUB_ROWS, t)
    row = lambda w: pl.BlockSpec((tm, w), lambda i: (i, 0))
    full = lambda a: pl.BlockSpec(a.shape, lambda i: (0,) * a.ndim)
    return pl.pallas_call(
        _outproj_body,
        grid=(t // tm,),
        in_specs=[row(d), row(D_ATTN), row(D_SSM), full(ag), full(wa), full(ws), full(g2), full(wr), full(br)],
        out_specs=[row(d), pl.BlockSpec((tm * ROW_TILE, LANES), lambda i: (i, 0)), row(LANES)],
        out_shape=[jax.ShapeDtypeStruct((t, d), F32), jax.ShapeDtypeStruct((t * ROW_TILE, LANES), U32),
                   jax.ShapeDtypeStruct((t, LANES), F32)],
        compiler_params=_cparams(1),
        name="outproj_router",
    )(x2d, attn2d, ssm2d, ag, wa, ws, g2, wr, br)


def _moe_body(vblk_ref, vexp_ref, vlo_ref, vhi_ref, tok_ref, tok_next_ref, slot_ref, x_hbm, wg_ref, wu_ref, wd_ref,
              y_hbm, xbuf, ybuf, wgu_bf, wd_bf, cached_ref, gsem, ssem):
    blk = xbuf.shape[1] // ROW_TILE
    d_half = wg_ref.shape[1] // 2
    v = pl.program_id(0)
    lo = vlo_ref[v]
    hi = vhi_ref[v]
    s = vblk_ref[v]
    expert = vexp_ref[v]
    n_blocks = y_hbm.shape[0] // (blk * ROW_TILE)
    slot = s % 2
    other = 1 - slot

    def tile_rows(i):
        return pl.ds(pl.multiple_of(i * ROW_TILE, ROW_TILE), ROW_TILE)

    def start_gather(table_ref, buf_slot):
        def body(i, c):
            pltpu.make_async_copy(x_hbm.at[tile_rows(table_ref[0, 0, i])], xbuf.at[buf_slot, tile_rows(i)],
                                  gsem.at[buf_slot]).start()
            return c
        lax.fori_loop(0, blk, body, 0, unroll=MOE_DMA_UNROLL)

    def wait_gather(buf_slot):
        pltpu.make_async_copy(x_hbm.at[pl.ds(0, blk * ROW_TILE)], xbuf.at[buf_slot], gsem.at[buf_slot]).wait()

    def start_scatter(buf_slot):
        def body(i, c):
            pltpu.make_async_copy(ybuf.at[buf_slot, tile_rows(i)], y_hbm.at[tile_rows(slot_ref[0, 0, i])],
                                  ssem.at[buf_slot]).start()
            return c
        lax.fori_loop(0, blk, body, 0, unroll=MOE_DMA_UNROLL)

    def wait_scatter(buf_slot):
        pltpu.make_async_copy(ybuf.at[buf_slot], y_hbm.at[pl.ds(0, blk * ROW_TILE)], ssem.at[buf_slot]).wait()

    @pl.when(v == 0)
    def _():
        cached_ref[0] = -1

    @pl.when(hi > lo)
    def _():
        @pl.when(lo == 0)
        def _():
            @pl.when(s == 0)
            def _():
                start_gather(tok_ref, slot)
            wait_gather(slot)

            @pl.when(s + 1 < n_blocks)
            def _():
                start_gather(tok_next_ref, other)

        @pl.when(cached_ref[0] != expert)
        def _():
            rows = 256

            def cast_rows(i, c):
                r0 = pl.multiple_of(i * rows, rows)
                wgu_bf[pl.ds(r0, rows), :D_EXPERT] = wg_ref[0, pl.ds(r0, rows), :].astype(BF16)
                wgu_bf[pl.ds(r0, rows), D_EXPERT:] = wu_ref[0, pl.ds(r0, rows), :].astype(BF16)
                return c
            lax.fori_loop(0, 2 * d_half // rows, cast_rows, 0)
            wd_bf[...] = wd_ref[0].astype(BF16)
            cached_ref[0] = expert

        words = _load_tile_rows(xbuf, (slot,), blk)
        xlo = jnp.concatenate([_unpack_lo(w).astype(BF16) for w in words], axis=1)
        xhi = jnp.concatenate([_unpack_hi(w).astype(BF16) for w in words], axis=1)
        gu = (jnp.dot(xlo, wgu_bf[:d_half, :], preferred_element_type=F32)
              + jnp.dot(xhi, wgu_bf[d_half:, :], preferred_element_type=F32))
        gate = gu[:, :D_EXPERT]
        hid = (_silu(gate) * gu[:, D_EXPERT:]).astype(BF16)
        y = _pack_bf16_pairs(jnp.dot(hid, wd_bf[...], preferred_element_type=F32))

        @pl.when(lo == 0)
        def _():
            _store_tile_rows(ybuf, (slot,), y)

        @pl.when(lo > 0)
        def _():
            row = lax.broadcasted_iota(I32, (blk, 1), 0)
            old = jnp.concatenate(_load_tile_rows(ybuf, (slot,), blk), axis=1)
            _store_tile_rows(ybuf, (slot,), jnp.where((row >= lo) & (row < hi), y, old))

        @pl.when(hi == blk)
        def _():
            @pl.when(s > 0)
            def _():
                wait_scatter(other)
            start_scatter(slot)

            @pl.when(s == n_blocks - 1)
            def _():
                wait_scatter(slot)


def _moe(visits, row_tok, row_slot, hn_tiles, w_gate, w_up, w_down):
    blk = MOE_BLOCK
    n_assign = row_tok.shape[0]
    n_blocks = n_assign // blk
    n_visits = visits[0].shape[0]
    d = w_gate.shape[1]
    smem_rows = pl.BlockSpec((1, 1, blk), lambda v, vb, ve, vl, vh: (vb[v], 0, 0), memory_space=pltpu.SMEM)
    smem_next = pl.BlockSpec((1, 1, blk), lambda v, vb, ve, vl, vh: (jnp.minimum(vb[v] + 1, n_blocks - 1), 0, 0),
                             memory_space=pltpu.SMEM)
    by_expert = lambda shape: pl.BlockSpec((1,) + shape, lambda v, vb, ve, vl, vh: (ve[v], 0, 0))
    grid_spec = pltpu.PrefetchScalarGridSpec(
        num_scalar_prefetch=4,
        grid=(n_visits,),
        in_specs=[smem_rows, smem_next, smem_rows,
                  pl.BlockSpec(memory_space=pl.ANY),
                  by_expert((d, D_EXPERT)), by_expert((d, D_EXPERT)), by_expert((D_EXPERT, d))],
        out_specs=pl.BlockSpec(memory_space=pl.ANY),
        scratch_shapes=[pltpu.VMEM((2, blk * ROW_TILE, LANES), U32), pltpu.VMEM((2, blk * ROW_TILE, LANES), U32),
                        pltpu.VMEM((d, 2 * D_EXPERT), BF16), pltpu.VMEM((D_EXPERT, d), BF16),
                        pltpu.SMEM((1,), I32),
                        pltpu.SemaphoreType.DMA((2,)), pltpu.SemaphoreType.DMA((2,))],
    )
    tok3 = row_tok.reshape(n_blocks, 1, blk)
    return pl.pallas_call(
        _moe_body,
        grid_spec=grid_spec,
        out_shape=jax.ShapeDtypeStruct((n_assign * ROW_TILE, LANES), U32),
        compiler_params=_cparams(1),
        name="moe_experts",
    )(*visits, tok3, tok3, row_slot.reshape(n_blocks, 1, blk), hn_tiles, w_gate, w_up, w_down)


def _route_tables(route, t):
    blk = MOE_BLOCK
    n_assign = t * TOP_K
    n_blocks = n_assign // blk
    flat_e = route[:, :TOP_K].astype(I32).T.reshape(n_assign)
    order = jnp.argsort(flat_e, stable=True).astype(I32)
    counts = jnp.sum((flat_e[:, None] == jnp.arange(N_EXPERTS, dtype=I32)[None, :]).astype(I32), axis=0)
    ends = jnp.cumsum(counts)
    starts = ends - counts
    cuts = jnp.sort(jnp.concatenate([jnp.arange(n_blocks, dtype=I32) * blk, starts[1:]]))
    nxt = jnp.concatenate([cuts[1:], jnp.full((1,), n_assign, I32)])
    vblk = jnp.minimum(cuts // blk, n_blocks - 1)
    vexp = jnp.minimum(jnp.sum((ends[None, :] <= cuts[:, None]).astype(I32), axis=1), N_EXPERTS - 1)
    vlo = cuts - vblk * blk
    vhi = jnp.maximum(jnp.minimum(nxt, (vblk + 1) * blk) - vblk * blk, vlo)
    return (vblk, vexp, vlo, vhi), order % t, order


def _combine_body(x2_ref, y0_ref, y1_ref, rt_ref, o_ref):
    tm = o_ref.shape[0]
    half = o_ref.shape[1] // 2
    g0 = rt_ref[:, TOP_K:TOP_K + 1]
    g1 = rt_ref[:, TOP_K + 1:TOP_K + 2]
    y0 = _load_tile_rows(y0_ref, (), tm)
    y1 = _load_tile_rows(y1_ref, (), tm)
    for c in range(ROW_TILE):
        lo_cols = slice(c * LANES, (c + 1) * LANES)
        hi_cols = slice(half + c * LANES, half + (c + 1) * LANES)
        o_ref[:, lo_cols] = x2_ref[:, lo_cols] + g0 * _unpack_lo(y0[c]) + g1 * _unpack_lo(y1[c])
        o_ref[:, hi_cols] = x2_ref[:, hi_cols] + g0 * _unpack_hi(y0[c]) + g1 * _unpack_hi(y1[c])


def _combine(x2, y_slots, route):
    t, d = x2.shape
    tm = min(512, t)
    nb = t // tm
    return pl.pallas_call(
        _combine_body,
        grid=(nb,),
        in_specs=[pl.BlockSpec((tm, d), lambda i: (i, 0)),
                  pl.BlockSpec((tm * ROW_TILE, LANES), lambda i: (i, 0)),
                  pl.BlockSpec((tm * ROW_TILE, LANES), lambda i: (i + nb, 0)),
                  pl.BlockSpec((tm, LANES), lambda i: (i, 0))],
        out_specs=pl.BlockSpec((tm, d), lambda i: (i, 0)),
        out_shape=jax.ShapeDtypeStruct((t, d), F32),
        compiler_params=_cparams(1),
        name="moe_combine",
    )(x2, y_slots, y_slots, route)


def _prepare(norm1_g, w_in, q_norm_g, k_norm_g, rpb, attn_out_g, conv_w, conv_b, a_log_f, a_log_b,
             dt_bias_f, dt_bias_b, d_skip, ssm_norm_g, w_out, norm2_g, w_router_group, b_router_group,
             w_router_expert, b_router_expert, w_gate, w_up, w_down):
    d = w_in.shape[0]
    hpg = HEADS_PER_GROUP
    p = {}
    p["norm1_g"] = norm1_g.reshape(1, d).astype(F32)
    p["w_main"] = w_in[:, :D_PROJ_MAIN].astype(BF16)
    w_dt = w_in[:, D_PROJ_MAIN:]
    zeros = jnp.zeros((d, LANES - 2 * hpg), w_in.dtype)
    per_group = lambda v, g: v[..., g * hpg:(g + 1) * hpg]
    p["w_dt"] = jnp.concatenate(
        [jnp.concatenate([per_group(w_dt[:, :N_HEADS_SSM], g), per_group(w_dt[:, N_HEADS_SSM:], g), zeros], axis=1)
         for g in range(N_GROUPS_SSM)], axis=1).astype(BF16)
    lane_rows = lambda f, bwd: jnp.stack(
        [jnp.concatenate([per_group(f, g), per_group(bwd, g), jnp.zeros((LANES - 2 * hpg,), F32)])
         for g in range(N_GROUPS_SSM)])[:, None, :]
    p["dtb"] = lane_rows(dt_bias_f.astype(F32), dt_bias_b.astype(F32))
    p["arow"] = lane_rows(-jnp.exp(a_log_f.astype(F32)), -jnp.exp(a_log_b.astype(F32)))
    scale = HEAD_DIM ** -0.5
    p["qg2"] = (jnp.tile(q_norm_g.astype(F32), 2) * scale).reshape(1, LANES)
    p["kg2"] = jnp.tile(k_norm_g.astype(F32), 2).reshape(1, LANES)
    lane = jnp.arange(LANES)
    p["e_mat"] = (lane[:, None] // HEAD_DIM == lane[None, :] // HEAD_DIM).astype(BF16)
    p["bias"] = _attention_bias(rpb)
    p["attn_out_g"] = attn_out_g.reshape(1, D_ATTN).astype(F32)
    cw = conv_w.reshape(D_CONV, -1).astype(F32)
    cbias = conv_b.reshape(1, -1).astype(F32)
    group_cols = lambda a, g: jnp.concatenate(
        [a[:, g * D_GROUP:(g + 1) * D_GROUP],
         a[:, D_SSM + g * D_STATE:D_SSM + (g + 1) * D_STATE],
         a[:, D_SSM + D_BC + g * D_STATE:D_SSM + D_BC + (g + 1) * D_STATE]], axis=1)
    p["cw"] = jnp.stack([group_cols(cw, g) for g in range(N_GROUPS_SSM)])
    p["cb"] = jnp.stack([group_cols(cbias, g) for g in range(N_GROUPS_SSM)])
    p["dsk"] = jnp.repeat(d_skip.astype(F32), SSM_HEAD_DIM).reshape(N_GROUPS_SSM, 1, D_GROUP)
    p["ng"] = ssm_norm_g.astype(F32).reshape(N_GROUPS_SSM, 1, D_GROUP)
    col_head = jnp.arange(D_GROUP) // SSM_HEAD_DIM
    p["ef"] = (lane[:, None] == col_head[None, :]).astype(BF16)[None]
    p["eb"] = (lane[:, None] == col_head[None, :] + hpg).astype(BF16)[None]
    pos = jnp.arange(CHUNK)
    p["tri"] = jnp.concatenate([pos[:, None] >= pos[None, :], pos[:, None] <= pos[None, :]], axis=0).astype(BF16)
    src = jnp.arange(CHUNK + 2 * CONV_HALO)
    p["shift"] = jnp.concatenate(
        [src[None, :] == pos[:, None] + (CONV_HALO - D_CONV // 2 + k) for k in range(D_CONV) if k != D_CONV // 2],
        axis=0).astype(BF16)
    p["wa"] = w_out[:D_ATTN].astype(BF16)
    p["ws"] = w_out[D_ATTN:].astype(BF16)
    p["norm2_g"] = norm2_g.reshape(1, d).astype(F32)
    n_r = N_EXPERT_GROUPS + N_EXPERTS
    wr = jnp.concatenate([w_router_group, w_router_expert, jnp.zeros((d, LANES - n_r), F32)], axis=1).astype(F32)
    wr_hi = wr.astype(BF16)
    wr_lo = (wr - wr_hi.astype(F32)).astype(BF16)
    p["wr"] = jnp.concatenate([wr_hi, wr_lo], axis=1)
    p["br"] = jnp.concatenate([b_router_group, b_router_expert, jnp.zeros((LANES - n_r,), F32)]).reshape(1, LANES)
    p["w_gate"], p["w_up"], p["w_down"] = w_gate, w_up, w_down
    return p


def _layer(x, p):
    b, l, d = x.shape
    t = b * l
    x2d = x.reshape(t, d)
    proj, dt = _inproj(x2d, p["norm1_g"], p["w_main"], p["w_dt"])
    proj3 = proj.reshape(b, l, D_PROJ_MAIN)
    attn = _attention(proj3, p["qg2"], p["kg2"], p["e_mat"], p["bias"])
    ssm = _ssd(proj3, dt.reshape(b, l, N_GROUPS_SSM * LANES), p["cw"], p["cb"], p["dtb"], p["arow"],
               p["dsk"], p["ng"], p["ef"], p["eb"], p["tri"], p["shift"])
    x2, hn_packed, route = _outproj(x2d, attn.reshape(t, D_ATTN), ssm.reshape(t, D_SSM), p["attn_out_g"],
                                    p["wa"], p["ws"], p["norm2_g"], p["wr"], p["br"])
    visits, row_tok, row_slot = _route_tables(route, t)
    y_slots = _moe(visits, row_tok, row_slot, hn_packed, p["w_gate"], p["w_up"], p["w_down"])
    return _combine(x2, y_slots, route).reshape(b, l, d)


def kernel(x_prompt, x_sample, norm1_g, w_in, q_norm_g, k_norm_g, rpb, attn_out_g, conv_w, conv_b, a_log_f,
           a_log_b, dt_bias_f, dt_bias_b, d_skip, ssm_norm_g, w_out, norm2_g, w_router_group, b_router_group,
           w_router_expert, b_router_expert, w_gate, w_up, w_down):
    weights = (norm1_g, w_in, q_norm_g, k_norm_g, rpb, attn_out_g, conv_w, conv_b, a_log_f, a_log_b, dt_bias_f,
               dt_bias_b, d_skip, ssm_norm_g, w_out, norm2_g, w_router_group, b_router_group, w_router_expert,
               b_router_expert, w_gate, w_up, w_down)
    assert all(w.shape[0] == 1 for w in weights), "one layer of stacked weights expected"
    p = _prepare(*(w[0] for w in weights))
    return (_layer(x_prompt, p), _layer(x_sample, p))
```

```python
import jax
import jax.numpy as jnp
from jax import lax
from jax.experimental import pallas as pl
from jax.experimental.pallas import tpu as pltpu

F32 = jnp.float32
BF16 = jnp.bfloat16
U32 = jnp.uint32
I32 = jnp.int32

EPS = 1e-6
GRID_W = 64
N_HEADS_ATTN = 16
HEAD_DIM = 64
D_ATTN = N_HEADS_ATTN * HEAD_DIM
WIN_H = 8
WIN_W = 16
N_HEADS_SSM = 16
SSM_HEAD_DIM = 64
D_SSM = N_HEADS_SSM * SSM_HEAD_DIM
N_GROUPS_SSM = 2
HEADS_PER_GROUP = N_HEADS_SSM // N_GROUPS_SSM
D_GROUP = D_SSM // N_GROUPS_SSM
D_STATE = 128
D_CONV = 5
CHUNK = 128
D_BC = N_GROUPS_SSM * D_STATE
N_EXPERT_GROUPS = 4
EXPERTS_PER_GROUP = 8
N_EXPERTS = N_EXPERT_GROUPS * EXPERTS_PER_GROUP
TOP_K = 2
D_EXPERT = 512

LANES = 128
BF16_ROWS = 16
CONV_HALO = BF16_ROWS
NEG = -1e30
VMEM_LIMIT_BYTES = 56 * 1024 * 1024
MOE_BLOCK = 256
ATTN_ROW_UNROLL = 8
SSD_CHUNK_UNROLL = 4
MOE_DMA_UNROLL = 8
MOE_SPARE_SLOT = 2
MOE_COL_CHUNKS = 4
ROW_TILE = 8
OUTPROJ_SUB_ROWS = 256

COL_Q, COL_K, COL_V = 0, D_ATTN // LANES, 2 * D_ATTN // LANES
COL_Z = 3 * D_ATTN // D_GROUP
COL_XS = (3 * D_ATTN + D_SSM) // D_GROUP
COL_B = (3 * D_ATTN + 2 * D_SSM) // LANES
COL_C = COL_B + D_BC // LANES
D_PROJ_MAIN = 3 * D_ATTN + 2 * D_SSM + 2 * D_BC


def _cparams(n_axes):
    return pltpu.CompilerParams(dimension_semantics=("arbitrary",) * n_axes,
                                vmem_limit_bytes=VMEM_LIMIT_BYTES)


def _silu(x):
    return x * (0.5 * jnp.tanh(0.5 * x) + 0.5)


def _split3(x):
    hi = x.astype(BF16)
    r1 = x - hi.astype(F32)
    mid = r1.astype(BF16)
    lo = (r1 - mid.astype(F32)).astype(BF16)
    return hi, mid, lo


def _pack_bf16_pairs(x):
    n = x.shape[1] // 2
    u = lax.bitcast_convert_type(x.astype(BF16).astype(F32), U32)
    return (u[:, :n] >> 16) | u[:, n:]


def _unpack_lo(u):
    return lax.bitcast_convert_type(u << 16, F32)


def _unpack_hi(u):
    return lax.bitcast_convert_type(u & jnp.uint32(0xFFFF0000), F32)


def _store_tile_rows(ref, lead, packed):
    m = packed.shape[0]
    for c in range(ROW_TILE):
        ref[lead + (pl.ds(c, m, stride=ROW_TILE), slice(None))] = packed[:, c * LANES:(c + 1) * LANES]


def _load_tile_rows(ref, lead, m):
    return [ref[lead + (pl.ds(c, m, stride=ROW_TILE), slice(None))] for c in range(ROW_TILE)]


def _inproj_body(x_ref, g_ref, w_ref, wdt_ref, o_ref, dt_ref, hn_ref):
    tm = x_ref.shape[0]
    rows = min(tm, 256)

    @pl.when(pl.program_id(1) == 0)
    def _():
        def norm_rows(i, c):
            r0 = pl.multiple_of(i * rows, rows)
            x = x_ref[pl.ds(r0, rows), :]
            ms = jnp.mean(x * x, axis=-1, keepdims=True)
            hn_ref[pl.ds(r0, rows), :] = (x * lax.rsqrt(ms + EPS) * g_ref[...]).astype(BF16)
            return c
        lax.fori_loop(0, tm // rows, norm_rows, 0)
        dt_ref[...] = jnp.dot(hn_ref[...], wdt_ref[...], preferred_element_type=F32)

    o_ref[...] = jnp.dot(hn_ref[...], w_ref[...], preferred_element_type=F32).astype(BF16)


def _inproj(x2d, gain, w_main, w_dt):
    t, d = x2d.shape
    n = w_main.shape[1]
    ndt = w_dt.shape[1]
    tm = min(1024, t)
    tn = 512
    return pl.pallas_call(
        _inproj_body,
        grid=(t // tm, n // tn),
        in_specs=[pl.BlockSpec((tm, d), lambda i, j: (i, 0)),
                  pl.BlockSpec((1, d), lambda i, j: (0, 0)),
                  pl.BlockSpec((d, tn), lambda i, j: (0, j)),
                  pl.BlockSpec((d, ndt), lambda i, j: (0, 0))],
        out_specs=[pl.BlockSpec((tm, tn), lambda i, j: (i, j)),
                   pl.BlockSpec((tm, ndt), lambda i, j: (i, 0))],
        out_shape=[jax.ShapeDtypeStruct((t, n), BF16), jax.ShapeDtypeStruct((t, ndt), F32)],
        scratch_shapes=[pltpu.VMEM((tm, d), BF16)],
        compiler_params=_cparams(2),
        name="inproj",
    )(x2d, gain, w_main, w_dt)


def _attn_body(q_ref, k_ref, v_ref, qg_ref, kg_ref, e_ref, bias_ref, o_ref, qs, ks, s_scr, m_scr):
    l = q_ref.shape[1]
    n_rows = l // GRID_W
    win_keys = WIN_H * GRID_W
    ch = min(l, 512)
    head_a = lax.broadcasted_iota(I32, (1, LANES), 1) < HEAD_DIM
    sel_a = jnp.where(head_a, 1.0, 0.0).astype(BF16)
    sel_b = jnp.where(head_a, 0.0, 1.0).astype(BF16)

    def norm_rows(i, c):
        r0 = pl.multiple_of(i * ch, ch)
        for src, gref, dst in ((q_ref, qg_ref, qs), (k_ref, kg_ref, ks)):
            x = src[0, pl.ds(r0, ch), :].astype(F32)
            ssq = jnp.dot((x * x).astype(BF16), e_ref[...], preferred_element_type=F32)
            dst[pl.ds(r0, ch), :] = (x * lax.rsqrt(ssq * (1.0 / HEAD_DIM) + EPS) * gref[...]).astype(BF16)
        return c
    lax.fori_loop(0, l // ch, norm_rows, 0)

    def key_start(r):
        return jnp.clip(r - WIN_H // 2, 0, n_rows - WIN_H)

    def score_stage(r, slot):
        rs = key_start(r)
        q_r = qs[pl.ds(pl.multiple_of(r * GRID_W, GRID_W), GRID_W), :]
        qm = jnp.concatenate([q_r * sel_a, q_r * sel_b], axis=0)
        kb = ks[pl.ds(pl.multiple_of(rs * GRID_W, GRID_W), win_keys), :]
        s = lax.dot_general(qm, kb, (((1,), (1,)), ((), ())), preferred_element_type=F32)
        dr0 = rs - r + (WIN_H - 1)
        lane0 = pl.multiple_of((dr0 // 2) * LANES, LANES)
        s = s + bias_ref[0, dr0 % 2, :, pl.ds(lane0, win_keys)]
        s_scr[slot] = s
        m_scr[slot] = jnp.max(s, axis=-1, keepdims=True)

    def value_stage(r, slot):
        rs = key_start(r)
        vb = v_ref[0, pl.ds(pl.multiple_of(rs * GRID_W, GRID_W), win_keys), :]
        p = jnp.exp(s_scr[slot] - m_scr[slot])
        den = jnp.sum(p, axis=-1, keepdims=True)
        o = jnp.dot(p.astype(BF16), vb, preferred_element_type=F32) * (1.0 / den)
        out = jnp.where(head_a, o[:GRID_W], o[GRID_W:])
        o_ref[0, pl.ds(pl.multiple_of(r * GRID_W, GRID_W), GRID_W), :] = out.astype(BF16)

    score_stage(0, 0)

    def row_group(i, c):
        r = ATTN_ROW_UNROLL * i
        for j in range(ATTN_ROW_UNROLL):
            score_stage(jnp.minimum(r + j + 1, n_rows - 1), (j + 1) % 2)
            value_stage(r + j, j % 2)
        return c
    lax.fori_loop(0, n_rows // ATTN_ROW_UNROLL, row_group, 0)


def _attention(proj3, qg2, kg2, e_mat, bias):
    b, l, _ = proj3.shape
    n_pairs = N_HEADS_ATTN // 2
    blk = (1, l, LANES)
    return pl.pallas_call(
        _attn_body,
        grid=(n_pairs, b),
        in_specs=[pl.BlockSpec(blk, lambda hp, bi: (bi, 0, COL_Q + hp)),
                  pl.BlockSpec(blk, lambda hp, bi: (bi, 0, COL_K + hp)),
                  pl.BlockSpec(blk, lambda hp, bi: (bi, 0, COL_V + hp)),
                  pl.BlockSpec((1, LANES), lambda hp, bi: (0, 0)),
                  pl.BlockSpec((1, LANES), lambda hp, bi: (0, 0)),
                  pl.BlockSpec((LANES, LANES), lambda hp, bi: (0, 0)),
                  pl.BlockSpec((1,) + bias.shape[1:], lambda hp, bi: (hp, 0, 0, 0))],
        out_specs=pl.BlockSpec(blk, lambda hp, bi: (bi, 0, hp)),
        out_shape=jax.ShapeDtypeStruct((b, l, D_ATTN), BF16),
        scratch_shapes=[pltpu.VMEM((l, LANES), BF16), pltpu.VMEM((l, LANES), BF16),
                        pltpu.VMEM((2, LANES, WIN_H * GRID_W), F32), pltpu.VMEM((2, LANES, 1), F32)],
        compiler_params=_cparams(2),
        name="nbr_attention",
    )(proj3, proj3, proj3, qg2, kg2, e_mat, bias)


def _attention_bias(rpb):
    n_dr = 2 * WIN_H - 1
    cols = jnp.arange(GRID_W, dtype=I32)
    col_start = jnp.clip(cols - WIN_W // 2, 0, GRID_W - WIN_W)
    keys = cols[None, :]
    valid = (keys >= col_start[:, None]) & (keys < col_start[:, None] + WIN_W)
    dc = jnp.clip(keys - cols[:, None] + (WIN_W - 1), 0, 2 * WIN_W - 2)
    tab = jnp.where(valid[None, None], rpb.astype(F32)[:, :, dc], NEG)
    flat = jnp.transpose(tab, (0, 2, 1, 3)).reshape(N_HEADS_ATTN // 2, 2 * GRID_W, n_dr * GRID_W)
    moved = jnp.concatenate([flat[:, :, GRID_W:], jnp.full(flat.shape[:2] + (GRID_W,), NEG, F32)], axis=2)
    return jnp.stack([flat, moved], axis=1)


def _ssd_body(z_ref, xs_ref, b_ref, c_ref, dt_ref, cw_ref, cb_ref, dtb_ref, arow_ref, dsk_ref, ng_ref,
              ef_ref, eb_ref, tri_ref, shift_ref, o_ref, xs_s, b_s, c_s, y_s, sf_s, sb_s, x_f, pre_s):
    l = xs_ref.shape[1]
    n_chunks = l // CHUNK
    hpg = HEADS_PER_GROUP
    ii = lax.broadcasted_iota(I32, (CHUNK, CHUNK), 0)
    jj = lax.broadcasted_iota(I32, (CHUNK, CHUNK), 1)
    causal = ii >= jj
    anti = ii <= jj
    head_a = lax.broadcasted_iota(I32, (1, LANES), 1) < SSM_HEAD_DIM

    def conv_chunk(r0):
        p0 = pl.multiple_of(jnp.maximum(r0 - CONV_HALO, 0), CONV_HALO)
        n0 = pl.multiple_of(jnp.minimum(r0 + CHUNK, l - CONV_HALO), CONV_HALO)
        has_prev = r0 > 0
        has_next = r0 + CHUNK < l
        parts = []
        for ref in (xs_ref, b_ref, c_ref):
            prev = ref[0, pl.ds(p0, CONV_HALO), :]
            nxt = ref[0, pl.ds(n0, CONV_HALO), :]
            parts.append(jnp.concatenate([jnp.where(has_prev, prev, jnp.zeros_like(prev)),
                                          ref[0, pl.ds(r0, CHUNK), :],
                                          jnp.where(has_next, nxt, jnp.zeros_like(nxt))], axis=0))
        ext = jnp.concatenate(parts, axis=1)
        shifted = jnp.dot(shift_ref[...], ext, preferred_element_type=F32)
        mid = D_CONV // 2
        acc = cb_ref[0] + ext[CONV_HALO:CONV_HALO + CHUNK].astype(F32) * cw_ref[0, mid:mid + 1, :]
        for n, k in enumerate(k for k in range(D_CONV) if k != mid):
            acc = acc + shifted[n * CHUNK:(n + 1) * CHUNK] * cw_ref[0, k:k + 1, :]
        return _silu(acc)

    def dt_terms(r0):
        raw = dt_ref[0, pl.ds(r0, CHUNK), :] + dtb_ref[0]
        dtv = jnp.maximum(raw, 0.0) + jnp.log(1.0 + jnp.exp(-jnp.abs(raw)))
        adt = dtv * arow_ref[0]
        hi, mid, lo = _split3(adt)
        cs3 = jnp.dot(tri_ref[...], jnp.concatenate([hi, mid, lo], axis=1), preferred_element_type=F32)
        cs = cs3[:, :LANES] + cs3[:, LANES:2 * LANES] + cs3[:, 2 * LANES:]
        return dtv, cs[:CHUNK], cs[CHUNK:]

    def expand_exact(row, e_ref):
        hi, mid, lo = _split3(jnp.broadcast_to(row, (8, LANES)))
        e = e_ref[0]
        r = (jnp.dot(hi, e, preferred_element_type=F32) + jnp.dot(mid, e, preferred_element_type=F32)
             + jnp.dot(lo, e, preferred_element_type=F32))
        return r[0:1]

    def state_terms(cc, bt, x, dtv, cs, tot, e_ref, s_ref):
        e = e_ref[0]
        expcs = jnp.dot(jnp.exp(cs).astype(BF16), e, preferred_element_type=F32)
        y_off = jnp.dot(cc, s_ref[...].astype(BF16), preferred_element_type=F32) * expcs
        scl = jnp.dot((dtv * jnp.exp(tot - cs)).astype(BF16), e, preferred_element_type=F32)
        states_t = jnp.dot(bt, (x * scl).astype(BF16), preferred_element_type=F32)
        s_ref[...] = s_ref[...] * expand_exact(jnp.exp(tot), e_ref) + states_t
        return y_off

    sf_s[...] = jnp.zeros_like(sf_s)
    sb_s[...] = jnp.zeros_like(sb_s)

    def prep_stage(c, slot):
        r0 = pl.multiple_of(c * CHUNK, CHUNK)
        xbc = conv_chunk(r0)
        x = xbc[:, :D_GROUP]
        bm = xbc[:, D_GROUP:D_GROUP + D_STATE]
        xs_s[pl.ds(r0, CHUNK), :] = x.astype(BF16)
        b_s[pl.ds(r0, CHUNK), :] = bm.astype(BF16)
        c_s[pl.ds(r0, CHUNK), :] = xbc[:, D_GROUP + D_STATE:].astype(BF16)
        x_f[slot] = x
        dtv, cs_f, cs_b = dt_terms(r0)
        for n, a in enumerate((dtv, cs_f, cs_b, cs_f.T, cs_b.T, dtv.T, bm.T)):
            pre_s[slot, n] = a

    def mix_stage(c, slot):
        r0 = pl.multiple_of(c * CHUNK, CHUNK)
        x = x_f[slot]
        xb = xs_s[pl.ds(r0, CHUNK), :]
        bb = b_s[pl.ds(r0, CHUNK), :]
        cc = c_s[pl.ds(r0, CHUNK), :]
        dtv, cs_f, cs_b, cst_f, cst_b, dtt, bmt = (pre_s[slot, n] for n in range(7))
        cb = lax.dot_general(cc, bb, (((1,), (1,)), ((), ())), preferred_element_type=F32)
        pieces = []
        for pair in range(hpg // 2):
            xp = xb[:, pair * LANES:(pair + 1) * LANES]
            ys = []
            for hh in range(2):
                h = 2 * pair + hh
                df = cs_f[:, h:h + 1] - cst_f[h:h + 1, :]
                db = cs_b[:, hpg + h:hpg + h + 1] - cst_b[hpg + h:hpg + h + 1, :]
                lf = jnp.exp(jnp.where(causal, df, NEG)) * dtt[h:h + 1, :]
                lb = jnp.exp(jnp.where(anti, db, NEG)) * dtt[hpg + h:hpg + h + 1, :]
                m = (cb * (lf + lb)).astype(BF16)
                ys.append(jnp.dot(m, xp, preferred_element_type=F32))
            pieces.append(jnp.where(head_a, ys[0], ys[1]))
        y = jnp.concatenate(pieces, axis=1) + x * dsk_ref[0]
        y = y + state_terms(cc, bmt.astype(BF16), x, dtv, cs_f, cs_f[CHUNK - 1:CHUNK, :], ef_ref, sf_s)
        y_s[pl.ds(r0, CHUNK), :] = y

    prep_stage(0, 0)

    def fwd_group(i, carry):
        c = SSD_CHUNK_UNROLL * i
        for j in range(SSD_CHUNK_UNROLL):
            prep_stage(jnp.minimum(c + j + 1, n_chunks - 1), (j + 1) % 2)
            mix_stage(c + j, j % 2)
        return carry
    lax.fori_loop(0, n_chunks // SSD_CHUNK_UNROLL, fwd_group, 0)

    def bwd_prep_stage(c, slot):
        r0 = pl.multiple_of(c * CHUNK, CHUNK)
        dtv, _, cs_b = dt_terms(r0)
        pre_s[slot, 0] = dtv
        pre_s[slot, 2] = cs_b
        pre_s[slot, 6] = b_s[pl.ds(r0, CHUNK), :].astype(F32).T

    def bwd_mix_stage(c, slot):
        r0 = pl.multiple_of(c * CHUNK, CHUNK)
        x = xs_s[pl.ds(r0, CHUNK), :].astype(F32)
        cc = c_s[pl.ds(r0, CHUNK), :]
        dtv, cs_b, bt = pre_s[slot, 0], pre_s[slot, 2], pre_s[slot, 6].astype(BF16)
        y = y_s[pl.ds(r0, CHUNK), :] + state_terms(cc, bt, x, dtv, cs_b, cs_b[0:1, :], eb_ref, sb_s)
        y = y * _silu(z_ref[0, pl.ds(r0, CHUNK), :].astype(F32))
        ms = jnp.mean(y * y, axis=-1, keepdims=True)
        o_ref[0, pl.ds(r0, CHUNK), :] = (y * lax.rsqrt(ms + EPS) * ng_ref[0]).astype(BF16)

    bwd_prep_stage(n_chunks - 1, 0)

    def bwd_group(i, carry):
        c = n_chunks - 1 - SSD_CHUNK_UNROLL * i
        for j in range(SSD_CHUNK_UNROLL):
            bwd_prep_stage(jnp.maximum(c - j - 1, 0), (j + 1) % 2)
            bwd_mix_stage(c - j, j % 2)
        return carry
    lax.fori_loop(0, n_chunks // SSD_CHUNK_UNROLL, bwd_group, 0)


def _ssd(proj3, dt3, cw, cb, dtb, arow, dsk, ng, ef, eb, tri, shift):
    b, l, _ = proj3.shape
    g = N_GROUPS_SSM
    wconv = D_GROUP + 2 * D_STATE
    per_group = lambda shape: pl.BlockSpec((1,) + shape, lambda bi, gi: (gi,) + (0,) * len(shape))
    return pl.pallas_call(
        _ssd_body,
        grid=(b, g),
        in_specs=[pl.BlockSpec((1, l, D_GROUP), lambda bi, gi: (bi, 0, COL_Z + gi)),
                  pl.BlockSpec((1, l, D_GROUP), lambda bi, gi: (bi, 0, COL_XS + gi)),
                  pl.BlockSpec((1, l, D_STATE), lambda bi, gi: (bi, 0, COL_B + gi)),
                  pl.BlockSpec((1, l, D_STATE), lambda bi, gi: (bi, 0, COL_C + gi)),
                  pl.BlockSpec((1, l, LANES), lambda bi, gi: (bi, 0, gi)),
                  per_group((D_CONV, wconv)), per_group((1, wconv)),
                  per_group((1, LANES)), per_group((1, LANES)),
                  per_group((1, D_GROUP)), per_group((1, D_GROUP)),
                  pl.BlockSpec((1, LANES, D_GROUP), lambda bi, gi: (0, 0, 0)),
                  pl.BlockSpec((1, LANES, D_GROUP), lambda bi, gi: (0, 0, 0)),
                  pl.BlockSpec((2 * CHUNK, CHUNK), lambda bi, gi: (0, 0)),
                  pl.BlockSpec(shift.shape, lambda bi, gi: (0, 0))],
        out_specs=pl.BlockSpec((1, l, D_GROUP), lambda bi, gi: (bi, 0, gi)),
        out_shape=jax.ShapeDtypeStruct((b, l, D_SSM), BF16),
        scratch_shapes=[pltpu.VMEM((l, D_GROUP), BF16), pltpu.VMEM((l, D_STATE), BF16),
                        pltpu.VMEM((l, D_STATE), BF16), pltpu.VMEM((l, D_GROUP), F32),
                        pltpu.VMEM((D_STATE, D_GROUP), F32), pltpu.VMEM((D_STATE, D_GROUP), F32),
                        pltpu.VMEM((2, CHUNK, D_GROUP), F32), pltpu.VMEM((2, 7, CHUNK, LANES), F32)],
        compiler_params=_cparams(2),
        name="ssd_mixer",
    )(proj3, proj3, proj3, proj3, dt3, cw, cb, dtb, arow, dsk, ng, ef, eb, tri, shift)


def _route_rows(lg):
    lane = lax.broadcasted_iota(I32, lg.shape, 1).astype(F32)
    n_g = float(N_EXPERT_GROUPS)
    n_e = float(EXPERTS_PER_GROUP)
    gl = jnp.where(lane < n_g, lg, NEG)
    gmax = jnp.max(gl, axis=-1, keepdims=True)
    g_idx = jnp.min(jnp.where(gl == gmax, lane, float(LANES)), axis=-1, keepdims=True)
    g_p = 1.0 / jnp.sum(jnp.exp(gl - gmax), axis=-1, keepdims=True)
    first = n_g + n_e * g_idx
    el = jnp.where((lane >= first) & (lane < first + n_e), lg, NEG)
    e1 = jnp.max(el, axis=-1, keepdims=True)
    i1 = jnp.min(jnp.where(el == e1, lane, float(LANES)), axis=-1, keepdims=True)
    el2 = jnp.where(lane == i1, NEG, el)
    e2 = jnp.max(el2, axis=-1, keepdims=True)
    i2 = jnp.min(jnp.where(el2 == e2, lane, float(LANES)), axis=-1, keepdims=True)
    r = jnp.exp(e2 - e1)
    gate1 = g_p / (1.0 + r)
    gate2 = gate1 * r
    return jnp.where(lane == 0.0, i1 - n_g,
                     jnp.where(lane == 1.0, i2 - n_g,
                               jnp.where(lane == 2.0, gate1, jnp.where(lane == 3.0, gate2, 0.0))))


def _outproj_body(x_ref, a_ref, s_ref, ag_ref, wa_ref, ws_ref, g2_ref, wr_ref, br_ref, x2_ref, hp_ref, rt_ref):
    tm = x_ref.shape[0]
    sub = min(tm, OUTPROJ_SUB_ROWS)
    for r0 in range(0, tm, sub):
        rows = slice(r0, r0 + sub)
        a = a_ref[rows, :].astype(F32)
        ms = jnp.mean(a * a, axis=-1, keepdims=True)
        an = (a * lax.rsqrt(ms + EPS) * ag_ref[...]).astype(BF16)
        y = (jnp.dot(an, wa_ref[...], preferred_element_type=F32)
             + jnp.dot(s_ref[rows, :], ws_ref[...], preferred_element_type=F32))
        x2 = x_ref[rows, :] + y
        x2_ref[rows, :] = x2
        ms2 = jnp.mean(x2 * x2, axis=-1, keepdims=True)
        hn = x2 * lax.rsqrt(ms2 + EPS) * g2_ref[...]
        hi = hn.astype(BF16)
        lo = (hn - hi.astype(F32)).astype(BF16)
        l1 = jnp.dot(hi, wr_ref[...], preferred_element_type=F32)
        l2 = jnp.dot(lo, wr_ref[:, :LANES], preferred_element_type=F32)
        rt_ref[rows, :] = _route_rows(l1[:, :LANES] + l1[:, LANES:] + l2 + br_ref[...])
        packed = _pack_bf16_pairs(hi.astype(F32))
        for c in range(ROW_TILE):
            hp_ref[pl.ds(r0 * ROW_TILE + c, sub, stride=ROW_TILE), :] = packed[:, c * LANES:(c + 1) * LANES]


def _outproj(x2d, attn2d, ssm2d, ag, wa, ws, g2, wr, br):
    t, d = x2d.shape
    tm = min(2 * OUTPROJ_SUB_ROWS, t)
    row = lambda w: pl.BlockSpec((tm, w), lambda i: (i, 0))
    full = lambda a: pl.BlockSpec(a.shape, lambda i: (0,) * a.ndim)
    return pl.pallas_call(
        _outproj_body,
        grid=(t // tm,),
        in_specs=[row(d), row(D_ATTN), row(D_SSM), full(ag), full(wa), full(ws), full(g2), full(wr), full(br)],
        out_specs=[row(d), pl.BlockSpec((tm * ROW_TILE, LANES), lambda i: (i, 0)), row(LANES)],
        out_shape=[jax.ShapeDtypeStruct((t, d), F32), jax.ShapeDtypeStruct((t * ROW_TILE, LANES), U32),
                   jax.ShapeDtypeStruct((t, LANES), F32)],
        compiler_params=_cparams(1),
        name="outproj_router",
    )(x2d, attn2d, ssm2d, ag, wa, ws, g2, wr, br)


def _moe_body(vblk_ref, vexp_ref, vlo_ref, vhi_ref, tok_ref, tok_next_ref, slot_ref, x_hbm, wg_ref, wu_ref, wd_ref,
              y_hbm, xbuf, ybuf, wgu_bf, wd_bf, cached_ref, gsem, ssem):
    blk = xbuf.shape[1] // ROW_TILE
    d_half = wg_ref.shape[1] // 2
    v = pl.program_id(0)
    lo = vlo_ref[v]
    hi = vhi_ref[v]
    s = vblk_ref[v]
    expert = vexp_ref[v]
    n_blocks = y_hbm.shape[0] // (blk * ROW_TILE)
    slot = s % 2
    other = 1 - slot

    def tile_rows(i):
        return pl.ds(pl.multiple_of(i * ROW_TILE, ROW_TILE), ROW_TILE)

    def start_gather(table_ref, buf_slot):
        def body(i, c):
            pltpu.make_async_copy(x_hbm.at[tile_rows(table_ref[0, 0, i])], xbuf.at[buf_slot, tile_rows(i)],
                                  gsem.at[buf_slot]).start()
            return c
        lax.fori_loop(0, blk, body, 0, unroll=MOE_DMA_UNROLL)

    def wait_gather(buf_slot):
        pltpu.make_async_copy(x_hbm.at[pl.ds(0, blk * ROW_TILE)], xbuf.at[buf_slot], gsem.at[buf_slot]).wait()

    def start_scatter(buf_slot):
        def body(i, c):
            pltpu.make_async_copy(ybuf.at[buf_slot, tile_rows(i)], y_hbm.at[tile_rows(slot_ref[0, 0, i])],
                                  ssem.at[buf_slot]).start()
            return c
        lax.fori_loop(0, blk, body, 0, unroll=MOE_DMA_UNROLL)

    def wait_scatter(buf_slot):
        pltpu.make_async_copy(ybuf.at[buf_slot], y_hbm.at[pl.ds(0, blk * ROW_TILE)], ssem.at[buf_slot]).wait()

    @pl.when(v == 0)
    def _():
        cached_ref[0] = -1

    @pl.when(hi > lo)
    def _():
        @pl.when(lo == 0)
        def _():
            @pl.when(s == 0)
            def _():
                start_gather(tok_ref, slot)
            wait_gather(slot)

        @pl.when(cached_ref[0] != expert)
        def _():
            rows = 256

            def cast_rows(i, c):
                r0 = pl.multiple_of(i * rows, rows)
                wgu_bf[pl.ds(r0, rows), :D_EXPERT] = wg_ref[0, pl.ds(r0, rows), :].astype(BF16)
                wgu_bf[pl.ds(r0, rows), D_EXPERT:] = wu_ref[0, pl.ds(r0, rows), :].astype(BF16)
                return c
            lax.fori_loop(0, 2 * d_half // rows, cast_rows, 0)
            wd_bf[...] = wd_ref[0].astype(BF16)
            cached_ref[0] = expert

        prefetch = (lo == 0) & (s + 1 < n_blocks)
        dst = jnp.where(prefetch, other, MOE_SPARE_SLOT)
        def issue_rows(first, count):
            for i in range(first, first + count):
                pltpu.make_async_copy(x_hbm.at[tile_rows(tok_next_ref[0, 0, i])],
                                      xbuf.at[dst, pl.ds(i * ROW_TILE, ROW_TILE)], gsem.at[dst]).start()

        words = _load_tile_rows(xbuf, (slot,), blk)
        xlo = jnp.concatenate([_unpack_lo(w).astype(BF16) for w in words], axis=1)
        xhi = jnp.concatenate([_unpack_hi(w).astype(BF16) for w in words], axis=1)
        quarter = blk // 4
        gate_up = []
        for j in range(2):
            issue_rows(j * quarter, quarter)
            cols = slice(j * D_EXPERT, (j + 1) * D_EXPERT)
            gate_up.append(jnp.dot(xlo, wgu_bf[:d_half, cols], preferred_element_type=F32)
                           + jnp.dot(xhi, wgu_bf[d_half:, cols], preferred_element_type=F32))
        hid = (_silu(gate_up[0]) * gate_up[1]).astype(BF16)
        ow = wd_bf.shape[1] // MOE_COL_CHUNKS
        per_chunk = 2 * quarter // MOE_COL_CHUNKS
        outs = []
        for j in range(MOE_COL_CHUNKS):
            issue_rows(2 * quarter + j * per_chunk, per_chunk)
            outs.append(jnp.dot(hid, wd_bf[:, j * ow:(j + 1) * ow], preferred_element_type=F32))
        y = _pack_bf16_pairs(jnp.concatenate(outs, axis=1))

        @pl.when(jnp.logical_not(prefetch))
        def _():
            wait_gather(MOE_SPARE_SLOT)

        @pl.when(lo == 0)
        def _():
            _store_tile_rows(ybuf, (slot,), y)

        @pl.when(lo > 0)
        def _():
            row = lax.broadcasted_iota(I32, (blk, 1), 0)
            old = jnp.concatenate(_load_tile_rows(ybuf, (slot,), blk), axis=1)
            _store_tile_rows(ybuf, (slot,), jnp.where((row >= lo) & (row < hi), y, old))

        @pl.when(hi == blk)
        def _():
            @pl.when(s > 0)
            def _():
                wait_scatter(other)
            start_scatter(slot)

            @pl.when(s == n_blocks - 1)
            def _():
                wait_scatter(slot)


def _moe(visits, row_tok, row_slot, hn_tiles, w_gate, w_up, w_down):
    blk = MOE_BLOCK
    n_assign = row_tok.shape[0]
    n_blocks = n_assign // blk
    n_visits = visits[0].shape[0]
    d = w_gate.shape[1]
    smem_rows = pl.BlockSpec((1, 1, blk), lambda v, vb, ve, vl, vh: (vb[v], 0, 0), memory_space=pltpu.SMEM)
    smem_next = pl.BlockSpec((1, 1, blk), lambda v, vb, ve, vl, vh: (jnp.minimum(vb[v] + 1, n_blocks - 1), 0, 0),
                             memory_space=pltpu.SMEM)
    by_expert = lambda shape: pl.BlockSpec((1,) + shape, lambda v, vb, ve, vl, vh: (ve[v], 0, 0))
    grid_spec = pltpu.PrefetchScalarGridSpec(
        num_scalar_prefetch=4,
        grid=(n_visits,),
        in_specs=[smem_rows, smem_next, smem_rows,
                  pl.BlockSpec(memory_space=pl.ANY),
                  by_expert((d, D_EXPERT)), by_expert((d, D_EXPERT)), by_expert((D_EXPERT, d))],
        out_specs=pl.BlockSpec(memory_space=pl.ANY),
        scratch_shapes=[pltpu.VMEM((MOE_SPARE_SLOT + 1, blk * ROW_TILE, LANES), U32),
                        pltpu.VMEM((2, blk * ROW_TILE, LANES), U32),
                        pltpu.VMEM((d, 2 * D_EXPERT), BF16), pltpu.VMEM((D_EXPERT, d), BF16),
                        pltpu.SMEM((1,), I32),
                        pltpu.SemaphoreType.DMA((MOE_SPARE_SLOT + 1,)), pltpu.SemaphoreType.DMA((2,))],
    )
    tok3 = row_tok.reshape(n_blocks, 1, blk)
    return pl.pallas_call(
        _moe_body,
        grid_spec=grid_spec,
        out_shape=jax.ShapeDtypeStruct((n_assign * ROW_TILE, LANES), U32),
        compiler_params=_cparams(1),
        name="moe_experts",
    )(*visits, tok3, tok3, row_slot.reshape(n_blocks, 1, blk), hn_tiles, w_gate, w_up, w_down)


def _route_tables(route, t):
    blk = MOE_BLOCK
    n_assign = t * TOP_K
    n_blocks = n_assign // blk
    flat_e = route[:, :TOP_K].astype(I32).T.reshape(n_assign)
    order = jnp.argsort(flat_e, stable=True).astype(I32)
    counts = jnp.sum((flat_e[:, None] == jnp.arange(N_EXPERTS, dtype=I32)[None, :]).astype(I32), axis=0)
    ends = jnp.cumsum(counts)
    starts = ends - counts
    cuts = jnp.sort(jnp.concatenate([jnp.arange(n_blocks, dtype=I32) * blk, starts[1:]]))
    nxt = jnp.concatenate([cuts[1:], jnp.full((1,), n_assign, I32)])
    vblk = jnp.minimum(cuts // blk, n_blocks - 1)
    vexp = jnp.minimum(jnp.sum((ends[None, :] <= cuts[:, None]).astype(I32), axis=1), N_EXPERTS - 1)
    vlo = cuts - vblk * blk
    vhi = jnp.maximum(jnp.minimum(nxt, (vblk + 1) * blk) - vblk * blk, vlo)
    return (vblk, vexp, vlo, vhi), order % t, order


def _combine_body(x2_ref, y0_ref, y1_ref, rt_ref, o_ref):
    tm = o_ref.shape[0]
    half = o_ref.shape[1] // 2
    g0 = rt_ref[:, TOP_K:TOP_K + 1]
    g1 = rt_ref[:, TOP_K + 1:TOP_K + 2]
    y0 = _load_tile_rows(y0_ref, (), tm)
    y1 = _load_tile_rows(y1_ref, (), tm)
    for c in range(ROW_TILE):
        lo_cols = slice(c * LANES, (c + 1) * LANES)
        hi_cols = slice(half + c * LANES, half + (c + 1) * LANES)
        o_ref[:, lo_cols] = x2_ref[:, lo_cols] + g0 * _unpack_lo(y0[c]) + g1 * _unpack_lo(y1[c])
        o_ref[:, hi_cols] = x2_ref[:, hi_cols] + g0 * _unpack_hi(y0[c]) + g1 * _unpack_hi(y1[c])


def _combine(x2, y_slots, route):
    t, d = x2.shape
    tm = min(512, t)
    nb = t // tm
    return pl.pallas_call(
        _combine_body,
        grid=(nb,),
        in_specs=[pl.BlockSpec((tm, d), lambda i: (i, 0)),
                  pl.BlockSpec((tm * ROW_TILE, LANES), lambda i: (i, 0)),
                  pl.BlockSpec((tm * ROW_TILE, LANES), lambda i: (i + nb, 0)),
                  pl.BlockSpec((tm, LANES), lambda i: (i, 0))],
        out_specs=pl.BlockSpec((tm, d), lambda i: (i, 0)),
        out_shape=jax.ShapeDtypeStruct((t, d), F32),
        compiler_params=_cparams(1),
        name="moe_combine",
    )(x2, y_slots, y_slots, route)


def _prepare(norm1_g, w_in, q_norm_g, k_norm_g, rpb, attn_out_g, conv_w, conv_b, a_log_f, a_log_b,
             dt_bias_f, dt_bias_b, d_skip, ssm_norm_g, w_out, norm2_g, w_router_group, b_router_group,
             w_router_expert, b_router_expert, w_gate, w_up, w_down):
    d = w_in.shape[0]
    hpg = HEADS_PER_GROUP
    p = {}
    p["norm1_g"] = norm1_g.reshape(1, d).astype(F32)
    p["w_main"] = w_in[:, :D_PROJ_MAIN].astype(BF16)
    w_dt = w_in[:, D_PROJ_MAIN:]
    zeros = jnp.zeros((d, LANES - 2 * hpg), w_in.dtype)
    per_group = lambda v, g: v[..., g * hpg:(g + 1) * hpg]
    p["w_dt"] = jnp.concatenate(
        [jnp.concatenate([per_group(w_dt[:, :N_HEADS_SSM], g), per_group(w_dt[:, N_HEADS_SSM:], g), zeros], axis=1)
         for g in range(N_GROUPS_SSM)], axis=1).astype(BF16)
    lane_rows = lambda f, bwd: jnp.stack(
        [jnp.concatenate([per_group(f, g), per_group(bwd, g), jnp.zeros((LANES - 2 * hpg,), F32)])
         for g in range(N_GROUPS_SSM)])[:, None, :]
    p["dtb"] = lane_rows(dt_bias_f.astype(F32), dt_bias_b.astype(F32))
    p["arow"] = lane_rows(-jnp.exp(a_log_f.astype(F32)), -jnp.exp(a_log_b.astype(F32)))
    scale = HEAD_DIM ** -0.5
    p["qg2"] = (jnp.tile(q_norm_g.astype(F32), 2) * scale).reshape(1, LANES)
    p["kg2"] = jnp.tile(k_norm_g.astype(F32), 2).reshape(1, LANES)
    lane = jnp.arange(LANES)
    p["e_mat"] = (lane[:, None] // HEAD_DIM == lane[None, :] // HEAD_DIM).astype(BF16)
    p["bias"] = _attention_bias(rpb)
    p["attn_out_g"] = attn_out_g.reshape(1, D_ATTN).astype(F32)
    cw = conv_w.reshape(D_CONV, -1).astype(F32)
    cbias = conv_b.reshape(1, -1).astype(F32)
    group_cols = lambda a, g: jnp.concatenate(
        [a[:, g * D_GROUP:(g + 1) * D_GROUP],
         a[:, D_SSM + g * D_STATE:D_SSM + (g + 1) * D_STATE],
         a[:, D_SSM + D_BC + g * D_STATE:D_SSM + D_BC + (g + 1) * D_STATE]], axis=1)
    p["cw"] = jnp.stack([group_cols(cw, g) for g in range(N_GROUPS_SSM)])
    p["cb"] = jnp.stack([group_cols(cbias, g) for g in range(N_GROUPS_SSM)])
    p["dsk"] = jnp.repeat(d_skip.astype(F32), SSM_HEAD_DIM).reshape(N_GROUPS_SSM, 1, D_GROUP)
    p["ng"] = ssm_norm_g.astype(F32).reshape(N_GROUPS_SSM, 1, D_GROUP)
    col_head = jnp.arange(D_GROUP) // SSM_HEAD_DIM
    p["ef"] = (lane[:, None] == col_head[None, :]).astype(BF16)[None]
    p["eb"] = (lane[:, None] == col_head[None, :] + hpg).astype(BF16)[None]
    pos = jnp.arange(CHUNK)
    p["tri"] = jnp.concatenate([pos[:, None] >= pos[None, :], pos[:, None] <= pos[None, :]], axis=0).astype(BF16)
    src = jnp.arange(CHUNK + 2 * CONV_HALO)
    p["shift"] = jnp.concatenate(
        [src[None, :] == pos[:, None] + (CONV_HALO - D_CONV // 2 + k) for k in range(D_CONV) if k != D_CONV // 2],
        axis=0).astype(BF16)
    p["wa"] = w_out[:D_ATTN].astype(BF16)
    p["ws"] = w_out[D_ATTN:].astype(BF16)
    p["norm2_g"] = norm2_g.reshape(1, d).astype(F32)
    n_r = N_EXPERT_GROUPS + N_EXPERTS
    wr = jnp.concatenate([w_router_group, w_router_expert, jnp.zeros((d, LANES - n_r), F32)], axis=1).astype(F32)
    wr_hi = wr.astype(BF16)
    wr_lo = (wr - wr_hi.astype(F32)).astype(BF16)
    p["wr"] = jnp.concatenate([wr_hi, wr_lo], axis=1)
    p["br"] = jnp.concatenate([b_router_group, b_router_expert, jnp.zeros((LANES - n_r,), F32)]).reshape(1, LANES)
    p["w_gate"], p["w_up"], p["w_down"] = w_gate, w_up, w_down
    return p


def _layer(x, p):
    b, l, d = x.shape
    t = b * l
    x2d = x.reshape(t, d)
    proj, dt = _inproj(x2d, p["norm1_g"], p["w_main"], p["w_dt"])
    proj3 = proj.reshape(b, l, D_PROJ_MAIN)
    attn = _attention(proj3, p["qg2"], p["kg2"], p["e_mat"], p["bias"])
    ssm = _ssd(proj3, dt.reshape(b, l, N_GROUPS_SSM * LANES), p["cw"], p["cb"], p["dtb"], p["arow"],
               p["dsk"], p["ng"], p["ef"], p["eb"], p["tri"], p["shift"])
    x2, hn_packed, route = _outproj(x2d, attn.reshape(t, D_ATTN), ssm.reshape(t, D_SSM), p["attn_out_g"],
                                    p["wa"], p["ws"], p["norm2_g"], p["wr"], p["br"])
    visits, row_tok, row_slot = _route_tables(route, t)
    y_slots = _moe(visits, row_tok, row_slot, hn_packed, p["w_gate"], p["w_up"], p["w_down"])
    return _combine(x2, y_slots, route).reshape(b, l, d)


def kernel(x_prompt, x_sample, norm1_g, w_in, q_norm_g, k_norm_g, rpb, attn_out_g, conv_w, conv_b, a_log_f,
           a_log_b, dt_bias_f, dt_bias_b, d_skip, ssm_norm_g, w_out, norm2_g, w_router_group, b_router_group,
           w_router_expert, b_router_expert, w_gate, w_up, w_down):
    weights = (norm1_g, w_in, q_norm_g, k_norm_g, rpb, attn_out_g, conv_w, conv_b, a_log_f, a_log_b, dt_bias_f,
               dt_bias_b, d_skip, ssm_norm_g, w_out, norm2_g, w_router_group, b_router_group, w_router_expert,
               b_router_expert, w_gate, w_up, w_down)
    assert all(w.shape[0] == 1 for w in weights), "one layer of stacked weights expected"
    p = _prepare(*(w[0] for w in weights))
    return (_layer(x_prompt, p), _layer(x_sample, p))
```

```python
import jax
import jax.numpy as jnp
from jax import lax
from jax.experimental import pallas as pl
from jax.experimental.pallas import tpu as pltpu

F32 = jnp.float32
BF16 = jnp.bfloat16
U32 = jnp.uint32
I32 = jnp.int32

EPS = 1e-6
GRID_W = 64
N_HEADS_ATTN = 16
HEAD_DIM = 64
D_ATTN = N_HEADS_ATTN * HEAD_DIM
WIN_H = 8
WIN_W = 16
N_HEADS_SSM = 16
SSM_HEAD_DIM = 64
D_SSM = N_HEADS_SSM * SSM_HEAD_DIM
N_GROUPS_SSM = 2
HEADS_PER_GROUP = N_HEADS_SSM // N_GROUPS_SSM
D_GROUP = D_SSM // N_GROUPS_SSM
D_STATE = 128
D_CONV = 5
CHUNK = 128
D_BC = N_GROUPS_SSM * D_STATE
N_EXPERT_GROUPS = 4
EXPERTS_PER_GROUP = 8
N_EXPERTS = N_EXPERT_GROUPS * EXPERTS_PER_GROUP
TOP_K = 2
D_EXPERT = 512

LANES = 128
BF16_ROWS = 16
CONV_HALO = BF16_ROWS
NEG = -1e30
VMEM_LIMIT_BYTES = 56 * 1024 * 1024
MOE_BLOCK = 256
ATTN_ROW_UNROLL = 8
SSD_CHUNK_UNROLL = 4
MOE_DMA_UNROLL = 8
MOE_SPARE_SLOT = 2
MOE_COL_CHUNKS = 4
ROW_TILE = 8
OUTPROJ_SUB_ROWS = 256

COL_Q, COL_K, COL_V = 0, D_ATTN // LANES, 2 * D_ATTN // LANES
COL_Z = 3 * D_ATTN // D_GROUP
COL_XS = (3 * D_ATTN + D_SSM) // D_GROUP
COL_B = (3 * D_ATTN + 2 * D_SSM) // LANES
COL_C = COL_B + D_BC // LANES
D_PROJ_MAIN = 3 * D_ATTN + 2 * D_SSM + 2 * D_BC


def _cparams(n_axes):
    return pltpu.CompilerParams(dimension_semantics=("arbitrary",) * n_axes,
                                vmem_limit_bytes=VMEM_LIMIT_BYTES)


def _silu(x):
    return x * (0.5 * jnp.tanh(0.5 * x) + 0.5)


def _split3(x):
    hi = x.astype(BF16)
    r1 = x - hi.astype(F32)
    mid = r1.astype(BF16)
    lo = (r1 - mid.astype(F32)).astype(BF16)
    return hi, mid, lo


def _pack_bf16_pairs(x):
    n = x.shape[1] // 2
    u = lax.bitcast_convert_type(x.astype(BF16).astype(F32), U32)
    return (u[:, :n] >> 16) | u[:, n:]


def _unpack_lo(u):
    return lax.bitcast_convert_type(u << 16, F32)


def _unpack_hi(u):
    return lax.bitcast_convert_type(u & jnp.uint32(0xFFFF0000), F32)


def _store_tile_rows(ref, lead, packed):
    m = packed.shape[0]
    for c in range(ROW_TILE):
        ref[lead + (pl.ds(c, m, stride=ROW_TILE), slice(None))] = packed[:, c * LANES:(c + 1) * LANES]


def _load_tile_rows(ref, lead, m):
    return [ref[lead + (pl.ds(c, m, stride=ROW_TILE), slice(None))] for c in range(ROW_TILE)]


def _inproj_body(x_ref, g_ref, w_ref, wdt_ref, o_ref, dt_ref, hn_ref):
    tm = x_ref.shape[0]
    rows = min(tm, 256)

    @pl.when(pl.program_id(1) == 0)
    def _():
        def norm_rows(i, c):
            r0 = pl.multiple_of(i * rows, rows)
            x = x_ref[pl.ds(r0, rows), :]
            ms = jnp.mean(x * x, axis=-1, keepdims=True)
            hn_ref[pl.ds(r0, rows), :] = (x * lax.rsqrt(ms + EPS) * g_ref[...]).astype(BF16)
            return c
        lax.fori_loop(0, tm // rows, norm_rows, 0)
        dt_ref[...] = jnp.dot(hn_ref[...], wdt_ref[...], preferred_element_type=F32)

    o_ref[...] = jnp.dot(hn_ref[...], w_ref[...], preferred_element_type=F32).astype(BF16)


def _inproj(x2d, gain, w_main, w_dt):
    t, d = x2d.shape
    n = w_main.shape[1]
    ndt = w_dt.shape[1]
    tm = min(1024, t)
    tn = 512
    return pl.pallas_call(
        _inproj_body,
        grid=(t // tm, n // tn),
        in_specs=[pl.BlockSpec((tm, d), lambda i, j: (i, 0)),
                  pl.BlockSpec((1, d), lambda i, j: (0, 0)),
                  pl.BlockSpec((d, tn), lambda i, j: (0, j)),
                  pl.BlockSpec((d, ndt), lambda i, j: (0, 0))],
        out_specs=[pl.BlockSpec((tm, tn), lambda i, j: (i, j)),
                   pl.BlockSpec((tm, ndt), lambda i, j: (i, 0))],
        out_shape=[jax.ShapeDtypeStruct((t, n), BF16), jax.ShapeDtypeStruct((t, ndt), F32)],
        scratch_shapes=[pltpu.VMEM((tm, d), BF16)],
        compiler_params=_cparams(2),
        name="inproj",
    )(x2d, gain, w_main, w_dt)


def _attn_body(q_ref, k_ref, v_ref, qg_ref, kg_ref, e_ref, bias_ref, o_ref, qs, ks, s_scr, m_scr):
    l = q_ref.shape[1]
    n_rows = l // GRID_W
    win_keys = WIN_H * GRID_W
    ch = min(l, 512)
    head_a = lax.broadcasted_iota(I32, (1, LANES), 1) < HEAD_DIM
    sel_a = jnp.where(head_a, 1.0, 0.0).astype(BF16)
    sel_b = jnp.where(head_a, 0.0, 1.0).astype(BF16)

    def norm_rows(i, c):
        r0 = pl.multiple_of(i * ch, ch)
        for src, gref, dst in ((q_ref, qg_ref, qs), (k_ref, kg_ref, ks)):
            x = src[0, pl.ds(r0, ch), :].astype(F32)
            ssq = jnp.dot((x * x).astype(BF16), e_ref[...], preferred_element_type=F32)
            dst[pl.ds(r0, ch), :] = (x * lax.rsqrt(ssq * (1.0 / HEAD_DIM) + EPS) * gref[...]).astype(BF16)
        return c
    lax.fori_loop(0, l // ch, norm_rows, 0)

    def key_start(r):
        return jnp.clip(r - WIN_H // 2, 0, n_rows - WIN_H)

    def score_stage(r, slot):
        rs = key_start(r)
        q_r = qs[pl.ds(pl.multiple_of(r * GRID_W, GRID_W), GRID_W), :]
        qm = jnp.concatenate([q_r * sel_a, q_r * sel_b], axis=0)
        kb = ks[pl.ds(pl.multiple_of(rs * GRID_W, GRID_W), win_keys), :]
        s = lax.dot_general(qm, kb, (((1,), (1,)), ((), ())), preferred_element_type=F32)
        dr0 = rs - r + (WIN_H - 1)
        lane0 = pl.multiple_of((dr0 // 2) * LANES, LANES)
        s = s + bias_ref[0, dr0 % 2, :, pl.ds(lane0, win_keys)]
        s_scr[slot] = s
        m_scr[slot] = jnp.max(s, axis=-1, keepdims=True)

    def value_stage(r, slot):
        rs = key_start(r)
        vb = v_ref[0, pl.ds(pl.multiple_of(rs * GRID_W, GRID_W), win_keys), :]
        p = jnp.exp(s_scr[slot] - m_scr[slot])
        den = jnp.sum(p, axis=-1, keepdims=True)
        o = jnp.dot(p.astype(BF16), vb, preferred_element_type=F32) * (1.0 / den)
        out = jnp.where(head_a, o[:GRID_W], o[GRID_W:])
        o_ref[0, pl.ds(pl.multiple_of(r * GRID_W, GRID_W), GRID_W), :] = out.astype(BF16)

    score_stage(0, 0)

    def row_group(i, c):
        r = ATTN_ROW_UNROLL * i
        for j in range(ATTN_ROW_UNROLL):
            score_stage(jnp.minimum(r + j + 1, n_rows - 1), (j + 1) % 2)
            value_stage(r + j, j % 2)
        return c
    lax.fori_loop(0, n_rows // ATTN_ROW_UNROLL, row_group, 0)


def _attention(proj3, qg2, kg2, e_mat, bias):
    b, l, _ = proj3.shape
    n_pairs = N_HEADS_ATTN // 2
    blk = (1, l, LANES)
    return pl.pallas_call(
        _attn_body,
        grid=(n_pairs, b),
        in_specs=[pl.BlockSpec(blk, lambda hp, bi: (bi, 0, COL_Q + hp)),
                  pl.BlockSpec(blk, lambda hp, bi: (bi, 0, COL_K + hp)),
                  pl.BlockSpec(blk, lambda hp, bi: (bi, 0, COL_V + hp)),
                  pl.BlockSpec((1, LANES), lambda hp, bi: (0, 0)),
                  pl.BlockSpec((1, LANES), lambda hp, bi: (0, 0)),
                  pl.BlockSpec((LANES, LANES), lambda hp, bi: (0, 0)),
                  pl.BlockSpec((1,) + bias.shape[1:], lambda hp, bi: (hp, 0, 0, 0))],
        out_specs=pl.BlockSpec(blk, lambda hp, bi: (bi, 0, hp)),
        out_shape=jax.ShapeDtypeStruct((b, l, D_ATTN), BF16),
        scratch_shapes=[pltpu.VMEM((l, LANES), BF16), pltpu.VMEM((l, LANES), BF16),
                        pltpu.VMEM((2, LANES, WIN_H * GRID_W), F32), pltpu.VMEM((2, LANES, 1), F32)],
        compiler_params=_cparams(2),
        name="nbr_attention",
    )(proj3, proj3, proj3, qg2, kg2, e_mat, bias)


def _attention_bias(rpb):
    n_dr = 2 * WIN_H - 1
    cols = jnp.arange(GRID_W, dtype=I32)
    col_start = jnp.clip(cols - WIN_W // 2, 0, GRID_W - WIN_W)
    keys = cols[None, :]
    valid = (keys >= col_start[:, None]) & (keys < col_start[:, None] + WIN_W)
    dc = jnp.clip(keys - cols[:, None] + (WIN_W - 1), 0, 2 * WIN_W - 2)
    tab = jnp.where(valid[None, None], rpb.astype(F32)[:, :, dc], NEG)
    flat = jnp.transpose(tab, (0, 2, 1, 3)).reshape(N_HEADS_ATTN // 2, 2 * GRID_W, n_dr * GRID_W)
    moved = jnp.concatenate([flat[:, :, GRID_W:], jnp.full(flat.shape[:2] + (GRID_W,), NEG, F32)], axis=2)
    return jnp.stack([flat, moved], axis=1)


def _ssd_body(z_ref, xs_ref, b_ref, c_ref, dt_ref, cw_ref, cb_ref, dtb_ref, arow_ref, dsk_ref, ng_ref,
              ef_ref, eb_ref, tri_ref, shift_ref, o_ref, xs_s, b_s, c_s, y_s, sf_s, sb_s, x_f, pre_s):
    l = xs_ref.shape[1]
    n_chunks = l // CHUNK
    hpg = HEADS_PER_GROUP
    ii = lax.broadcasted_iota(I32, (CHUNK, CHUNK), 0)
    jj = lax.broadcasted_iota(I32, (CHUNK, CHUNK), 1)
    causal = ii >= jj
    anti = ii <= jj
    head_a = lax.broadcasted_iota(I32, (1, LANES), 1) < SSM_HEAD_DIM

    def conv_chunk(r0):
        p0 = pl.multiple_of(jnp.maximum(r0 - CONV_HALO, 0), CONV_HALO)
        n0 = pl.multiple_of(jnp.minimum(r0 + CHUNK, l - CONV_HALO), CONV_HALO)
        has_prev = r0 > 0
        has_next = r0 + CHUNK < l
        parts = []
        for ref in (xs_ref, b_ref, c_ref):
            prev = ref[0, pl.ds(p0, CONV_HALO), :]
            nxt = ref[0, pl.ds(n0, CONV_HALO), :]
            parts.append(jnp.concatenate([jnp.where(has_prev, prev, jnp.zeros_like(prev)),
                                          ref[0, pl.ds(r0, CHUNK), :],
                                          jnp.where(has_next, nxt, jnp.zeros_like(nxt))], axis=0))
        ext = jnp.concatenate(parts, axis=1)
        shifted = jnp.dot(shift_ref[...], ext, preferred_element_type=F32)
        mid = D_CONV // 2
        acc = cb_ref[0] + ext[CONV_HALO:CONV_HALO + CHUNK].astype(F32) * cw_ref[0, mid:mid + 1, :]
        for n, k in enumerate(k for k in range(D_CONV) if k != mid):
            acc = acc + shifted[n * CHUNK:(n + 1) * CHUNK] * cw_ref[0, k:k + 1, :]
        return _silu(acc)

    def dt_terms(r0):
        raw = dt_ref[0, pl.ds(r0, CHUNK), :] + dtb_ref[0]
        dtv = jnp.maximum(raw, 0.0) + jnp.log(1.0 + jnp.exp(-jnp.abs(raw)))
        adt = dtv * arow_ref[0]
        hi, mid, lo = _split3(adt)
        cs3 = jnp.dot(tri_ref[...], jnp.concatenate([hi, mid, lo], axis=1), preferred_element_type=F32)
        cs = cs3[:, :LANES] + cs3[:, LANES:2 * LANES] + cs3[:, 2 * LANES:]
        return dtv, cs[:CHUNK], cs[CHUNK:]

    def expand_exact(row, e_ref):
        hi, mid, lo = _split3(jnp.broadcast_to(row, (8, LANES)))
        e = e_ref[0]
        r = (jnp.dot(hi, e, preferred_element_type=F32) + jnp.dot(mid, e, preferred_element_type=F32)
             + jnp.dot(lo, e, preferred_element_type=F32))
        return r[0:1]

    def state_terms(cc, bt, x, dtv, cs, tot, e_ref, s_ref):
        e = e_ref[0]
        expcs = jnp.dot(jnp.exp(cs).astype(BF16), e, preferred_element_type=F32)
        y_off = jnp.dot(cc, s_ref[...].astype(BF16), preferred_element_type=F32) * expcs
        scl = jnp.dot((dtv * jnp.exp(tot - cs)).astype(BF16), e, preferred_element_type=F32)
        states_t = jnp.dot(bt, (x * scl).astype(BF16), preferred_element_type=F32)
        s_ref[...] = s_ref[...] * expand_exact(jnp.exp(tot), e_ref) + states_t
        return y_off

    sf_s[...] = jnp.zeros_like(sf_s)
    sb_s[...] = jnp.zeros_like(sb_s)

    def prep_stage(c, slot):
        r0 = pl.multiple_of(c * CHUNK, CHUNK)
        xbc = conv_chunk(r0)
        x = xbc[:, :D_GROUP]
        bm = xbc[:, D_GROUP:D_GROUP + D_STATE]
        xs_s[pl.ds(r0, CHUNK), :] = x.astype(BF16)
        b_s[pl.ds(r0, CHUNK), :] = bm.astype(BF16)
        c_s[pl.ds(r0, CHUNK), :] = xbc[:, D_GROUP + D_STATE:].astype(BF16)
        x_f[slot] = x
        dtv, cs_f, cs_b = dt_terms(r0)
        for n, a in enumerate((dtv, cs_f, cs_b, cs_f.T, cs_b.T, dtv.T, bm.T)):
            pre_s[slot, n] = a

    def mix_stage(c, slot):
        r0 = pl.multiple_of(c * CHUNK, CHUNK)
        x = x_f[slot]
        xb = xs_s[pl.ds(r0, CHUNK), :]
        bb = b_s[pl.ds(r0, CHUNK), :]
        cc = c_s[pl.ds(r0, CHUNK), :]
        dtv, cs_f, cs_b, cst_f, cst_b, dtt, bmt = (pre_s[slot, n] for n in range(7))
        cb = lax.dot_general(cc, bb, (((1,), (1,)), ((), ())), preferred_element_type=F32)
        pieces = []
        for pair in range(hpg // 2):
            xp = xb[:, pair * LANES:(pair + 1) * LANES]
            ys = []
            for hh in range(2):
                h = 2 * pair + hh
                df = cs_f[:, h:h + 1] - cst_f[h:h + 1, :]
                db = cs_b[:, hpg + h:hpg + h + 1] - cst_b[hpg + h:hpg + h + 1, :]
                lf = jnp.exp(jnp.where(causal, df, NEG)) * dtt[h:h + 1, :]
                lb = jnp.exp(jnp.where(anti, db, NEG)) * dtt[hpg + h:hpg + h + 1, :]
                m = (cb * (lf + lb)).astype(BF16)
                ys.append(jnp.dot(m, xp, preferred_element_type=F32))
            pieces.append(jnp.where(head_a, ys[0], ys[1]))
        y = jnp.concatenate(pieces, axis=1) + x * dsk_ref[0]
        y = y + state_terms(cc, bmt.astype(BF16), x, dtv, cs_f, cs_f[CHUNK - 1:CHUNK, :], ef_ref, sf_s)
        y_s[pl.ds(r0, CHUNK), :] = y

    prep_stage(0, 0)

    def fwd_group(i, carry):
        c = SSD_CHUNK_UNROLL * i
        for j in range(SSD_CHUNK_UNROLL):
            prep_stage(jnp.minimum(c + j + 1, n_chunks - 1), (j + 1) % 2)
            mix_stage(c + j, j % 2)
        return carry
    lax.fori_loop(0, n_chunks // SSD_CHUNK_UNROLL, fwd_group, 0)

    def bwd_prep_stage(c, slot):
        r0 = pl.multiple_of(c * CHUNK, CHUNK)
        dtv, _, cs_b = dt_terms(r0)
        pre_s[slot, 0] = dtv
        pre_s[slot, 2] = cs_b
        pre_s[slot, 6] = b_s[pl.ds(r0, CHUNK), :].astype(F32).T

    def bwd_mix_stage(c, slot):
        r0 = pl.multiple_of(c * CHUNK, CHUNK)
        x = xs_s[pl.ds(r0, CHUNK), :].astype(F32)
        cc = c_s[pl.ds(r0, CHUNK), :]
        dtv, cs_b, bt = pre_s[slot, 0], pre_s[slot, 2], pre_s[slot, 6].astype(BF16)
        y = y_s[pl.ds(r0, CHUNK), :] + state_terms(cc, bt, x, dtv, cs_b, cs_b[0:1, :], eb_ref, sb_s)
        y = y * _silu(z_ref[0, pl.ds(r0, CHUNK), :].astype(F32))
        ms = jnp.mean(y * y, axis=-1, keepdims=True)
        o_ref[0, pl.ds(r0, CHUNK), :] = (y * lax.rsqrt(ms + EPS) * ng_ref[0]).astype(BF16)

    bwd_prep_stage(n_chunks - 1, 0)

    def bwd_group(i, carry):
        c = n_chunks - 1 - SSD_CHUNK_UNROLL * i
        for j in range(SSD_CHUNK_UNROLL):
            bwd_prep_stage(jnp.maximum(c - j - 1, 0), (j + 1) % 2)
            bwd_mix_stage(c - j, j % 2)
        return carry
    lax.fori_loop(0, n_chunks // SSD_CHUNK_UNROLL, bwd_group, 0)


def _ssd(proj3, dt3, cw, cb, dtb, arow, dsk, ng, ef, eb, tri, shift):
    b, l, _ = proj3.shape
    g = N_GROUPS_SSM
    wconv = D_GROUP + 2 * D_STATE
    per_group = lambda shape: pl.BlockSpec((1,) + shape, lambda bi, gi: (gi,) + (0,) * len(shape))
    return pl.pallas_call(
        _ssd_body,
        grid=(b, g),
        in_specs=[pl.BlockSpec((1, l, D_GROUP), lambda bi, gi: (bi, 0, COL_Z + gi)),
                  pl.BlockSpec((1, l, D_GROUP), lambda bi, gi: (bi, 0, COL_XS + gi)),
                  pl.BlockSpec((1, l, D_STATE), lambda bi, gi: (bi, 0, COL_B + gi)),
                  pl.BlockSpec((1, l, D_STATE), lambda bi, gi: (bi, 0, COL_C + gi)),
                  pl.BlockSpec((1, l, LANES), lambda bi, gi: (bi, 0, gi)),
                  per_group((D_CONV, wconv)), per_group((1, wconv)),
                  per_group((1, LANES)), per_group((1, LANES)),
                  per_group((1, D_GROUP)), per_group((1, D_GROUP)),
                  pl.BlockSpec((1, LANES, D_GROUP), lambda bi, gi: (0, 0, 0)),
                  pl.BlockSpec((1, LANES, D_GROUP), lambda bi, gi: (0, 0, 0)),
                  pl.BlockSpec((2 * CHUNK, CHUNK), lambda bi, gi: (0, 0)),
                  pl.BlockSpec(shift.shape, lambda bi, gi: (0, 0))],
        out_specs=pl.BlockSpec((1, l, D_GROUP), lambda bi, gi: (bi, 0, gi)),
        out_shape=jax.ShapeDtypeStruct((b, l, D_SSM), BF16),
        scratch_shapes=[pltpu.VMEM((l, D_GROUP), BF16), pltpu.VMEM((l, D_STATE), BF16),
                        pltpu.VMEM((l, D_STATE), BF16), pltpu.VMEM((l, D_GROUP), F32),
                        pltpu.VMEM((D_STATE, D_GROUP), F32), pltpu.VMEM((D_STATE, D_GROUP), F32),
                        pltpu.VMEM((2, CHUNK, D_GROUP), F32), pltpu.VMEM((2, 7, CHUNK, LANES), F32)],
        compiler_params=_cparams(2),
        name="ssd_mixer",
    )(proj3, proj3, proj3, proj3, dt3, cw, cb, dtb, arow, dsk, ng, ef, eb, tri, shift)


def _route_rows(lg):
    lane = lax.broadcasted_iota(I32, lg.shape, 1).astype(F32)
    n_g = float(N_EXPERT_GROUPS)
    n_e = float(EXPERTS_PER_GROUP)
    gl = jnp.where(lane < n_g, lg, NEG)
    gmax = jnp.max(gl, axis=-1, keepdims=True)
    g_idx = jnp.min(jnp.where(gl == gmax, lane, float(LANES)), axis=-1, keepdims=True)
    g_p = 1.0 / jnp.sum(jnp.exp(gl - gmax), axis=-1, keepdims=True)
    first = n_g + n_e * g_idx
    el = jnp.where((lane >= first) & (lane < first + n_e), lg, NEG)
    e1 = jnp.max(el, axis=-1, keepdims=True)
    i1 = jnp.min(jnp.where(el == e1, lane, float(LANES)), axis=-1, keepdims=True)
    el2 = jnp.where(lane == i1, NEG, el)
    e2 = jnp.max(el2, axis=-1, keepdims=True)
    i2 = jnp.min(jnp.where(el2 == e2, lane, float(LANES)), axis=-1, keepdims=True)
    r = jnp.exp(e2 - e1)
    gate1 = g_p / (1.0 + r)
    gate2 = gate1 * r
    return jnp.where(lane == 0.0, i1 - n_g,
                     jnp.where(lane == 1.0, i2 - n_g,
                               jnp.where(lane == 2.0, gate1, jnp.where(lane == 3.0, gate2, 0.0))))


def _outproj_body(x_ref, a_ref, s_ref, ag_ref, wa_ref, ws_ref, g2_ref, wr_ref, br_ref, x2_ref, hp_ref, rt_ref):
    tm = x_ref.shape[0]
    sub = min(tm, OUTPROJ_SUB_ROWS)
    for r0 in range(0, tm, sub):
        rows = slice(r0, r0 + sub)
        a = a_ref[rows, :].astype(F32)
        ms = jnp.mean(a * a, axis=-1, keepdims=True)
        an = (a * lax.rsqrt(ms + EPS) * ag_ref[...]).astype(BF16)
        y = (jnp.dot(an, wa_ref[...], preferred_element_type=F32)
             + jnp.dot(s_ref[rows, :], ws_ref[...], preferred_element_type=F32))
        x2 = x_ref[rows, :] + y
        x2_ref[rows, :] = x2
        ms2 = jnp.mean(x2 * x2, axis=-1, keepdims=True)
        hn = x2 * lax.rsqrt(ms2 + EPS) * g2_ref[...]
        hi = hn.astype(BF16)
        lo = (hn - hi.astype(F32)).astype(BF16)
        l1 = jnp.dot(hi, wr_ref[...], preferred_element_type=F32)
        l2 = jnp.dot(lo, wr_ref[:, :LANES], preferred_element_type=F32)
        rt_ref[rows, :] = _route_rows(l1[:, :LANES] + l1[:, LANES:] + l2 + br_ref[...])
        packed = _pack_bf16_pairs(hi.astype(F32))
        for c in range(ROW_TILE):
            hp_ref[pl.ds(r0 * ROW_TILE + c, sub, stride=ROW_TILE), :] = packed[:, c * LANES:(c + 1) * LANES]


def _outproj(x2d, attn2d, ssm2d, ag, wa, ws, g2, wr, br):
    t, d = x2d.shape
    tm = min(2 * OUTPROJ_SUB_ROWS, t)
    row = lambda w: pl.BlockSpec((tm, w), lambda i: (i, 0))
    full = lambda a: pl.BlockSpec(a.shape, lambda i: (0,) * a.ndim)
    return pl.pallas_call(
        _outproj_body,
        grid=(t // tm,),
        in_specs=[row(d), row(D_ATTN), row(D_SSM), full(ag), full(wa), full(ws), full(g2), full(wr), full(br)],
        out_specs=[row(d), pl.BlockSpec((tm * ROW_TILE, LANES), lambda i: (i, 0)), row(LANES)],
        out_shape=[jax.ShapeDtypeStruct((t, d), F32), jax.ShapeDtypeStruct((t * ROW_TILE, LANES), U32),
                   jax.ShapeDtypeStruct((t, LANES), F32)],
        compiler_params=_cparams(1),
        name="outproj_router",
    )(x2d, attn2d, ssm2d, ag, wa, ws, g2, wr, br)


def _moe_body(vblk_ref, vexp_ref, vlo_ref, vhi_ref, tok_ref, tok_next_ref, slot_ref, slot_prev_ref, x_hbm, wg_ref,
              wu_ref, wd_ref, y_hbm, xbuf, ybuf, wgu_bf, wd_bf, cached_ref, gsem, ssem):
    blk = xbuf.shape[1] // ROW_TILE
    d_half = wg_ref.shape[1] // 2
    v = pl.program_id(0)
    lo = vlo_ref[v]
    hi = vhi_ref[v]
    s = vblk_ref[v]
    expert = vexp_ref[v]
    n_blocks = y_hbm.shape[0] // (blk * ROW_TILE) - 1
    spare_row0 = n_blocks * blk
    slot = s % 2
    other = 1 - slot

    def tile_rows(i):
        return pl.ds(pl.multiple_of(i * ROW_TILE, ROW_TILE), ROW_TILE)

    def start_gather(table_ref, buf_slot):
        def body(i, c):
            pltpu.make_async_copy(x_hbm.at[tile_rows(table_ref[0, 0, i])], xbuf.at[buf_slot, tile_rows(i)],
                                  gsem.at[buf_slot]).start()
            return c
        lax.fori_loop(0, blk, body, 0, unroll=MOE_DMA_UNROLL)

    def wait_gather(buf_slot):
        pltpu.make_async_copy(x_hbm.at[pl.ds(0, blk * ROW_TILE)], xbuf.at[buf_slot], gsem.at[buf_slot]).wait()

    def start_scatter(buf_slot):
        def body(i, c):
            pltpu.make_async_copy(ybuf.at[buf_slot, tile_rows(i)], y_hbm.at[tile_rows(slot_ref[0, 0, i])],
                                  ssem.at[buf_slot]).start()
            return c
        lax.fori_loop(0, blk, body, 0, unroll=MOE_DMA_UNROLL)

    def wait_scatter(sem_slot):
        pltpu.make_async_copy(ybuf.at[0], y_hbm.at[pl.ds(0, blk * ROW_TILE)], ssem.at[sem_slot]).wait()

    @pl.when(v == 0)
    def _():
        cached_ref[0] = -1
        ybuf[...] = jnp.zeros_like(ybuf)

    @pl.when(hi > lo)
    def _():
        @pl.when(lo == 0)
        def _():
            @pl.when(s == 0)
            def _():
                start_gather(tok_ref, slot)
            wait_gather(slot)

            @pl.when(s >= 2)
            def _():
                wait_scatter(slot)

        @pl.when(cached_ref[0] != expert)
        def _():
            rows = 256

            def cast_rows(i, c):
                r0 = pl.multiple_of(i * rows, rows)
                wgu_bf[pl.ds(r0, rows), :D_EXPERT] = wg_ref[0, pl.ds(r0, rows), :].astype(BF16)
                wgu_bf[pl.ds(r0, rows), D_EXPERT:] = wu_ref[0, pl.ds(r0, rows), :].astype(BF16)
                return c
            lax.fori_loop(0, 2 * d_half // rows, cast_rows, 0)
            wd_bf[...] = wd_ref[0].astype(BF16)
            cached_ref[0] = expert

        prefetch = (lo == 0) & (s + 1 < n_blocks)
        dst = jnp.where(prefetch, other, MOE_SPARE_SLOT)
        drain = (lo == 0) & (s > 0)
        dsem = jnp.where(drain, other, MOE_SPARE_SLOT)

        def issue_rows(first, count):
            for i in range(first, first + count):
                pltpu.make_async_copy(x_hbm.at[tile_rows(tok_next_ref[0, 0, i])],
                                      xbuf.at[dst, pl.ds(i * ROW_TILE, ROW_TILE)], gsem.at[dst]).start()
                out_row = jnp.where(drain, slot_prev_ref[0, 0, i], spare_row0 + i)
                pltpu.make_async_copy(ybuf.at[other, pl.ds(i * ROW_TILE, ROW_TILE)], y_hbm.at[tile_rows(out_row)],
                                      ssem.at[dsem]).start()

        words = _load_tile_rows(xbuf, (slot,), blk)
        xlo = jnp.concatenate([_unpack_lo(w).astype(BF16) for w in words], axis=1)
        xhi = jnp.concatenate([_unpack_hi(w).astype(BF16) for w in words], axis=1)
        quarter = blk // 4
        gate_up = []
        for j in range(2):
            issue_rows(j * quarter, quarter)
            cols = slice(j * D_EXPERT, (j + 1) * D_EXPERT)
            gate_up.append(jnp.dot(xlo, wgu_bf[:d_half, cols], preferred_element_type=F32)
                           + jnp.dot(xhi, wgu_bf[d_half:, cols], preferred_element_type=F32))
        hid = (_silu(gate_up[0]) * gate_up[1]).astype(BF16)
        ow = wd_bf.shape[1] // MOE_COL_CHUNKS
        per_chunk = 2 * quarter // MOE_COL_CHUNKS
        outs = []
        for j in range(MOE_COL_CHUNKS):
            issue_rows(2 * quarter + j * per_chunk, per_chunk)
            outs.append(jnp.dot(hid, wd_bf[:, j * ow:(j + 1) * ow], preferred_element_type=F32))
        y = _pack_bf16_pairs(jnp.concatenate(outs, axis=1))

        @pl.when(jnp.logical_not(prefetch))
        def _():
            wait_gather(MOE_SPARE_SLOT)

        @pl.when(jnp.logical_not(drain))
        def _():
            wait_scatter(MOE_SPARE_SLOT)

        @pl.when(lo == 0)
        def _():
            _store_tile_rows(ybuf, (slot,), y)

        @pl.when(lo > 0)
        def _():
            row = lax.broadcasted_iota(I32, (blk, 1), 0)
            old = jnp.concatenate(_load_tile_rows(ybuf, (slot,), blk), axis=1)
            _store_tile_rows(ybuf, (slot,), jnp.where((row >= lo) & (row < hi), y, old))

        @pl.when((hi == blk) & (s == n_blocks - 1))
        def _():
            if n_blocks > 1:
                wait_scatter(other)
            start_scatter(slot)
            wait_scatter(slot)


def _moe(visits, row_tok, row_slot, hn_tiles, w_gate, w_up, w_down):
    blk = MOE_BLOCK
    n_assign = row_tok.shape[0]
    n_blocks = n_assign // blk
    n_visits = visits[0].shape[0]
    d = w_gate.shape[1]
    smem_rows = pl.BlockSpec((1, 1, blk), lambda v, vb, ve, vl, vh: (vb[v], 0, 0), memory_space=pltpu.SMEM)
    smem_next = pl.BlockSpec((1, 1, blk), lambda v, vb, ve, vl, vh: (jnp.minimum(vb[v] + 1, n_blocks - 1), 0, 0),
                             memory_space=pltpu.SMEM)
    smem_prev = pl.BlockSpec((1, 1, blk), lambda v, vb, ve, vl, vh: (jnp.maximum(vb[v] - 1, 0), 0, 0),
                             memory_space=pltpu.SMEM)
    by_expert = lambda shape: pl.BlockSpec((1,) + shape, lambda v, vb, ve, vl, vh: (ve[v], 0, 0))
    grid_spec = pltpu.PrefetchScalarGridSpec(
        num_scalar_prefetch=4,
        grid=(n_visits,),
        in_specs=[smem_rows, smem_next, smem_rows, smem_prev,
                  pl.BlockSpec(memory_space=pl.ANY),
                  by_expert((d, D_EXPERT)), by_expert((d, D_EXPERT)), by_expert((D_EXPERT, d))],
        out_specs=pl.BlockSpec(memory_space=pl.ANY),
        scratch_shapes=[pltpu.VMEM((MOE_SPARE_SLOT + 1, blk * ROW_TILE, LANES), U32),
                        pltpu.VMEM((2, blk * ROW_TILE, LANES), U32),
                        pltpu.VMEM((d, 2 * D_EXPERT), BF16), pltpu.VMEM((D_EXPERT, d), BF16),
                        pltpu.SMEM((1,), I32),
                        pltpu.SemaphoreType.DMA((MOE_SPARE_SLOT + 1,)),
                        pltpu.SemaphoreType.DMA((MOE_SPARE_SLOT + 1,))],
    )
    tok3 = row_tok.reshape(n_blocks, 1, blk)
    slot3 = row_slot.reshape(n_blocks, 1, blk)
    return pl.pallas_call(
        _moe_body,
        grid_spec=grid_spec,
        out_shape=jax.ShapeDtypeStruct(((n_assign + blk) * ROW_TILE, LANES), U32),
        compiler_params=_cparams(1),
        name="moe_experts",
    )(*visits, tok3, tok3, slot3, slot3, hn_tiles, w_gate, w_up, w_down)


def _route_tables(route, t):
    blk = MOE_BLOCK
    n_assign = t * TOP_K
    n_blocks = n_assign // blk
    flat_e = route[:, :TOP_K].astype(I32).T.reshape(n_assign)
    order = jnp.argsort(flat_e, stable=True).astype(I32)
    counts = jnp.sum((flat_e[:, None] == jnp.arange(N_EXPERTS, dtype=I32)[None, :]).astype(I32), axis=0)
    ends = jnp.cumsum(counts)
    starts = ends - counts
    cuts = jnp.sort(jnp.concatenate([jnp.arange(n_blocks, dtype=I32) * blk, starts[1:]]))
    nxt = jnp.concatenate([cuts[1:], jnp.full((1,), n_assign, I32)])
    vblk = jnp.minimum(cuts // blk, n_blocks - 1)
    vexp = jnp.minimum(jnp.sum((ends[None, :] <= cuts[:, None]).astype(I32), axis=1), N_EXPERTS - 1)
    vlo = cuts - vblk * blk
    vhi = jnp.maximum(jnp.minimum(nxt, (vblk + 1) * blk) - vblk * blk, vlo)
    return (vblk, vexp, vlo, vhi), order % t, order


def _combine_body(x2_ref, y0_ref, y1_ref, rt_ref, o_ref):
    tm = o_ref.shape[0]
    half = o_ref.shape[1] // 2
    g0 = rt_ref[:, TOP_K:TOP_K + 1]
    g1 = rt_ref[:, TOP_K + 1:TOP_K + 2]
    y0 = _load_tile_rows(y0_ref, (), tm)
    y1 = _load_tile_rows(y1_ref, (), tm)
    for c in range(ROW_TILE):
        lo_cols = slice(c * LANES, (c + 1) * LANES)
        hi_cols = slice(half + c * LANES, half + (c + 1) * LANES)
        o_ref[:, lo_cols] = x2_ref[:, lo_cols] + g0 * _unpack_lo(y0[c]) + g1 * _unpack_lo(y1[c])
        o_ref[:, hi_cols] = x2_ref[:, hi_cols] + g0 * _unpack_hi(y0[c]) + g1 * _unpack_hi(y1[c])


def _combine(x2, y_slots, route):
    t, d = x2.shape
    tm = min(512, t)
    nb = t // tm
    return pl.pallas_call(
        _combine_body,
        grid=(nb,),
        in_specs=[pl.BlockSpec((tm, d), lambda i: (i, 0)),
                  pl.BlockSpec((tm * ROW_TILE, LANES), lambda i: (i, 0)),
                  pl.BlockSpec((tm * ROW_TILE, LANES), lambda i: (i + nb, 0)),
                  pl.BlockSpec((tm, LANES), lambda i: (i, 0))],
        out_specs=pl.BlockSpec((tm, d), lambda i: (i, 0)),
        out_shape=jax.ShapeDtypeStruct((t, d), F32),
        compiler_params=_cparams(1),
        name="moe_combine",
    )(x2, y_slots, y_slots, route)


def _prepare(norm1_g, w_in, q_norm_g, k_norm_g, rpb, attn_out_g, conv_w, conv_b, a_log_f, a_log_b,
             dt_bias_f, dt_bias_b, d_skip, ssm_norm_g, w_out, norm2_g, w_router_group, b_router_group,
             w_router_expert, b_router_expert, w_gate, w_up, w_down):
    d = w_in.shape[0]
    hpg = HEADS_PER_GROUP
    p = {}
    p["norm1_g"] = norm1_g.reshape(1, d).astype(F32)
    p["w_main"] = w_in[:, :D_PROJ_MAIN].astype(BF16)
    w_dt = w_in[:, D_PROJ_MAIN:]
    zeros = jnp.zeros((d, LANES - 2 * hpg), w_in.dtype)
    per_group = lambda v, g: v[..., g * hpg:(g + 1) * hpg]
    p["w_dt"] = jnp.concatenate(
        [jnp.concatenate([per_group(w_dt[:, :N_HEADS_SSM], g), per_group(w_dt[:, N_HEADS_SSM:], g), zeros], axis=1)
         for g in range(N_GROUPS_SSM)], axis=1).astype(BF16)
    lane_rows = lambda f, bwd: jnp.stack(
        [jnp.concatenate([per_group(f, g), per_group(bwd, g), jnp.zeros((LANES - 2 * hpg,), F32)])
         for g in range(N_GROUPS_SSM)])[:, None, :]
    p["dtb"] = lane_rows(dt_bias_f.astype(F32), dt_bias_b.astype(F32))
    p["arow"] = lane_rows(-jnp.exp(a_log_f.astype(F32)), -jnp.exp(a_log_b.astype(F32)))
    scale = HEAD_DIM ** -0.5
    p["qg2"] = (jnp.tile(q_norm_g.astype(F32), 2) * scale).reshape(1, LANES)
    p["kg2"] = jnp.tile(k_norm_g.astype(F32), 2).reshape(1, LANES)
    lane = jnp.arange(LANES)
    p["e_mat"] = (lane[:, None] // HEAD_DIM == lane[None, :] // HEAD_DIM).astype(BF16)
    p["bias"] = _attention_bias(rpb)
    p["attn_out_g"] = attn_out_g.reshape(1, D_ATTN).astype(F32)
    cw = conv_w.reshape(D_CONV, -1).astype(F32)
    cbias = conv_b.reshape(1, -1).astype(F32)
    group_cols = lambda a, g: jnp.concatenate(
        [a[:, g * D_GROUP:(g + 1) * D_GROUP],
         a[:, D_SSM + g * D_STATE:D_SSM + (g + 1) * D_STATE],
         a[:, D_SSM + D_BC + g * D_STATE:D_SSM + D_BC + (g + 1) * D_STATE]], axis=1)
    p["cw"] = jnp.stack([group_cols(cw, g) for g in range(N_GROUPS_SSM)])
    p["cb"] = jnp.stack([group_cols(cbias, g) for g in range(N_GROUPS_SSM)])
    p["dsk"] = jnp.repeat(d_skip.astype(F32), SSM_HEAD_DIM).reshape(N_GROUPS_SSM, 1, D_GROUP)
    p["ng"] = ssm_norm_g.astype(F32).reshape(N_GROUPS_SSM, 1, D_GROUP)
    col_head = jnp.arange(D_GROUP) // SSM_HEAD_DIM
    p["ef"] = (lane[:, None] == col_head[None, :]).astype(BF16)[None]
    p["eb"] = (lane[:, None] == col_head[None, :] + hpg).astype(BF16)[None]
    pos = jnp.arange(CHUNK)
    p["tri"] = jnp.concatenate([pos[:, None] >= pos[None, :], pos[:, None] <= pos[None, :]], axis=0).astype(BF16)
    src = jnp.arange(CHUNK + 2 * CONV_HALO)
    p["shift"] = jnp.concatenate(
        [src[None, :] == pos[:, None] + (CONV_HALO - D_CONV // 2 + k) for k in range(D_CONV) if k != D_CONV // 2],
        axis=0).astype(BF16)
    p["wa"] = w_out[:D_ATTN].astype(BF16)
    p["ws"] = w_out[D_ATTN:].astype(BF16)
    p["norm2_g"] = norm2_g.reshape(1, d).astype(F32)
    n_r = N_EXPERT_GROUPS + N_EXPERTS
    wr = jnp.concatenate([w_router_group, w_router_expert, jnp.zeros((d, LANES - n_r), F32)], axis=1).astype(F32)
    wr_hi = wr.astype(BF16)
    wr_lo = (wr - wr_hi.astype(F32)).astype(BF16)
    p["wr"] = jnp.concatenate([wr_hi, wr_lo], axis=1)
    p["br"] = jnp.concatenate([b_router_group, b_router_expert, jnp.zeros((LANES - n_r,), F32)]).reshape(1, LANES)
    p["w_gate"], p["w_up"], p["w_down"] = w_gate, w_up, w_down
    return p


def _layer(x, p):
    b, l, d = x.shape
    t = b * l
    x2d = x.reshape(t, d)
    proj, dt = _inproj(x2d, p["norm1_g"], p["w_main"], p["w_dt"])
    proj3 = proj.reshape(b, l, D_PROJ_MAIN)
    attn = _attention(proj3, p["qg2"], p["kg2"], p["e_mat"], p["bias"])
    ssm = _ssd(proj3, dt.reshape(b, l, N_GROUPS_SSM * LANES), p["cw"], p["cb"], p["dtb"], p["arow"],
               p["dsk"], p["ng"], p["ef"], p["eb"], p["tri"], p["shift"])
    x2, hn_packed, route = _outproj(x2d, attn.reshape(t, D_ATTN), ssm.reshape(t, D_SSM), p["attn_out_g"],
                                    p["wa"], p["ws"], p["norm2_g"], p["wr"], p["br"])
    visits, row_tok, row_slot = _route_tables(route, t)
    y_slots = _moe(visits, row_tok, row_slot, hn_packed, p["w_gate"], p["w_up"], p["w_down"])
    return _combine(x2, y_slots, route).reshape(b, l, d)


def kernel(x_prompt, x_sample, norm1_g, w_in, q_norm_g, k_norm_g, rpb, attn_out_g, conv_w, conv_b, a_log_f,
           a_log_b, dt_bias_f, dt_bias_b, d_skip, ssm_norm_g, w_out, norm2_g, w_router_group, b_router_group,
           w_router_expert, b_router_expert, w_gate, w_up, w_down):
    weights = (norm1_g, w_in, q_norm_g, k_norm_g, rpb, attn_out_g, conv_w, conv_b, a_log_f, a_log_b, dt_bias_f,
               dt_bias_b, d_skip, ssm_norm_g, w_out, norm2_g, w_router_group, b_router_group, w_router_expert,
               b_router_expert, w_gate, w_up, w_down)
    assert all(w.shape[0] == 1 for w in weights), "one layer of stacked weights expected"
    p = _prepare(*(w[0] for w in weights))
    return (_layer(x_prompt, p), _layer(x_sample, p))
```

```python
import jax
import jax.numpy as jnp
from jax import lax
from jax.experimental import pallas as pl
from jax.experimental.pallas import tpu as pltpu

F32 = jnp.float32
BF16 = jnp.bfloat16
U32 = jnp.uint32
I32 = jnp.int32

EPS = 1e-6
GRID_W = 64
N_HEADS_ATTN = 16
HEAD_DIM = 64
D_ATTN = N_HEADS_ATTN * HEAD_DIM
WIN_H = 8
WIN_W = 16
N_HEADS_SSM = 16
SSM_HEAD_DIM = 64
D_SSM = N_HEADS_SSM * SSM_HEAD_DIM
N_GROUPS_SSM = 2
HEADS_PER_GROUP = N_HEADS_SSM // N_GROUPS_SSM
D_GROUP = D_SSM // N_GROUPS_SSM
D_STATE = 128
D_CONV = 5
CHUNK = 128
D_BC = N_GROUPS_SSM * D_STATE
N_EXPERT_GROUPS = 4
EXPERTS_PER_GROUP = 8
N_EXPERTS = N_EXPERT_GROUPS * EXPERTS_PER_GROUP
TOP_K = 2
D_EXPERT = 512

LANES = 128
BF16_ROWS = 16
CONV_HALO = BF16_ROWS
NEG = -1e30
VMEM_LIMIT_BYTES = 56 * 1024 * 1024
MOE_BLOCK = 256
ATTN_ROW_UNROLL = 8
SSD_CHUNK_UNROLL = 4
MOE_DMA_UNROLL = 8
MOE_SPARE_SLOT = 2
MOE_COL_CHUNKS = 4
ROW_TILE = 8
OUTPROJ_SUB_ROWS = 256
INPROJ_ROWS = 512
INPROJ_COLS = 2816

COL_Q, COL_K, COL_V = 0, D_ATTN // LANES, 2 * D_ATTN // LANES
COL_Z = 3 * D_ATTN // D_GROUP
COL_XS = (3 * D_ATTN + D_SSM) // D_GROUP
COL_B = (3 * D_ATTN + 2 * D_SSM) // LANES
COL_C = COL_B + D_BC // LANES
D_PROJ_MAIN = 3 * D_ATTN + 2 * D_SSM + 2 * D_BC


def _cparams(n_axes):
    return pltpu.CompilerParams(dimension_semantics=("arbitrary",) * n_axes,
                                vmem_limit_bytes=VMEM_LIMIT_BYTES)


def _silu(x):
    return x * (0.5 * jnp.tanh(0.5 * x) + 0.5)


def _split3(x):
    hi = x.astype(BF16)
    r1 = x - hi.astype(F32)
    mid = r1.astype(BF16)
    lo = (r1 - mid.astype(F32)).astype(BF16)
    return hi, mid, lo


def _pack_bf16_pairs(x):
    n = x.shape[1] // 2
    u = lax.bitcast_convert_type(x.astype(BF16).astype(F32), U32)
    return (u[:, :n] >> 16) | u[:, n:]


def _unpack_lo(u):
    return lax.bitcast_convert_type(u << 16, F32)


def _unpack_hi(u):
    return lax.bitcast_convert_type(u & jnp.uint32(0xFFFF0000), F32)


def _store_tile_rows(ref, lead, packed):
    m = packed.shape[0]
    for c in range(ROW_TILE):
        ref[lead + (pl.ds(c, m, stride=ROW_TILE), slice(None))] = packed[:, c * LANES:(c + 1) * LANES]


def _load_tile_rows(ref, lead, m):
    return [ref[lead + (pl.ds(c, m, stride=ROW_TILE), slice(None))] for c in range(ROW_TILE)]


def _inproj_body(x_ref, g_ref, w_ref, wdt_ref, o_ref, dt_ref, hn_ref):
    tm = x_ref.shape[0]
    rows = min(tm, 256)

    @pl.when(pl.program_id(1) == 0)
    def _():
        def norm_rows(i, c):
            r0 = pl.multiple_of(i * rows, rows)
            x = x_ref[pl.ds(r0, rows), :]
            ms = jnp.mean(x * x, axis=-1, keepdims=True)
            hn_ref[pl.ds(r0, rows), :] = (x * lax.rsqrt(ms + EPS) * g_ref[...]).astype(BF16)
            return c
        lax.fori_loop(0, tm // rows, norm_rows, 0)
        dt_ref[...] = jnp.dot(hn_ref[...], wdt_ref[...], preferred_element_type=F32)

    o_ref[...] = jnp.dot(hn_ref[...], w_ref[...], preferred_element_type=F32).astype(BF16)


def _inproj(x2d, gain, w_main, w_dt):
    t, d = x2d.shape
    n = w_main.shape[1]
    ndt = w_dt.shape[1]
    tm = min(INPROJ_ROWS, t)
    tn = INPROJ_COLS if n % INPROJ_COLS == 0 else 4 * LANES
    return pl.pallas_call(
        _inproj_body,
        grid=(t // tm, n // tn),
        in_specs=[pl.BlockSpec((tm, d), lambda i, j: (i, 0)),
                  pl.BlockSpec((1, d), lambda i, j: (0, 0)),
                  pl.BlockSpec((d, tn), lambda i, j: (0, j)),
                  pl.BlockSpec((d, ndt), lambda i, j: (0, 0))],
        out_specs=[pl.BlockSpec((tm, tn), lambda i, j: (i, j)),
                   pl.BlockSpec((tm, ndt), lambda i, j: (i, 0))],
        out_shape=[jax.ShapeDtypeStruct((t, n), BF16), jax.ShapeDtypeStruct((t, ndt), F32)],
        scratch_shapes=[pltpu.VMEM((tm, d), BF16)],
        compiler_params=_cparams(2),
        name="inproj",
    )(x2d, gain, w_main, w_dt)


def _attn_body(q_ref, k_ref, v_ref, qg_ref, kg_ref, e_ref, bias_ref, o_ref, qs, ks, s_scr, m_scr):
    l = q_ref.shape[1]
    n_rows = l // GRID_W
    win_keys = WIN_H * GRID_W
    ch = min(l, 512)
    head_a = lax.broadcasted_iota(I32, (1, LANES), 1) < HEAD_DIM
    sel_a = jnp.where(head_a, 1.0, 0.0).astype(BF16)
    sel_b = jnp.where(head_a, 0.0, 1.0).astype(BF16)

    def norm_rows(i, c):
        r0 = pl.multiple_of(i * ch, ch)
        for src, gref, dst in ((q_ref, qg_ref, qs), (k_ref, kg_ref, ks)):
            x = src[0, pl.ds(r0, ch), :].astype(F32)
            ssq = jnp.dot((x * x).astype(BF16), e_ref[...], preferred_element_type=F32)
            dst[pl.ds(r0, ch), :] = (x * lax.rsqrt(ssq * (1.0 / HEAD_DIM) + EPS) * gref[...]).astype(BF16)
        return c
    lax.fori_loop(0, l // ch, norm_rows, 0)

    def key_start(r):
        return jnp.clip(r - WIN_H // 2, 0, n_rows - WIN_H)

    def score_stage(r, slot):
        rs = key_start(r)
        q_r = qs[pl.ds(pl.multiple_of(r * GRID_W, GRID_W), GRID_W), :]
        qm = jnp.concatenate([q_r * sel_a, q_r * sel_b], axis=0)
        kb = ks[pl.ds(pl.multiple_of(rs * GRID_W, GRID_W), win_keys), :]
        s = lax.dot_general(qm, kb, (((1,), (1,)), ((), ())), preferred_element_type=F32)
        dr0 = rs - r + (WIN_H - 1)
        lane0 = pl.multiple_of((dr0 // 2) * LANES, LANES)
        s = s + bias_ref[0, dr0 % 2, :, pl.ds(lane0, win_keys)]
        s_scr[slot] = s
        m_scr[slot] = jnp.max(s, axis=-1, keepdims=True)

    def value_stage(r, slot):
        rs = key_start(r)
        vb = v_ref[0, pl.ds(pl.multiple_of(rs * GRID_W, GRID_W), win_keys), :]
        p = jnp.exp(s_scr[slot] - m_scr[slot])
        den = jnp.sum(p, axis=-1, keepdims=True)
        o = jnp.dot(p.astype(BF16), vb, preferred_element_type=F32) * (1.0 / den)
        out = jnp.where(head_a, o[:GRID_W], o[GRID_W:])
        o_ref[0, pl.ds(pl.multiple_of(r * GRID_W, GRID_W), GRID_W), :] = out.astype(BF16)

    score_stage(0, 0)

    def row_group(i, c):
        r = ATTN_ROW_UNROLL * i
        for j in range(ATTN_ROW_UNROLL):
            score_stage(jnp.minimum(r + j + 1, n_rows - 1), (j + 1) % 2)
            value_stage(r + j, j % 2)
        return c
    lax.fori_loop(0, n_rows // ATTN_ROW_UNROLL, row_group, 0)


def _attention(proj3, qg2, kg2, e_mat, bias):
    b, l, _ = proj3.shape
    n_pairs = N_HEADS_ATTN // 2
    blk = (1, l, LANES)
    return pl.pallas_call(
        _attn_body,
        grid=(n_pairs, b),
        in_specs=[pl.BlockSpec(blk, lambda hp, bi: (bi, 0, COL_Q + hp)),
                  pl.BlockSpec(blk, lambda hp, bi: (bi, 0, COL_K + hp)),
                  pl.BlockSpec(blk, lambda hp, bi: (bi, 0, COL_V + hp)),
                  pl.BlockSpec((1, LANES), lambda hp, bi: (0, 0)),
                  pl.BlockSpec((1, LANES), lambda hp, bi: (0, 0)),
                  pl.BlockSpec((LANES, LANES), lambda hp, bi: (0, 0)),
                  pl.BlockSpec((1,) + bias.shape[1:], lambda hp, bi: (hp, 0, 0, 0))],
        out_specs=pl.BlockSpec(blk, lambda hp, bi: (bi, 0, hp)),
        out_shape=jax.ShapeDtypeStruct((b, l, D_ATTN), BF16),
        scratch_shapes=[pltpu.VMEM((l, LANES), BF16), pltpu.VMEM((l, LANES), BF16),
                        pltpu.VMEM((2, LANES, WIN_H * GRID_W), F32), pltpu.VMEM((2, LANES, 1), F32)],
        compiler_params=_cparams(2),
        name="nbr_attention",
    )(proj3, proj3, proj3, qg2, kg2, e_mat, bias)


def _attention_bias(rpb):
    n_dr = 2 * WIN_H - 1
    cols = jnp.arange(GRID_W, dtype=I32)
    col_start = jnp.clip(cols - WIN_W // 2, 0, GRID_W - WIN_W)
    keys = cols[None, :]
    valid = (keys >= col_start[:, None]) & (keys < col_start[:, None] + WIN_W)
    dc = jnp.clip(keys - cols[:, None] + (WIN_W - 1), 0, 2 * WIN_W - 2)
    tab = jnp.where(valid[None, None], rpb.astype(F32)[:, :, dc], NEG)
    flat = jnp.transpose(tab, (0, 2, 1, 3)).reshape(N_HEADS_ATTN // 2, 2 * GRID_W, n_dr * GRID_W)
    moved = jnp.concatenate([flat[:, :, GRID_W:], jnp.full(flat.shape[:2] + (GRID_W,), NEG, F32)], axis=2)
    return jnp.stack([flat, moved], axis=1)


def _ssd_body(z_ref, xs_ref, b_ref, c_ref, dt_ref, cw_ref, cb_ref, dtb_ref, arow_ref, dsk_ref, ng_ref,
              ef_ref, eb_ref, tri_ref, shift_ref, o_ref, xs_s, b_s, c_s, y_s, sf_s, sb_s, x_f, pre_s):
    l = xs_ref.shape[1]
    n_chunks = l // CHUNK
    hpg = HEADS_PER_GROUP
    ii = lax.broadcasted_iota(I32, (CHUNK, CHUNK), 0)
    jj = lax.broadcasted_iota(I32, (CHUNK, CHUNK), 1)
    causal = ii >= jj
    anti = ii <= jj
    head_a = lax.broadcasted_iota(I32, (1, LANES), 1) < SSM_HEAD_DIM

    def conv_chunk(r0):
        p0 = pl.multiple_of(jnp.maximum(r0 - CONV_HALO, 0), CONV_HALO)
        n0 = pl.multiple_of(jnp.minimum(r0 + CHUNK, l - CONV_HALO), CONV_HALO)
        has_prev = r0 > 0
        has_next = r0 + CHUNK < l
        parts = []
        for ref in (xs_ref, b_ref, c_ref):
            prev = ref[0, pl.ds(p0, CONV_HALO), :]
            nxt = ref[0, pl.ds(n0, CONV_HALO), :]
            parts.append(jnp.concatenate([jnp.where(has_prev, prev, jnp.zeros_like(prev)),
                                          ref[0, pl.ds(r0, CHUNK), :],
                                          jnp.where(has_next, nxt, jnp.zeros_like(nxt))], axis=0))
        ext = jnp.concatenate(parts, axis=1)
        shifted = jnp.dot(shift_ref[...], ext, preferred_element_type=F32)
        mid = D_CONV // 2
        acc = cb_ref[0] + ext[CONV_HALO:CONV_HALO + CHUNK].astype(F32) * cw_ref[0, mid:mid + 1, :]
        for n, k in enumerate(k for k in range(D_CONV) if k != mid):
            acc = acc + shifted[n * CHUNK:(n + 1) * CHUNK] * cw_ref[0, k:k + 1, :]
        return _silu(acc)

    def dt_terms(r0):
        raw = dt_ref[0, pl.ds(r0, CHUNK), :] + dtb_ref[0]
        dtv = jnp.maximum(raw, 0.0) + jnp.log(1.0 + jnp.exp(-jnp.abs(raw)))
        adt = dtv * arow_ref[0]
        hi, mid, lo = _split3(adt)
        cs3 = jnp.dot(tri_ref[...], jnp.concatenate([hi, mid, lo], axis=1), preferred_element_type=F32)
        cs = cs3[:, :LANES] + cs3[:, LANES:2 * LANES] + cs3[:, 2 * LANES:]
        return dtv, cs[:CHUNK], cs[CHUNK:]

    def expand_exact(row, e_ref):
        hi, mid, lo = _split3(jnp.broadcast_to(row, (8, LANES)))
        e = e_ref[0]
        r = (jnp.dot(hi, e, preferred_element_type=F32) + jnp.dot(mid, e, preferred_element_type=F32)
             + jnp.dot(lo, e, preferred_element_type=F32))
        return r[0:1]

    def state_terms(cc, bt, x, dtv, cs, tot, e_ref, s_ref):
        e = e_ref[0]
        expcs = jnp.dot(jnp.exp(cs).astype(BF16), e, preferred_element_type=F32)
        y_off = jnp.dot(cc, s_ref[...].astype(BF16), preferred_element_type=F32) * expcs
        scl = jnp.dot((dtv * jnp.exp(tot - cs)).astype(BF16), e, preferred_element_type=F32)
        states_t = jnp.dot(bt, (x * scl).astype(BF16), preferred_element_type=F32)
        s_ref[...] = s_ref[...] * expand_exact(jnp.exp(tot), e_ref) + states_t
        return y_off

    sf_s[...] = jnp.zeros_like(sf_s)
    sb_s[...] = jnp.zeros_like(sb_s)

    def prep_stage(c, slot):
        r0 = pl.multiple_of(c * CHUNK, CHUNK)
        xbc = conv_chunk(r0)
        x = xbc[:, :D_GROUP]
        bm = xbc[:, D_GROUP:D_GROUP + D_STATE]
        xs_s[pl.ds(r0, CHUNK), :] = x.astype(BF16)
        b_s[pl.ds(r0, CHUNK), :] = bm.astype(BF16)
        c_s[pl.ds(r0, CHUNK), :] = xbc[:, D_GROUP + D_STATE:].astype(BF16)
        x_f[slot] = x
        dtv, cs_f, cs_b = dt_terms(r0)
        for n, a in enumerate((dtv, cs_f, cs_b, cs_f.T, cs_b.T, dtv.T, bm.T)):
            pre_s[slot, n] = a

    def mix_stage(c, slot):
        r0 = pl.multiple_of(c * CHUNK, CHUNK)
        x = x_f[slot]
        xb = xs_s[pl.ds(r0, CHUNK), :]
        bb = b_s[pl.ds(r0, CHUNK), :]
        cc = c_s[pl.ds(r0, CHUNK), :]
        dtv, cs_f, cs_b, cst_f, cst_b, dtt, bmt = (pre_s[slot, n] for n in range(7))
        cb = lax.dot_general(cc, bb, (((1,), (1,)), ((), ())), preferred_element_type=F32)
        pieces = []
        for pair in range(hpg // 2):
            xp = xb[:, pair * LANES:(pair + 1) * LANES]
            ys = []
            for hh in range(2):
                h = 2 * pair + hh
                df = cs_f[:, h:h + 1] - cst_f[h:h + 1, :]
                db = cs_b[:, hpg + h:hpg + h + 1] - cst_b[hpg + h:hpg + h + 1, :]
                lf = jnp.exp(jnp.where(causal, df, NEG)) * dtt[h:h + 1, :]
                lb = jnp.exp(jnp.where(anti, db, NEG)) * dtt[hpg + h:hpg + h + 1, :]
                m = (cb * (lf + lb)).astype(BF16)
                ys.append(jnp.dot(m, xp, preferred_element_type=F32))
            pieces.append(jnp.where(head_a, ys[0], ys[1]))
        y = jnp.concatenate(pieces, axis=1) + x * dsk_ref[0]
        y = y + state_terms(cc, bmt.astype(BF16), x, dtv, cs_f, cs_f[CHUNK - 1:CHUNK, :], ef_ref, sf_s)
        y_s[pl.ds(r0, CHUNK), :] = y

    prep_stage(0, 0)

    def fwd_group(i, carry):
        c = SSD_CHUNK_UNROLL * i
        for j in range(SSD_CHUNK_UNROLL):
            prep_stage(jnp.minimum(c + j + 1, n_chunks - 1), (j + 1) % 2)
            mix_stage(c + j, j % 2)
        return carry
    lax.fori_loop(0, n_chunks // SSD_CHUNK_UNROLL, fwd_group, 0)

    def bwd_prep_stage(c, slot):
        r0 = pl.multiple_of(c * CHUNK, CHUNK)
        dtv, _, cs_b = dt_terms(r0)
        pre_s[slot, 0] = dtv
        pre_s[slot, 2] = cs_b
        pre_s[slot, 6] = b_s[pl.ds(r0, CHUNK), :].astype(F32).T

    def bwd_mix_stage(c, slot):
        r0 = pl.multiple_of(c * CHUNK, CHUNK)
        x = xs_s[pl.ds(r0, CHUNK), :].astype(F32)
        cc = c_s[pl.ds(r0, CHUNK), :]
        dtv, cs_b, bt = pre_s[slot, 0], pre_s[slot, 2], pre_s[slot, 6].astype(BF16)
        y = y_s[pl.ds(r0, CHUNK), :] + state_terms(cc, bt, x, dtv, cs_b, cs_b[0:1, :], eb_ref, sb_s)
        y = y * _silu(z_ref[0, pl.ds(r0, CHUNK), :].astype(F32))
        ms = jnp.mean(y * y, axis=-1, keepdims=True)
        o_ref[0, pl.ds(r0, CHUNK), :] = (y * lax.rsqrt(ms + EPS) * ng_ref[0]).astype(BF16)

    bwd_prep_stage(n_chunks - 1, 0)

    def bwd_group(i, carry):
        c = n_chunks - 1 - SSD_CHUNK_UNROLL * i
        for j in range(SSD_CHUNK_UNROLL):
            bwd_prep_stage(jnp.maximum(c - j - 1, 0), (j + 1) % 2)
            bwd_mix_stage(c - j, j % 2)
        return carry
    lax.fori_loop(0, n_chunks // SSD_CHUNK_UNROLL, bwd_group, 0)


def _ssd(proj3, dt3, cw, cb, dtb, arow, dsk, ng, ef, eb, tri, shift):
    b, l, _ = proj3.shape
    g = N_GROUPS_SSM
    wconv = D_GROUP + 2 * D_STATE
    per_group = lambda shape: pl.BlockSpec((1,) + shape, lambda bi, gi: (gi,) + (0,) * len(shape))
    return pl.pallas_call(
        _ssd_body,
        grid=(b, g),
        in_specs=[pl.BlockSpec((1, l, D_GROUP), lambda bi, gi: (bi, 0, COL_Z + gi)),
                  pl.BlockSpec((1, l, D_GROUP), lambda bi, gi: (bi, 0, COL_XS + gi)),
                  pl.BlockSpec((1, l, D_STATE), lambda bi, gi: (bi, 0, COL_B + gi)),
                  pl.BlockSpec((1, l, D_STATE), lambda bi, gi: (bi, 0, COL_C + gi)),
                  pl.BlockSpec((1, l, LANES), lambda bi, gi: (bi, 0, gi)),
                  per_group((D_CONV, wconv)), per_group((1, wconv)),
                  per_group((1, LANES)), per_group((1, LANES)),
                  per_group((1, D_GROUP)), per_group((1, D_GROUP)),
                  pl.BlockSpec((1, LANES, D_GROUP), lambda bi, gi: (0, 0, 0)),
                  pl.BlockSpec((1, LANES, D_GROUP), lambda bi, gi: (0, 0, 0)),
                  pl.BlockSpec((2 * CHUNK, CHUNK), lambda bi, gi: (0, 0)),
                  pl.BlockSpec(shift.shape, lambda bi, gi: (0, 0))],
        out_specs=pl.BlockSpec((1, l, D_GROUP), lambda bi, gi: (bi, 0, gi)),
        out_shape=jax.ShapeDtypeStruct((b, l, D_SSM), BF16),
        scratch_shapes=[pltpu.VMEM((l, D_GROUP), BF16), pltpu.VMEM((l, D_STATE), BF16),
                        pltpu.VMEM((l, D_STATE), BF16), pltpu.VMEM((l, D_GROUP), F32),
                        pltpu.VMEM((D_STATE, D_GROUP), F32), pltpu.VMEM((D_STATE, D_GROUP), F32),
                        pltpu.VMEM((2, CHUNK, D_GROUP), F32), pltpu.VMEM((2, 7, CHUNK, LANES), F32)],
        compiler_params=_cparams(2),
        name="ssd_mixer",
    )(proj3, proj3, proj3, proj3, dt3, cw, cb, dtb, arow, dsk, ng, ef, eb, tri, shift)


def _route_rows(lg):
    lane = lax.broadcasted_iota(I32, lg.shape, 1).astype(F32)
    n_g = float(N_EXPERT_GROUPS)
    n_e = float(EXPERTS_PER_GROUP)
    gl = jnp.where(lane < n_g, lg, NEG)
    gmax = jnp.max(gl, axis=-1, keepdims=True)
    g_idx = jnp.min(jnp.where(gl == gmax, lane, float(LANES)), axis=-1, keepdims=True)
    g_p = 1.0 / jnp.sum(jnp.exp(gl - gmax), axis=-1, keepdims=True)
    first = n_g + n_e * g_idx
    el = jnp.where((lane >= first) & (lane < first + n_e), lg, NEG)
    e1 = jnp.max(el, axis=-1, keepdims=True)
    i1 = jnp.min(jnp.where(el == e1, lane, float(LANES)), axis=-1, keepdims=True)
    el2 = jnp.where(lane == i1, NEG, el)
    e2 = jnp.max(el2, axis=-1, keepdims=True)
    i2 = jnp.min(jnp.where(el2 == e2, lane, float(LANES)), axis=-1, keepdims=True)
    r = jnp.exp(e2 - e1)
    gate1 = g_p / (1.0 + r)
    gate2 = gate1 * r
    return jnp.where(lane == 0.0, i1 - n_g,
                     jnp.where(lane == 1.0, i2 - n_g,
                               jnp.where(lane == 2.0, gate1, jnp.where(lane == 3.0, gate2, 0.0))))


def _outproj_body(x_ref, a_ref, s_ref, ag_ref, wa_ref, ws_ref, g2_ref, wr_ref, br_ref, x2_ref, hp_ref, rt_ref):
    tm = x_ref.shape[0]
    sub = min(tm, OUTPROJ_SUB_ROWS)
    for r0 in range(0, tm, sub):
        rows = slice(r0, r0 + sub)
        a = a_ref[rows, :].astype(F32)
        ms = jnp.mean(a * a, axis=-1, keepdims=True)
        an = (a * lax.rsqrt(ms + EPS) * ag_ref[...]).astype(BF16)
        y = (jnp.dot(an, wa_ref[...], preferred_element_type=F32)
             + jnp.dot(s_ref[rows, :], ws_ref[...], preferred_element_type=F32))
        x2 = x_ref[rows, :] + y
        x2_ref[rows, :] = x2
        ms2 = jnp.mean(x2 * x2, axis=-1, keepdims=True)
        hn = x2 * lax.rsqrt(ms2 + EPS) * g2_ref[...]
        hi = hn.astype(BF16)
        lo = (hn - hi.astype(F32)).astype(BF16)
        l1 = jnp.dot(hi, wr_ref[...], preferred_element_type=F32)
        l2 = jnp.dot(lo, wr_ref[:, :LANES], preferred_element_type=F32)
        rt_ref[rows, :] = _route_rows(l1[:, :LANES] + l1[:, LANES:] + l2 + br_ref[...])
        packed = _pack_bf16_pairs(hi.astype(F32))
        for c in range(ROW_TILE):
            hp_ref[pl.ds(r0 * ROW_TILE + c, sub, stride=ROW_TILE), :] = packed[:, c * LANES:(c + 1) * LANES]


def _outproj(x2d, attn2d, ssm2d, ag, wa, ws, g2, wr, br):
    t, d = x2d.shape
    tm = min(2 * OUTPROJ_SUB_ROWS, t)
    row = lambda w: pl.BlockSpec((tm, w), lambda i: (i, 0))
    full = lambda a: pl.BlockSpec(a.shape, lambda i: (0,) * a.ndim)
    return pl.pallas_call(
        _outproj_body,
        grid=(t // tm,),
        in_specs=[row(d), row(D_ATTN), row(D_SSM), full(ag), full(wa), full(ws), full(g2), full(wr), full(br)],
        out_specs=[row(d), pl.BlockSpec((tm * ROW_TILE, LANES), lambda i: (i, 0)), row(LANES)],
        out_shape=[jax.ShapeDtypeStruct((t, d), F32), jax.ShapeDtypeStruct((t * ROW_TILE, LANES), U32),
                   jax.ShapeDtypeStruct((t, LANES), F32)],
        compiler_params=_cparams(1),
        name="outproj_router",
    )(x2d, attn2d, ssm2d, ag, wa, ws, g2, wr, br)


def _moe_body(vblk_ref, vexp_ref, vlo_ref, vhi_ref, tok_ref, tok_next_ref, slot_ref, slot_prev_ref, x_hbm, wg_ref,
              wu_ref, wd_ref, y_hbm, xbuf, ybuf, wgu_bf, wd_bf, cached_ref, gsem, ssem):
    blk = xbuf.shape[1] // ROW_TILE
    d_half = wg_ref.shape[1] // 2
    v = pl.program_id(0)
    lo = vlo_ref[v]
    hi = vhi_ref[v]
    s = vblk_ref[v]
    expert = vexp_ref[v]
    n_blocks = y_hbm.shape[0] // (blk * ROW_TILE) - 1
    spare_row0 = n_blocks * blk
    slot = s % 2
    other = 1 - slot

    def tile_rows(i):
        return pl.ds(pl.multiple_of(i * ROW_TILE, ROW_TILE), ROW_TILE)

    def start_gather(table_ref, buf_slot):
        def body(i, c):
            pltpu.make_async_copy(x_hbm.at[tile_rows(table_ref[0, 0, i])], xbuf.at[buf_slot, tile_rows(i)],
                                  gsem.at[buf_slot]).start()
            return c
        lax.fori_loop(0, blk, body, 0, unroll=MOE_DMA_UNROLL)

    def wait_gather(buf_slot):
        pltpu.make_async_copy(x_hbm.at[pl.ds(0, blk * ROW_TILE)], xbuf.at[buf_slot], gsem.at[buf_slot]).wait()

    def start_scatter(buf_slot):
        def body(i, c):
            pltpu.make_async_copy(ybuf.at[buf_slot, tile_rows(i)], y_hbm.at[tile_rows(slot_ref[0, 0, i])],
                                  ssem.at[buf_slot]).start()
            return c
        lax.fori_loop(0, blk, body, 0, unroll=MOE_DMA_UNROLL)

    def wait_scatter(sem_slot):
        pltpu.make_async_copy(ybuf.at[0], y_hbm.at[pl.ds(0, blk * ROW_TILE)], ssem.at[sem_slot]).wait()

    @pl.when(v == 0)
    def _():
        cached_ref[0] = -1
        ybuf[...] = jnp.zeros_like(ybuf)

    @pl.when(hi > lo)
    def _():
        @pl.when(lo == 0)
        def _():
            @pl.when(s == 0)
            def _():
                start_gather(tok_ref, slot)
            wait_gather(slot)

            @pl.when(s >= 2)
            def _():
                wait_scatter(slot)

        @pl.when(cached_ref[0] != expert)
        def _():
            rows = 256

            def cast_rows(i, c):
                r0 = pl.multiple_of(i * rows, rows)
                wgu_bf[pl.ds(r0, rows), :D_EXPERT] = wg_ref[0, pl.ds(r0, rows), :].astype(BF16)
                wgu_bf[pl.ds(r0, rows), D_EXPERT:] = wu_ref[0, pl.ds(r0, rows), :].astype(BF16)
                return c
            lax.fori_loop(0, 2 * d_half // rows, cast_rows, 0)
            wd_bf[...] = wd_ref[0].astype(BF16)
            cached_ref[0] = expert

        prefetch = (lo == 0) & (s + 1 < n_blocks)
        dst = jnp.where(prefetch, other, MOE_SPARE_SLOT)
        drain = (lo == 0) & (s > 0)
        dsem = jnp.where(drain, other, MOE_SPARE_SLOT)

        def issue_rows(first, count):
            for i in range(first, first + count):
                pltpu.make_async_copy(x_hbm.at[tile_rows(tok_next_ref[0, 0, i])],
                                      xbuf.at[dst, pl.ds(i * ROW_TILE, ROW_TILE)], gsem.at[dst]).start()
                out_row = jnp.where(drain, slot_prev_ref[0, 0, i], spare_row0 + i)
                pltpu.make_async_copy(ybuf.at[other, pl.ds(i * ROW_TILE, ROW_TILE)], y_hbm.at[tile_rows(out_row)],
                                      ssem.at[dsem]).start()

        words = _load_tile_rows(xbuf, (slot,), blk)
        xlo = jnp.concatenate([_unpack_lo(w).astype(BF16) for w in words], axis=1)
        xhi = jnp.concatenate([_unpack_hi(w).astype(BF16) for w in words], axis=1)
        quarter = blk // 4
        gate_up = []
        for j in range(2):
            issue_rows(j * quarter, quarter)
            cols = slice(j * D_EXPERT, (j + 1) * D_EXPERT)
            gate_up.append(jnp.dot(xlo, wgu_bf[:d_half, cols], preferred_element_type=F32)
                           + jnp.dot(xhi, wgu_bf[d_half:, cols], preferred_element_type=F32))
        hid = (_silu(gate_up[0]) * gate_up[1]).astype(BF16)
        ow = wd_bf.shape[1] // MOE_COL_CHUNKS
        per_chunk = 2 * quarter // MOE_COL_CHUNKS
        outs = []
        for j in range(MOE_COL_CHUNKS):
            issue_rows(2 * quarter + j * per_chunk, per_chunk)
            outs.append(jnp.dot(hid, wd_bf[:, j * ow:(j + 1) * ow], preferred_element_type=F32))
        y = _pack_bf16_pairs(jnp.concatenate(outs, axis=1))

        @pl.when(jnp.logical_not(prefetch))
        def _():
            wait_gather(MOE_SPARE_SLOT)

        @pl.when(jnp.logical_not(drain))
        def _():
            wait_scatter(MOE_SPARE_SLOT)

        @pl.when(lo == 0)
        def _():
            _store_tile_rows(ybuf, (slot,), y)

        @pl.when(lo > 0)
        def _():
            row = lax.broadcasted_iota(I32, (blk, 1), 0)
            old = jnp.concatenate(_load_tile_rows(ybuf, (slot,), blk), axis=1)
            _store_tile_rows(ybuf, (slot,), jnp.where((row >= lo) & (row < hi), y, old))

        @pl.when((hi == blk) & (s == n_blocks - 1))
        def _():
            if n_blocks > 1:
                wait_scatter(other)
            start_scatter(slot)
            wait_scatter(slot)


def _moe(visits, row_tok, row_slot, hn_tiles, w_gate, w_up, w_down):
    blk = MOE_BLOCK
    n_assign = row_tok.shape[0]
    n_blocks = n_assign // blk
    n_visits = visits[0].shape[0]
    d = w_gate.shape[1]
    smem_rows = pl.BlockSpec((1, 1, blk), lambda v, vb, ve, vl, vh: (vb[v], 0, 0), memory_space=pltpu.SMEM)
    smem_next = pl.BlockSpec((1, 1, blk), lambda v, vb, ve, vl, vh: (jnp.minimum(vb[v] + 1, n_blocks - 1), 0, 0),
                             memory_space=pltpu.SMEM)
    smem_prev = pl.BlockSpec((1, 1, blk), lambda v, vb, ve, vl, vh: (jnp.maximum(vb[v] - 1, 0), 0, 0),
                             memory_space=pltpu.SMEM)
    by_expert = lambda shape: pl.BlockSpec((1,) + shape, lambda v, vb, ve, vl, vh: (ve[v], 0, 0))
    grid_spec = pltpu.PrefetchScalarGridSpec(
        num_scalar_prefetch=4,
        grid=(n_visits,),
        in_specs=[smem_rows, smem_next, smem_rows, smem_prev,
                  pl.BlockSpec(memory_space=pl.ANY),
                  by_expert((d, D_EXPERT)), by_expert((d, D_EXPERT)), by_expert((D_EXPERT, d))],
        out_specs=pl.BlockSpec(memory_space=pl.ANY),
        scratch_shapes=[pltpu.VMEM((MOE_SPARE_SLOT + 1, blk * ROW_TILE, LANES), U32),
                        pltpu.VMEM((2, blk * ROW_TILE, LANES), U32),
                        pltpu.VMEM((d, 2 * D_EXPERT), BF16), pltpu.VMEM((D_EXPERT, d), BF16),
                        pltpu.SMEM((1,), I32),
                        pltpu.SemaphoreType.DMA((MOE_SPARE_SLOT + 1,)),
                        pltpu.SemaphoreType.DMA((MOE_SPARE_SLOT + 1,))],
    )
    tok3 = row_tok.reshape(n_blocks, 1, blk)
    slot3 = row_slot.reshape(n_blocks, 1, blk)
    return pl.pallas_call(
        _moe_body,
        grid_spec=grid_spec,
        out_shape=jax.ShapeDtypeStruct(((n_assign + blk) * ROW_TILE, LANES), U32),
        compiler_params=_cparams(1),
        name="moe_experts",
    )(*visits, tok3, tok3, slot3, slot3, hn_tiles, w_gate, w_up, w_down)


def _route_tables(route, t):
    blk = MOE_BLOCK
    n_assign = t * TOP_K
    n_blocks = n_assign // blk
    flat_e = route[:, :TOP_K].astype(I32).T.reshape(n_assign)
    order = jnp.argsort(flat_e, stable=True).astype(I32)
    counts = jnp.sum((flat_e[:, None] == jnp.arange(N_EXPERTS, dtype=I32)[None, :]).astype(I32), axis=0)
    ends = jnp.cumsum(counts)
    starts = ends - counts
    cuts = jnp.sort(jnp.concatenate([jnp.arange(n_blocks, dtype=I32) * blk, starts[1:]]))
    nxt = jnp.concatenate([cuts[1:], jnp.full((1,), n_assign, I32)])
    vblk = jnp.minimum(cuts // blk, n_blocks - 1)
    vexp = jnp.minimum(jnp.sum((ends[None, :] <= cuts[:, None]).astype(I32), axis=1), N_EXPERTS - 1)
    vlo = cuts - vblk * blk
    vhi = jnp.maximum(jnp.minimum(nxt, (vblk + 1) * blk) - vblk * blk, vlo)
    return (vblk, vexp, vlo, vhi), order % t, order


def _combine_body(x2_ref, y0_ref, y1_ref, rt_ref, o_ref):
    tm = o_ref.shape[0]
    half = o_ref.shape[1] // 2
    g0 = rt_ref[:, TOP_K:TOP_K + 1]
    g1 = rt_ref[:, TOP_K + 1:TOP_K + 2]
    y0 = _load_tile_rows(y0_ref, (), tm)
    y1 = _load_tile_rows(y1_ref, (), tm)
    for c in range(ROW_TILE):
        lo_cols = slice(c * LANES, (c + 1) * LANES)
        hi_cols = slice(half + c * LANES, half + (c + 1) * LANES)
        o_ref[:, lo_cols] = x2_ref[:, lo_cols] + g0 * _unpack_lo(y0[c]) + g1 * _unpack_lo(y1[c])
        o_ref[:, hi_cols] = x2_ref[:, hi_cols] + g0 * _unpack_hi(y0[c]) + g1 * _unpack_hi(y1[c])


def _combine(x2, y_slots, route):
    t, d = x2.shape
    tm = min(512, t)
    nb = t // tm
    return pl.pallas_call(
        _combine_body,
        grid=(nb,),
        in_specs=[pl.BlockSpec((tm, d), lambda i: (i, 0)),
                  pl.BlockSpec((tm * ROW_TILE, LANES), lambda i: (i, 0)),
                  pl.BlockSpec((tm * ROW_TILE, LANES), lambda i: (i + nb, 0)),
                  pl.BlockSpec((tm, LANES), lambda i: (i, 0))],
        out_specs=pl.BlockSpec((tm, d), lambda i: (i, 0)),
        out_shape=jax.ShapeDtypeStruct((t, d), F32),
        compiler_params=_cparams(1),
        name="moe_combine",
    )(x2, y_slots, y_slots, route)


def _prepare(norm1_g, w_in, q_norm_g, k_norm_g, rpb, attn_out_g, conv_w, conv_b, a_log_f, a_log_b,
             dt_bias_f, dt_bias_b, d_skip, ssm_norm_g, w_out, norm2_g, w_router_group, b_router_group,
             w_router_expert, b_router_expert, w_gate, w_up, w_down):
    d = w_in.shape[0]
    hpg = HEADS_PER_GROUP
    p = {}
    p["norm1_g"] = norm1_g.reshape(1, d).astype(F32)
    p["w_main"] = w_in[:, :D_PROJ_MAIN].astype(BF16)
    w_dt = w_in[:, D_PROJ_MAIN:]
    zeros = jnp.zeros((d, LANES - 2 * hpg), w_in.dtype)
    per_group = lambda v, g: v[..., g * hpg:(g + 1) * hpg]
    p["w_dt"] = jnp.concatenate(
        [jnp.concatenate([per_group(w_dt[:, :N_HEADS_SSM], g), per_group(w_dt[:, N_HEADS_SSM:], g), zeros], axis=1)
         for g in range(N_GROUPS_SSM)], axis=1).astype(BF16)
    lane_rows = lambda f, bwd: jnp.stack(
        [jnp.concatenate([per_group(f, g), per_group(bwd, g), jnp.zeros((LANES - 2 * hpg,), F32)])
         for g in range(N_GROUPS_SSM)])[:, None, :]
    p["dtb"] = lane_rows(dt_bias_f.astype(F32), dt_bias_b.astype(F32))
    p["arow"] = lane_rows(-jnp.exp(a_log_f.astype(F32)), -jnp.exp(a_log_b.astype(F32)))
    scale = HEAD_DIM ** -0.5
    p["qg2"] = (jnp.tile(q_norm_g.astype(F32), 2) * scale).reshape(1, LANES)
    p["kg2"] = jnp.tile(k_norm_g.astype(F32), 2).reshape(1, LANES)
    lane = jnp.arange(LANES)
    p["e_mat"] = (lane[:, None] // HEAD_DIM == lane[None, :] // HEAD_DIM).astype(BF16)
    p["bias"] = _attention_bias(rpb)
    p["attn_out_g"] = attn_out_g.reshape(1, D_ATTN).astype(F32)
    cw = conv_w.reshape(D_CONV, -1).astype(F32)
    cbias = conv_b.reshape(1, -1).astype(F32)
    group_cols = lambda a, g: jnp.concatenate(
        [a[:, g * D_GROUP:(g + 1) * D_GROUP],
         a[:, D_SSM + g * D_STATE:D_SSM + (g + 1) * D_STATE],
         a[:, D_SSM + D_BC + g * D_STATE:D_SSM + D_BC + (g + 1) * D_STATE]], axis=1)
    p["cw"] = jnp.stack([group_cols(cw, g) for g in range(N_GROUPS_SSM)])
    p["cb"] = jnp.stack([group_cols(cbias, g) for g in range(N_GROUPS_SSM)])
    p["dsk"] = jnp.repeat(d_skip.astype(F32), SSM_HEAD_DIM).reshape(N_GROUPS_SSM, 1, D_GROUP)
    p["ng"] = ssm_norm_g.astype(F32).reshape(N_GROUPS_SSM, 1, D_GROUP)
    col_head = jnp.arange(D_GROUP) // SSM_HEAD_DIM
    p["ef"] = (lane[:, None] == col_head[None, :]).astype(BF16)[None]
    p["eb"] = (lane[:, None] == col_head[None, :] + hpg).astype(BF16)[None]
    pos = jnp.arange(CHUNK)
    p["tri"] = jnp.concatenate([pos[:, None] >= pos[None, :], pos[:, None] <= pos[None, :]], axis=0).astype(BF16)
    src = jnp.arange(CHUNK + 2 * CONV_HALO)
    p["shift"] = jnp.concatenate(
        [src[None, :] == pos[:, None] + (CONV_HALO - D_CONV // 2 + k) for k in range(D_CONV) if k != D_CONV // 2],
        axis=0).astype(BF16)
    p["wa"] = w_out[:D_ATTN].astype(BF16)
    p["ws"] = w_out[D_ATTN:].astype(BF16)
    p["norm2_g"] = norm2_g.reshape(1, d).astype(F32)
    n_r = N_EXPERT_GROUPS + N_EXPERTS
    wr = jnp.concatenate([w_router_group, w_router_expert, jnp.zeros((d, LANES - n_r), F32)], axis=1).astype(F32)
    wr_hi = wr.astype(BF16)
    wr_lo = (wr - wr_hi.astype(F32)).astype(BF16)
    p["wr"] = jnp.concatenate([wr_hi, wr_lo], axis=1)
    p["br"] = jnp.concatenate([b_router_group, b_router_expert, jnp.zeros((LANES - n_r,), F32)]).reshape(1, LANES)
    p["w_gate"], p["w_up"], p["w_down"] = w_gate, w_up, w_down
    return p


def _layer(x, p):
    b, l, d = x.shape
    t = b * l
    x2d = x.reshape(t, d)
    proj, dt = _inproj(x2d, p["norm1_g"], p["w_main"], p["w_dt"])
    proj3 = proj.reshape(b, l, D_PROJ_MAIN)
    attn = _attention(proj3, p["qg2"], p["kg2"], p["e_mat"], p["bias"])
    ssm = _ssd(proj3, dt.reshape(b, l, N_GROUPS_SSM * LANES), p["cw"], p["cb"], p["dtb"], p["arow"],
               p["dsk"], p["ng"], p["ef"], p["eb"], p["tri"], p["shift"])
    x2, hn_packed, route = _outproj(x2d, attn.reshape(t, D_ATTN), ssm.reshape(t, D_SSM), p["attn_out_g"],
                                    p["wa"], p["ws"], p["norm2_g"], p["wr"], p["br"])
    visits, row_tok, row_slot = _route_tables(route, t)
    y_slots = _moe(visits, row_tok, row_slot, hn_packed, p["w_gate"], p["w_up"], p["w_down"])
    return _combine(x2, y_slots, route).reshape(b, l, d)


def kernel(x_prompt, x_sample, norm1_g, w_in, q_norm_g, k_norm_g, rpb, attn_out_g, conv_w, conv_b, a_log_f,
           a_log_b, dt_bias_f, dt_bias_b, d_skip, ssm_norm_g, w_out, norm2_g, w_router_group, b_router_group,
           w_router_expert, b_router_expert, w_gate, w_up, w_down):
    weights = (norm1_g, w_in, q_norm_g, k_norm_g, rpb, attn_out_g, conv_w, conv_b, a_log_f, a_log_b, dt_bias_f,
               dt_bias_b, d_skip, ssm_norm_g, w_out, norm2_g, w_router_group, b_router_group, w_router_expert,
               b_router_expert, w_gate, w_up, w_down)
    assert all(w.shape[0] == 1 for w in weights), "one layer of stacked weights expected"
    p = _prepare(*(w[0] for w in weights))
    return (_layer(x_prompt, p), _layer(x_sample, p))
```

```python
import jax
import jax.numpy as jnp
from jax import lax
from jax.experimental import pallas as pl
from jax.experimental.pallas import tpu as pltpu

F32 = jnp.float32
BF16 = jnp.bfloat16
U32 = jnp.uint32
I32 = jnp.int32

EPS = 1e-6
GRID_W = 64
N_HEADS_ATTN = 16
HEAD_DIM = 64
D_ATTN = N_HEADS_ATTN * HEAD_DIM
WIN_H = 8
WIN_W = 16
N_HEADS_SSM = 16
SSM_HEAD_DIM = 64
D_SSM = N_HEADS_SSM * SSM_HEAD_DIM
N_GROUPS_SSM = 2
HEADS_PER_GROUP = N_HEADS_SSM // N_GROUPS_SSM
D_GROUP = D_SSM // N_GROUPS_SSM
D_STATE = 128
D_CONV = 5
CHUNK = 128
D_BC = N_GROUPS_SSM * D_STATE
N_EXPERT_GROUPS = 4
EXPERTS_PER_GROUP = 8
N_EXPERTS = N_EXPERT_GROUPS * EXPERTS_PER_GROUP
TOP_K = 2
D_EXPERT = 512

LANES = 128
BF16_ROWS = 16
CONV_HALO = BF16_ROWS
NEG = -1e30
VMEM_LIMIT_BYTES = 56 * 1024 * 1024
MOE_BLOCK = 256
ATTN_ROW_UNROLL = 8
SSD_CHUNK_UNROLL = 4
MOE_DMA_UNROLL = 8
MOE_SPARE_SLOT = 2
MOE_COL_CHUNKS = 4
ROW_TILE = 8
OUTPROJ_SUB_ROWS = 256
INPROJ_ROWS = 512
INPROJ_COLS = 2816

COL_Q, COL_K, COL_V = 0, D_ATTN // LANES, 2 * D_ATTN // LANES
COL_Z = 3 * D_ATTN // D_GROUP
COL_XS = (3 * D_ATTN + D_SSM) // D_GROUP
COL_B = (3 * D_ATTN + 2 * D_SSM) // LANES
COL_C = COL_B + D_BC // LANES
D_PROJ_MAIN = 3 * D_ATTN + 2 * D_SSM + 2 * D_BC


def _cparams(n_axes):
    return pltpu.CompilerParams(dimension_semantics=("arbitrary",) * n_axes,
                                vmem_limit_bytes=VMEM_LIMIT_BYTES)


def _silu(x):
    return x * (0.5 * jnp.tanh(0.5 * x) + 0.5)


def _split3(x):
    hi = x.astype(BF16)
    r1 = x - hi.astype(F32)
    mid = r1.astype(BF16)
    lo = (r1 - mid.astype(F32)).astype(BF16)
    return hi, mid, lo


def _pack_bf16_pairs(x):
    n = x.shape[1] // 2
    u = lax.bitcast_convert_type(x.astype(BF16).astype(F32), U32)
    return (u[:, :n] >> 16) | u[:, n:]


def _unpack_lo(u):
    return lax.bitcast_convert_type(u << 16, F32)


def _unpack_hi(u):
    return lax.bitcast_convert_type(u & jnp.uint32(0xFFFF0000), F32)


def _store_tile_rows(ref, lead, packed):
    m = packed.shape[0]
    for c in range(ROW_TILE):
        ref[lead + (pl.ds(c, m, stride=ROW_TILE), slice(None))] = packed[:, c * LANES:(c + 1) * LANES]


def _load_tile_rows(ref, lead, m):
    return [ref[lead + (pl.ds(c, m, stride=ROW_TILE), slice(None))] for c in range(ROW_TILE)]


def _inproj_body(x_ref, g_ref, w_ref, wdt_ref, o_ref, dt_ref, hn_ref):
    tm = x_ref.shape[0]
    rows = min(tm, 256)

    @pl.when(pl.program_id(1) == 0)
    def _():
        def norm_rows(i, c):
            r0 = pl.multiple_of(i * rows, rows)
            x = x_ref[pl.ds(r0, rows), :]
            ms = jnp.mean(x * x, axis=-1, keepdims=True)
            hn_ref[pl.ds(r0, rows), :] = (x * lax.rsqrt(ms + EPS) * g_ref[...]).astype(BF16)
            return c
        lax.fori_loop(0, tm // rows, norm_rows, 0)
        dt_ref[...] = jnp.dot(hn_ref[...], wdt_ref[...], preferred_element_type=F32)

    o_ref[...] = jnp.dot(hn_ref[...], w_ref[...], preferred_element_type=F32).astype(BF16)


def _inproj(x2d, gain, w_main, w_dt):
    t, d = x2d.shape
    n = w_main.shape[1]
    ndt = w_dt.shape[1]
    tm = min(INPROJ_ROWS, t)
    tn = n
    return pl.pallas_call(
        _inproj_body,
        grid=(t // tm, n // tn),
        in_specs=[pl.BlockSpec((tm, d), lambda i, j: (i, 0)),
                  pl.BlockSpec((1, d), lambda i, j: (0, 0)),
                  pl.BlockSpec((d, tn), lambda i, j: (0, j), pipeline_mode=pl.Buffered(1)),
                  pl.BlockSpec((d, ndt), lambda i, j: (0, 0))],
        out_specs=[pl.BlockSpec((tm, tn), lambda i, j: (i, j)),
                   pl.BlockSpec((tm, ndt), lambda i, j: (i, 0))],
        out_shape=[jax.ShapeDtypeStruct((t, n), BF16), jax.ShapeDtypeStruct((t, ndt), F32)],
        scratch_shapes=[pltpu.VMEM((tm, d), BF16)],
        compiler_params=_cparams(2),
        name="inproj",
    )(x2d, gain, w_main, w_dt)


def _attn_body(q_ref, k_ref, v_ref, qg_ref, kg_ref, e_ref, bias_ref, o_ref, qs, ks, s_scr, m_scr):
    l = q_ref.shape[1]
    n_rows = l // GRID_W
    win_keys = WIN_H * GRID_W
    ch = min(l, 512)
    head_a = lax.broadcasted_iota(I32, (1, LANES), 1) < HEAD_DIM
    sel_a = jnp.where(head_a, 1.0, 0.0).astype(BF16)
    sel_b = jnp.where(head_a, 0.0, 1.0).astype(BF16)

    def norm_rows(i, c):
        r0 = pl.multiple_of(i * ch, ch)
        for src, gref, dst in ((q_ref, qg_ref, qs), (k_ref, kg_ref, ks)):
            x = src[0, pl.ds(r0, ch), :].astype(F32)
            ssq = jnp.dot((x * x).astype(BF16), e_ref[...], preferred_element_type=F32)
            dst[pl.ds(r0, ch), :] = (x * lax.rsqrt(ssq * (1.0 / HEAD_DIM) + EPS) * gref[...]).astype(BF16)
        return c
    lax.fori_loop(0, l // ch, norm_rows, 0)

    def key_start(r):
        return jnp.clip(r - WIN_H // 2, 0, n_rows - WIN_H)

    def score_stage(r, slot):
        rs = key_start(r)
        q_r = qs[pl.ds(pl.multiple_of(r * GRID_W, GRID_W), GRID_W), :]
        qm = jnp.concatenate([q_r * sel_a, q_r * sel_b], axis=0)
        kb = ks[pl.ds(pl.multiple_of(rs * GRID_W, GRID_W), win_keys), :]
        s = lax.dot_general(qm, kb, (((1,), (1,)), ((), ())), preferred_element_type=F32)
        dr0 = rs - r + (WIN_H - 1)
        lane0 = pl.multiple_of((dr0 // 2) * LANES, LANES)
        s = s + bias_ref[0, dr0 % 2, :, pl.ds(lane0, win_keys)]
        s_scr[slot] = s
        m_scr[slot] = jnp.max(s, axis=-1, keepdims=True)

    def value_stage(r, slot):
        rs = key_start(r)
        vb = v_ref[0, pl.ds(pl.multiple_of(rs * GRID_W, GRID_W), win_keys), :]
        p = jnp.exp(s_scr[slot] - m_scr[slot])
        den = jnp.sum(p, axis=-1, keepdims=True)
        o = jnp.dot(p.astype(BF16), vb, preferred_element_type=F32) * (1.0 / den)
        out = jnp.where(head_a, o[:GRID_W], o[GRID_W:])
        o_ref[0, pl.ds(pl.multiple_of(r * GRID_W, GRID_W), GRID_W), :] = out.astype(BF16)

    score_stage(0, 0)

    def row_group(i, c):
        r = ATTN_ROW_UNROLL * i
        for j in range(ATTN_ROW_UNROLL):
            score_stage(jnp.minimum(r + j + 1, n_rows - 1), (j + 1) % 2)
            value_stage(r + j, j % 2)
        return c
    lax.fori_loop(0, n_rows // ATTN_ROW_UNROLL, row_group, 0)


def _attention(proj3, qg2, kg2, e_mat, bias):
    b, l, _ = proj3.shape
    n_pairs = N_HEADS_ATTN // 2
    blk = (1, l, LANES)
    return pl.pallas_call(
        _attn_body,
        grid=(n_pairs, b),
        in_specs=[pl.BlockSpec(blk, lambda hp, bi: (bi, 0, COL_Q + hp)),
                  pl.BlockSpec(blk, lambda hp, bi: (bi, 0, COL_K + hp)),
                  pl.BlockSpec(blk, lambda hp, bi: (bi, 0, COL_V + hp)),
                  pl.BlockSpec((1, LANES), lambda hp, bi: (0, 0)),
                  pl.BlockSpec((1, LANES), lambda hp, bi: (0, 0)),
                  pl.BlockSpec((LANES, LANES), lambda hp, bi: (0, 0)),
                  pl.BlockSpec((1,) + bias.shape[1:], lambda hp, bi: (hp, 0, 0, 0))],
        out_specs=pl.BlockSpec(blk, lambda hp, bi: (bi, 0, hp)),
        out_shape=jax.ShapeDtypeStruct((b, l, D_ATTN), BF16),
        scratch_shapes=[pltpu.VMEM((l, LANES), BF16), pltpu.VMEM((l, LANES), BF16),
                        pltpu.VMEM((2, LANES, WIN_H * GRID_W), F32), pltpu.VMEM((2, LANES, 1), F32)],
        compiler_params=_cparams(2),
        name="nbr_attention",
    )(proj3, proj3, proj3, qg2, kg2, e_mat, bias)


def _attention_bias(rpb):
    n_dr = 2 * WIN_H - 1
    cols = jnp.arange(GRID_W, dtype=I32)
    col_start = jnp.clip(cols - WIN_W // 2, 0, GRID_W - WIN_W)
    keys = cols[None, :]
    valid = (keys >= col_start[:, None]) & (keys < col_start[:, None] + WIN_W)
    dc = jnp.clip(keys - cols[:, None] + (WIN_W - 1), 0, 2 * WIN_W - 2)
    tab = jnp.where(valid[None, None], rpb.astype(F32)[:, :, dc], NEG)
    flat = jnp.transpose(tab, (0, 2, 1, 3)).reshape(N_HEADS_ATTN // 2, 2 * GRID_W, n_dr * GRID_W)
    moved = jnp.concatenate([flat[:, :, GRID_W:], jnp.full(flat.shape[:2] + (GRID_W,), NEG, F32)], axis=2)
    return jnp.stack([flat, moved], axis=1)


def _ssd_body(z_ref, xs_ref, b_ref, c_ref, dt_ref, cw_ref, cb_ref, dtb_ref, arow_ref, dsk_ref, ng_ref,
              ef_ref, eb_ref, tri_ref, shift_ref, o_ref, xs_s, b_s, c_s, y_s, sf_s, sb_s, x_f, pre_s):
    l = xs_ref.shape[1]
    n_chunks = l // CHUNK
    hpg = HEADS_PER_GROUP
    ii = lax.broadcasted_iota(I32, (CHUNK, CHUNK), 0)
    jj = lax.broadcasted_iota(I32, (CHUNK, CHUNK), 1)
    causal = ii >= jj
    anti = ii <= jj
    head_a = lax.broadcasted_iota(I32, (1, LANES), 1) < SSM_HEAD_DIM

    def conv_chunk(r0):
        p0 = pl.multiple_of(jnp.maximum(r0 - CONV_HALO, 0), CONV_HALO)
        n0 = pl.multiple_of(jnp.minimum(r0 + CHUNK, l - CONV_HALO), CONV_HALO)
        has_prev = r0 > 0
        has_next = r0 + CHUNK < l
        parts = []
        for ref in (xs_ref, b_ref, c_ref):
            prev = ref[0, pl.ds(p0, CONV_HALO), :]
            nxt = ref[0, pl.ds(n0, CONV_HALO), :]
            parts.append(jnp.concatenate([jnp.where(has_prev, prev, jnp.zeros_like(prev)),
                                          ref[0, pl.ds(r0, CHUNK), :],
                                          jnp.where(has_next, nxt, jnp.zeros_like(nxt))], axis=0))
        ext = jnp.concatenate(parts, axis=1)
        shifted = jnp.dot(shift_ref[...], ext, preferred_element_type=F32)
        mid = D_CONV // 2
        acc = cb_ref[0] + ext[CONV_HALO:CONV_HALO + CHUNK].astype(F32) * cw_ref[0, mid:mid + 1, :]
        for n, k in enumerate(k for k in range(D_CONV) if k != mid):
            acc = acc + shifted[n * CHUNK:(n + 1) * CHUNK] * cw_ref[0, k:k + 1, :]
        return _silu(acc)

    def dt_terms(r0):
        raw = dt_ref[0, pl.ds(r0, CHUNK), :] + dtb_ref[0]
        dtv = jnp.maximum(raw, 0.0) + jnp.log(1.0 + jnp.exp(-jnp.abs(raw)))
        adt = dtv * arow_ref[0]
        hi, mid, lo = _split3(adt)
        cs3 = jnp.dot(tri_ref[...], jnp.concatenate([hi, mid, lo], axis=1), preferred_element_type=F32)
        cs = cs3[:, :LANES] + cs3[:, LANES:2 * LANES] + cs3[:, 2 * LANES:]
        return dtv, cs[:CHUNK], cs[CHUNK:]

    def expand_exact(row, e_ref):
        hi, mid, lo = _split3(jnp.broadcast_to(row, (8, LANES)))
        e = e_ref[0]
        r = (jnp.dot(hi, e, preferred_element_type=F32) + jnp.dot(mid, e, preferred_element_type=F32)
             + jnp.dot(lo, e, preferred_element_type=F32))
        return r[0:1]

    def state_terms(cc, bt, x, dtv, cs, tot, e_ref, s_ref):
        e = e_ref[0]
        expcs = jnp.dot(jnp.exp(cs).astype(BF16), e, preferred_element_type=F32)
        y_off = jnp.dot(cc, s_ref[...].astype(BF16), preferred_element_type=F32) * expcs
        scl = jnp.dot((dtv * jnp.exp(tot - cs)).astype(BF16), e, preferred_element_type=F32)
        states_t = jnp.dot(bt, (x * scl).astype(BF16), preferred_element_type=F32)
        s_ref[...] = s_ref[...] * expand_exact(jnp.exp(tot), e_ref) + states_t
        return y_off

    sf_s[...] = jnp.zeros_like(sf_s)
    sb_s[...] = jnp.zeros_like(sb_s)

    def prep_stage(c, slot):
        r0 = pl.multiple_of(c * CHUNK, CHUNK)
        xbc = conv_chunk(r0)
        x = xbc[:, :D_GROUP]
        bm = xbc[:, D_GROUP:D_GROUP + D_STATE]
        xs_s[pl.ds(r0, CHUNK), :] = x.astype(BF16)
        b_s[pl.ds(r0, CHUNK), :] = bm.astype(BF16)
        c_s[pl.ds(r0, CHUNK), :] = xbc[:, D_GROUP + D_STATE:].astype(BF16)
        x_f[slot] = x
        dtv, cs_f, cs_b = dt_terms(r0)
        for n, a in enumerate((dtv, cs_f, cs_b, cs_f.T, cs_b.T, dtv.T, bm.T)):
            pre_s[slot, n] = a

    def mix_stage(c, slot):
        r0 = pl.multiple_of(c * CHUNK, CHUNK)
        x = x_f[slot]
        xb = xs_s[pl.ds(r0, CHUNK), :]
        bb = b_s[pl.ds(r0, CHUNK), :]
        cc = c_s[pl.ds(r0, CHUNK), :]
        dtv, cs_f, cs_b, cst_f, cst_b, dtt, bmt = (pre_s[slot, n] for n in range(7))
        cb = lax.dot_general(cc, bb, (((1,), (1,)), ((), ())), preferred_element_type=F32)
        pieces = []
        for pair in range(hpg // 2):
            xp = xb[:, pair * LANES:(pair + 1) * LANES]
            ys = []
            for hh in range(2):
                h = 2 * pair + hh
                df = cs_f[:, h:h + 1] - cst_f[h:h + 1, :]
                db = cs_b[:, hpg + h:hpg + h + 1] - cst_b[hpg + h:hpg + h + 1, :]
                lf = jnp.exp(jnp.where(causal, df, NEG)) * dtt[h:h + 1, :]
                lb = jnp.exp(jnp.where(anti, db, NEG)) * dtt[hpg + h:hpg + h + 1, :]
                m = (cb * (lf + lb)).astype(BF16)
                ys.append(jnp.dot(m, xp, preferred_element_type=F32))
            pieces.append(jnp.where(head_a, ys[0], ys[1]))
        y = jnp.concatenate(pieces, axis=1) + x * dsk_ref[0]
        y = y + state_terms(cc, bmt.astype(BF16), x, dtv, cs_f, cs_f[CHUNK - 1:CHUNK, :], ef_ref, sf_s)
        y_s[pl.ds(r0, CHUNK), :] = y

    prep_stage(0, 0)

    def fwd_group(i, carry):
        c = SSD_CHUNK_UNROLL * i
        for j in range(SSD_CHUNK_UNROLL):
            prep_stage(jnp.minimum(c + j + 1, n_chunks - 1), (j + 1) % 2)
            mix_stage(c + j, j % 2)
        return carry
    lax.fori_loop(0, n_chunks // SSD_CHUNK_UNROLL, fwd_group, 0)

    def bwd_prep_stage(c, slot):
        r0 = pl.multiple_of(c * CHUNK, CHUNK)
        dtv, _, cs_b = dt_terms(r0)
        pre_s[slot, 0] = dtv
        pre_s[slot, 2] = cs_b
        pre_s[slot, 6] = b_s[pl.ds(r0, CHUNK), :].astype(F32).T

    def bwd_mix_stage(c, slot):
        r0 = pl.multiple_of(c * CHUNK, CHUNK)
        x = xs_s[pl.ds(r0, CHUNK), :].astype(F32)
        cc = c_s[pl.ds(r0, CHUNK), :]
        dtv, cs_b, bt = pre_s[slot, 0], pre_s[slot, 2], pre_s[slot, 6].astype(BF16)
        y = y_s[pl.ds(r0, CHUNK), :] + state_terms(cc, bt, x, dtv, cs_b, cs_b[0:1, :], eb_ref, sb_s)
        y = y * _silu(z_ref[0, pl.ds(r0, CHUNK), :].astype(F32))
        ms = jnp.mean(y * y, axis=-1, keepdims=True)
        o_ref[0, pl.ds(r0, CHUNK), :] = (y * lax.rsqrt(ms + EPS) * ng_ref[0]).astype(BF16)

    bwd_prep_stage(n_chunks - 1, 0)

    def bwd_group(i, carry):
        c = n_chunks - 1 - SSD_CHUNK_UNROLL * i
        for j in range(SSD_CHUNK_UNROLL):
            bwd_prep_stage(jnp.maximum(c - j - 1, 0), (j + 1) % 2)
            bwd_mix_stage(c - j, j % 2)
        return carry
    lax.fori_loop(0, n_chunks // SSD_CHUNK_UNROLL, bwd_group, 0)


def _ssd(proj3, dt3, cw, cb, dtb, arow, dsk, ng, ef, eb, tri, shift):
    b, l, _ = proj3.shape
    g = N_GROUPS_SSM
    wconv = D_GROUP + 2 * D_STATE
    per_group = lambda shape: pl.BlockSpec((1,) + shape, lambda bi, gi: (gi,) + (0,) * len(shape))
    return pl.pallas_call(
        _ssd_body,
        grid=(b, g),
        in_specs=[pl.BlockSpec((1, l, D_GROUP), lambda bi, gi: (bi, 0, COL_Z + gi)),
                  pl.BlockSpec((1, l, D_GROUP), lambda bi, gi: (bi, 0, COL_XS + gi)),
                  pl.BlockSpec((1, l, D_STATE), lambda bi, gi: (bi, 0, COL_B + gi)),
                  pl.BlockSpec((1, l, D_STATE), lambda bi, gi: (bi, 0, COL_C + gi)),
                  pl.BlockSpec((1, l, LANES), lambda bi, gi: (bi, 0, gi)),
                  per_group((D_CONV, wconv)), per_group((1, wconv)),
                  per_group((1, LANES)), per_group((1, LANES)),
                  per_group((1, D_GROUP)), per_group((1, D_GROUP)),
                  pl.BlockSpec((1, LANES, D_GROUP), lambda bi, gi: (0, 0, 0)),
                  pl.BlockSpec((1, LANES, D_GROUP), lambda bi, gi: (0, 0, 0)),
                  pl.BlockSpec((2 * CHUNK, CHUNK), lambda bi, gi: (0, 0)),
                  pl.BlockSpec(shift.shape, lambda bi, gi: (0, 0))],
        out_specs=pl.BlockSpec((1, l, D_GROUP), lambda bi, gi: (bi, 0, gi)),
        out_shape=jax.ShapeDtypeStruct((b, l, D_SSM), BF16),
        scratch_shapes=[pltpu.VMEM((l, D_GROUP), BF16), pltpu.VMEM((l, D_STATE), BF16),
                        pltpu.VMEM((l, D_STATE), BF16), pltpu.VMEM((l, D_GROUP), F32),
                        pltpu.VMEM((D_STATE, D_GROUP), F32), pltpu.VMEM((D_STATE, D_GROUP), F32),
                        pltpu.VMEM((2, CHUNK, D_GROUP), F32), pltpu.VMEM((2, 7, CHUNK, LANES), F32)],
        compiler_params=_cparams(2),
        name="ssd_mixer",
    )(proj3, proj3, proj3, proj3, dt3, cw, cb, dtb, arow, dsk, ng, ef, eb, tri, shift)


def _route_rows(lg):
    lane = lax.broadcasted_iota(I32, lg.shape, 1).astype(F32)
    n_g = float(N_EXPERT_GROUPS)
    n_e = float(EXPERTS_PER_GROUP)
    gl = jnp.where(lane < n_g, lg, NEG)
    gmax = jnp.max(gl, axis=-1, keepdims=True)
    g_idx = jnp.min(jnp.where(gl == gmax, lane, float(LANES)), axis=-1, keepdims=True)
    g_p = 1.0 / jnp.sum(jnp.exp(gl - gmax), axis=-1, keepdims=True)
    first = n_g + n_e * g_idx
    el = jnp.where((lane >= first) & (lane < first + n_e), lg, NEG)
    e1 = jnp.max(el, axis=-1, keepdims=True)
    i1 = jnp.min(jnp.where(el == e1, lane, float(LANES)), axis=-1, keepdims=True)
    el2 = jnp.where(lane == i1, NEG, el)
    e2 = jnp.max(el2, axis=-1, keepdims=True)
    i2 = jnp.min(jnp.where(el2 == e2, lane, float(LANES)), axis=-1, keepdims=True)
    r = jnp.exp(e2 - e1)
    gate1 = g_p / (1.0 + r)
    gate2 = gate1 * r
    return jnp.where(lane == 0.0, i1 - n_g,
                     jnp.where(lane == 1.0, i2 - n_g,
                               jnp.where(lane == 2.0, gate1, jnp.where(lane == 3.0, gate2, 0.0))))


def _outproj_body(x_ref, a_ref, s_ref, ag_ref, wa_ref, ws_ref, g2_ref, wr_ref, br_ref, x2_ref, hp_ref, rt_ref):
    tm = x_ref.shape[0]
    sub = min(tm, OUTPROJ_SUB_ROWS)
    for r0 in range(0, tm, sub):
        rows = slice(r0, r0 + sub)
        a = a_ref[rows, :].astype(F32)
        ms = jnp.mean(a * a, axis=-1, keepdims=True)
        an = (a * lax.rsqrt(ms + EPS) * ag_ref[...]).astype(BF16)
        y = (jnp.dot(an, wa_ref[...], preferred_element_type=F32)
             + jnp.dot(s_ref[rows, :], ws_ref[...], preferred_element_type=F32))
        x2 = x_ref[rows, :] + y
        x2_ref[rows, :] = x2
        ms2 = jnp.mean(x2 * x2, axis=-1, keepdims=True)
        hn = x2 * lax.rsqrt(ms2 + EPS) * g2_ref[...]
        hi = hn.astype(BF16)
        lo = (hn - hi.astype(F32)).astype(BF16)
        l1 = jnp.dot(hi, wr_ref[...], preferred_element_type=F32)
        l2 = jnp.dot(lo, wr_ref[:, :LANES], preferred_element_type=F32)
        rt_ref[rows, :] = _route_rows(l1[:, :LANES] + l1[:, LANES:] + l2 + br_ref[...])
        packed = _pack_bf16_pairs(hi.astype(F32))
        for c in range(ROW_TILE):
            hp_ref[pl.ds(r0 * ROW_TILE + c, sub, stride=ROW_TILE), :] = packed[:, c * LANES:(c + 1) * LANES]


def _outproj(x2d, attn2d, ssm2d, ag, wa, ws, g2, wr, br):
    t, d = x2d.shape
    tm = min(2 * OUTPROJ_SUB_ROWS, t)
    row = lambda w: pl.BlockSpec((tm, w), lambda i: (i, 0))
    full = lambda a: pl.BlockSpec(a.shape, lambda i: (0,) * a.ndim)
    return pl.pallas_call(
        _outproj_body,
        grid=(t // tm,),
        in_specs=[row(d), row(D_ATTN), row(D_SSM), full(ag), full(wa), full(ws), full(g2), full(wr), full(br)],
        out_specs=[row(d), pl.BlockSpec((tm * ROW_TILE, LANES), lambda i: (i, 0)), row(LANES)],
        out_shape=[jax.ShapeDtypeStruct((t, d), F32), jax.ShapeDtypeStruct((t * ROW_TILE, LANES), U32),
                   jax.ShapeDtypeStruct((t, LANES), F32)],
        compiler_params=_cparams(1),
        name="outproj_router",
    )(x2d, attn2d, ssm2d, ag, wa, ws, g2, wr, br)


def _moe_body(vblk_ref, vexp_ref, vlo_ref, vhi_ref, tok_ref, tok_next_ref, slot_ref, slot_prev_ref, x_hbm, wg_ref,
              wu_ref, wd_ref, y_hbm, xbuf, ybuf, wgu_bf, wd_bf, cached_ref, gsem, ssem):
    blk = xbuf.shape[1] // ROW_TILE
    d_half = wg_ref.shape[1] // 2
    v = pl.program_id(0)
    lo = vlo_ref[v]
    hi = vhi_ref[v]
    s = vblk_ref[v]
    expert = vexp_ref[v]
    n_blocks = y_hbm.shape[0] // (blk * ROW_TILE) - 1
    spare_row0 = n_blocks * blk
    slot = s % 2
    other = 1 - slot

    def tile_rows(i):
        return pl.ds(pl.multiple_of(i * ROW_TILE, ROW_TILE), ROW_TILE)

    def start_gather(table_ref, buf_slot):
        def body(i, c):
            pltpu.make_async_copy(x_hbm.at[tile_rows(table_ref[0, 0, i])], xbuf.at[buf_slot, tile_rows(i)],
                                  gsem.at[buf_slot]).start()
            return c
        lax.fori_loop(0, blk, body, 0, unroll=MOE_DMA_UNROLL)

    def wait_gather(buf_slot):
        pltpu.make_async_copy(x_hbm.at[pl.ds(0, blk * ROW_TILE)], xbuf.at[buf_slot], gsem.at[buf_slot]).wait()

    def start_scatter(buf_slot):
        def body(i, c):
            pltpu.make_async_copy(ybuf.at[buf_slot, tile_rows(i)], y_hbm.at[tile_rows(slot_ref[0, 0, i])],
                                  ssem.at[buf_slot]).start()
            return c
        lax.fori_loop(0, blk, body, 0, unroll=MOE_DMA_UNROLL)

    def wait_scatter(sem_slot):
        pltpu.make_async_copy(ybuf.at[0], y_hbm.at[pl.ds(0, blk * ROW_TILE)], ssem.at[sem_slot]).wait()

    @pl.when(v == 0)
    def _():
        cached_ref[0] = -1
        ybuf[...] = jnp.zeros_like(ybuf)

    @pl.when(hi > lo)
    def _():
        @pl.when(lo == 0)
        def _():
            @pl.when(s == 0)
            def _():
                start_gather(tok_ref, slot)
            wait_gather(slot)

            @pl.when(s >= 2)
            def _():
                wait_scatter(slot)

        @pl.when(cached_ref[0] != expert)
        def _():
            rows = 256

            def cast_rows(i, c):
                r0 = pl.multiple_of(i * rows, rows)
                wgu_bf[pl.ds(r0, rows), :D_EXPERT] = wg_ref[0, pl.ds(r0, rows), :].astype(BF16)
                wgu_bf[pl.ds(r0, rows), D_EXPERT:] = wu_ref[0, pl.ds(r0, rows), :].astype(BF16)
                return c
            lax.fori_loop(0, 2 * d_half // rows, cast_rows, 0)
            wd_bf[...] = wd_ref[0].astype(BF16)
            cached_ref[0] = expert

        prefetch = (lo == 0) & (s + 1 < n_blocks)
        dst = jnp.where(prefetch, other, MOE_SPARE_SLOT)
        drain = (lo == 0) & (s > 0)
        dsem = jnp.where(drain, other, MOE_SPARE_SLOT)

        def issue_rows(first, count):
            for i in range(first, first + count):
                pltpu.make_async_copy(x_hbm.at[tile_rows(tok_next_ref[0, 0, i])],
                                      xbuf.at[dst, pl.ds(i * ROW_TILE, ROW_TILE)], gsem.at[dst]).start()
                out_row = jnp.where(drain, slot_prev_ref[0, 0, i], spare_row0 + i)
                pltpu.make_async_copy(ybuf.at[other, pl.ds(i * ROW_TILE, ROW_TILE)], y_hbm.at[tile_rows(out_row)],
                                      ssem.at[dsem]).start()

        words = _load_tile_rows(xbuf, (slot,), blk)
        xlo = jnp.concatenate([_unpack_lo(w).astype(BF16) for w in words], axis=1)
        xhi = jnp.concatenate([_unpack_hi(w).astype(BF16) for w in words], axis=1)
        quarter = blk // 4
        gate_up = []
        for j in range(2):
            issue_rows(j * quarter, quarter)
            cols = slice(j * D_EXPERT, (j + 1) * D_EXPERT)
            gate_up.append(jnp.dot(xlo, wgu_bf[:d_half, cols], preferred_element_type=F32)
                           + jnp.dot(xhi, wgu_bf[d_half:, cols], preferred_element_type=F32))
        hid = (_silu(gate_up[0]) * gate_up[1]).astype(BF16)
        ow = wd_bf.shape[1] // MOE_COL_CHUNKS
        per_chunk = 2 * quarter // MOE_COL_CHUNKS
        outs = []
        for j in range(MOE_COL_CHUNKS):
            issue_rows(2 * quarter + j * per_chunk, per_chunk)
            outs.append(jnp.dot(hid, wd_bf[:, j * ow:(j + 1) * ow], preferred_element_type=F32))
        y = _pack_bf16_pairs(jnp.concatenate(outs, axis=1))

        @pl.when(jnp.logical_not(prefetch))
        def _():
            wait_gather(MOE_SPARE_SLOT)

        @pl.when(jnp.logical_not(drain))
        def _():
            wait_scatter(MOE_SPARE_SLOT)

        @pl.when(lo == 0)
        def _():
            _store_tile_rows(ybuf, (slot,), y)

        @pl.when(lo > 0)
        def _():
            row = lax.broadcasted_iota(I32, (blk, 1), 0)
            old = jnp.concatenate(_load_tile_rows(ybuf, (slot,), blk), axis=1)
            _store_tile_rows(ybuf, (slot,), jnp.where((row >= lo) & (row < hi), y, old))

        @pl.when((hi == blk) & (s == n_blocks - 1))
        def _():
            if n_blocks > 1:
                wait_scatter(other)
            start_scatter(slot)
            wait_scatter(slot)


def _moe(visits, row_tok, row_slot, hn_tiles, w_gate, w_up, w_down):
    blk = MOE_BLOCK
    n_assign = row_tok.shape[0]
    n_blocks = n_assign // blk
    n_visits = visits[0].shape[0]
    d = w_gate.shape[1]
    smem_rows = pl.BlockSpec((1, 1, blk), lambda v, vb, ve, vl, vh: (vb[v], 0, 0), memory_space=pltpu.SMEM)
    smem_next = pl.BlockSpec((1, 1, blk), lambda v, vb, ve, vl, vh: (jnp.minimum(vb[v] + 1, n_blocks - 1), 0, 0),
                             memory_space=pltpu.SMEM)
    smem_prev = pl.BlockSpec((1, 1, blk), lambda v, vb, ve, vl, vh: (jnp.maximum(vb[v] - 1, 0), 0, 0),
                             memory_space=pltpu.SMEM)
    by_expert = lambda shape: pl.BlockSpec((1,) + shape, lambda v, vb, ve, vl, vh: (ve[v], 0, 0))
    grid_spec = pltpu.PrefetchScalarGridSpec(
        num_scalar_prefetch=4,
        grid=(n_visits,),
        in_specs=[smem_rows, smem_next, smem_rows, smem_prev,
                  pl.BlockSpec(memory_space=pl.ANY),
                  by_expert((d, D_EXPERT)), by_expert((d, D_EXPERT)), by_expert((D_EXPERT, d))],
        out_specs=pl.BlockSpec(memory_space=pl.ANY),
        scratch_shapes=[pltpu.VMEM((MOE_SPARE_SLOT + 1, blk * ROW_TILE, LANES), U32),
                        pltpu.VMEM((2, blk * ROW_TILE, LANES), U32),
                        pltpu.VMEM((d, 2 * D_EXPERT), BF16), pltpu.VMEM((D_EXPERT, d), BF16),
                        pltpu.SMEM((1,), I32),
                        pltpu.SemaphoreType.DMA((MOE_SPARE_SLOT + 1,)),
                        pltpu.SemaphoreType.DMA((MOE_SPARE_SLOT + 1,))],
    )
    tok3 = row_tok.reshape(n_blocks, 1, blk)
    slot3 = row_slot.reshape(n_blocks, 1, blk)
    return pl.pallas_call(
        _moe_body,
        grid_spec=grid_spec,
        out_shape=jax.ShapeDtypeStruct(((n_assign + blk) * ROW_TILE, LANES), U32),
        compiler_params=_cparams(1),
        name="moe_experts",
    )(*visits, tok3, tok3, slot3, slot3, hn_tiles, w_gate, w_up, w_down)


def _route_tables(route, t):
    blk = MOE_BLOCK
    n_assign = t * TOP_K
    n_blocks = n_assign // blk
    flat_e = route[:, :TOP_K].astype(I32).T.reshape(n_assign)
    order = jnp.argsort(flat_e, stable=True).astype(I32)
    counts = jnp.sum((flat_e[:, None] == jnp.arange(N_EXPERTS, dtype=I32)[None, :]).astype(I32), axis=0)
    ends = jnp.cumsum(counts)
    starts = ends - counts
    cuts = jnp.sort(jnp.concatenate([jnp.arange(n_blocks, dtype=I32) * blk, starts[1:]]))
    nxt = jnp.concatenate([cuts[1:], jnp.full((1,), n_assign, I32)])
    vblk = jnp.minimum(cuts // blk, n_blocks - 1)
    vexp = jnp.minimum(jnp.sum((ends[None, :] <= cuts[:, None]).astype(I32), axis=1), N_EXPERTS - 1)
    vlo = cuts - vblk * blk
    vhi = jnp.maximum(jnp.minimum(nxt, (vblk + 1) * blk) - vblk * blk, vlo)
    return (vblk, vexp, vlo, vhi), order % t, order


def _combine_body(x2_ref, y0_ref, y1_ref, rt_ref, o_ref):
    tm = o_ref.shape[0]
    half = o_ref.shape[1] // 2
    g0 = rt_ref[:, TOP_K:TOP_K + 1]
    g1 = rt_ref[:, TOP_K + 1:TOP_K + 2]
    y0 = _load_tile_rows(y0_ref, (), tm)
    y1 = _load_tile_rows(y1_ref, (), tm)
    for c in range(ROW_TILE):
        lo_cols = slice(c * LANES, (c + 1) * LANES)
        hi_cols = slice(half + c * LANES, half + (c + 1) * LANES)
        o_ref[:, lo_cols] = x2_ref[:, lo_cols] + g0 * _unpack_lo(y0[c]) + g1 * _unpack_lo(y1[c])
        o_ref[:, hi_cols] = x2_ref[:, hi_cols] + g0 * _unpack_hi(y0[c]) + g1 * _unpack_hi(y1[c])


def _combine(x2, y_slots, route):
    t, d = x2.shape
    tm = min(512, t)
    nb = t // tm
    return pl.pallas_call(
        _combine_body,
        grid=(nb,),
        in_specs=[pl.BlockSpec((tm, d), lambda i: (i, 0)),
                  pl.BlockSpec((tm * ROW_TILE, LANES), lambda i: (i, 0)),
                  pl.BlockSpec((tm * ROW_TILE, LANES), lambda i: (i + nb, 0)),
                  pl.BlockSpec((tm, LANES), lambda i: (i, 0))],
        out_specs=pl.BlockSpec((tm, d), lambda i: (i, 0)),
        out_shape=jax.ShapeDtypeStruct((t, d), F32),
        compiler_params=_cparams(1),
        name="moe_combine",
    )(x2, y_slots, y_slots, route)


def _prepare(norm1_g, w_in, q_norm_g, k_norm_g, rpb, attn_out_g, conv_w, conv_b, a_log_f, a_log_b,
             dt_bias_f, dt_bias_b, d_skip, ssm_norm_g, w_out, norm2_g, w_router_group, b_router_group,
             w_router_expert, b_router_expert, w_gate, w_up, w_down):
    d = w_in.shape[0]
    hpg = HEADS_PER_GROUP
    p = {}
    p["norm1_g"] = norm1_g.reshape(1, d).astype(F32)
    p["w_main"] = w_in[:, :D_PROJ_MAIN].astype(BF16)
    w_dt = w_in[:, D_PROJ_MAIN:]
    zeros = jnp.zeros((d, LANES - 2 * hpg), w_in.dtype)
    per_group = lambda v, g: v[..., g * hpg:(g + 1) * hpg]
    p["w_dt"] = jnp.concatenate(
        [jnp.concatenate([per_group(w_dt[:, :N_HEADS_SSM], g), per_group(w_dt[:, N_HEADS_SSM:], g), zeros], axis=1)
         for g in range(N_GROUPS_SSM)], axis=1).astype(BF16)
    lane_rows = lambda f, bwd: jnp.stack(
        [jnp.concatenate([per_group(f, g), per_group(bwd, g), jnp.zeros((LANES - 2 * hpg,), F32)])
         for g in range(N_GROUPS_SSM)])[:, None, :]
    p["dtb"] = lane_rows(dt_bias_f.astype(F32), dt_bias_b.astype(F32))
    p["arow"] = lane_rows(-jnp.exp(a_log_f.astype(F32)), -jnp.exp(a_log_b.astype(F32)))
    scale = HEAD_DIM ** -0.5
    p["qg2"] = (jnp.tile(q_norm_g.astype(F32), 2) * scale).reshape(1, LANES)
    p["kg2"] = jnp.tile(k_norm_g.astype(F32), 2).reshape(1, LANES)
    lane = jnp.arange(LANES)
    p["e_mat"] = (lane[:, None] // HEAD_DIM == lane[None, :] // HEAD_DIM).astype(BF16)
    p["bias"] = _attention_bias(rpb)
    p["attn_out_g"] = attn_out_g.reshape(1, D_ATTN).astype(F32)
    cw = conv_w.reshape(D_CONV, -1).astype(F32)
    cbias = conv_b.reshape(1, -1).astype(F32)
    group_cols = lambda a, g: jnp.concatenate(
        [a[:, g * D_GROUP:(g + 1) * D_GROUP],
         a[:, D_SSM + g * D_STATE:D_SSM + (g + 1) * D_STATE],
         a[:, D_SSM + D_BC + g * D_STATE:D_SSM + D_BC + (g + 1) * D_STATE]], axis=1)
    p["cw"] = jnp.stack([group_cols(cw, g) for g in range(N_GROUPS_SSM)])
    p["cb"] = jnp.stack([group_cols(cbias, g) for g in range(N_GROUPS_SSM)])
    p["dsk"] = jnp.repeat(d_skip.astype(F32), SSM_HEAD_DIM).reshape(N_GROUPS_SSM, 1, D_GROUP)
    p["ng"] = ssm_norm_g.astype(F32).reshape(N_GROUPS_SSM, 1, D_GROUP)
    col_head = jnp.arange(D_GROUP) // SSM_HEAD_DIM
    p["ef"] = (lane[:, None] == col_head[None, :]).astype(BF16)[None]
    p["eb"] = (lane[:, None] == col_head[None, :] + hpg).astype(BF16)[None]
    pos = jnp.arange(CHUNK)
    p["tri"] = jnp.concatenate([pos[:, None] >= pos[None, :], pos[:, None] <= pos[None, :]], axis=0).astype(BF16)
    src = jnp.arange(CHUNK + 2 * CONV_HALO)
    p["shift"] = jnp.concatenate(
        [src[None, :] == pos[:, None] + (CONV_HALO - D_CONV // 2 + k) for k in range(D_CONV) if k != D_CONV // 2],
        axis=0).astype(BF16)
    p["wa"] = w_out[:D_ATTN].astype(BF16)
    p["ws"] = w_out[D_ATTN:].astype(BF16)
    p["norm2_g"] = norm2_g.reshape(1, d).astype(F32)
    n_r = N_EXPERT_GROUPS + N_EXPERTS
    wr = jnp.concatenate([w_router_group, w_router_expert, jnp.zeros((d, LANES - n_r), F32)], axis=1).astype(F32)
    wr_hi = wr.astype(BF16)
    wr_lo = (wr - wr_hi.astype(F32)).astype(BF16)
    p["wr"] = jnp.concatenate([wr_hi, wr_lo], axis=1)
    p["br"] = jnp.concatenate([b_router_group, b_router_expert, jnp.zeros((LANES - n_r,), F32)]).reshape(1, LANES)
    p["w_gate"], p["w_up"], p["w_down"] = w_gate, w_up, w_down
    return p


def _layer(x, p):
    b, l, d = x.shape
    t = b * l
    x2d = x.reshape(t, d)
    proj, dt = _inproj(x2d, p["norm1_g"], p["w_main"], p["w_dt"])
    proj3 = proj.reshape(b, l, D_PROJ_MAIN)
    attn = _attention(proj3, p["qg2"], p["kg2"], p["e_mat"], p["bias"])
    ssm = _ssd(proj3, dt.reshape(b, l, N_GROUPS_SSM * LANES), p["cw"], p["cb"], p["dtb"], p["arow"],
               p["dsk"], p["ng"], p["ef"], p["eb"], p["tri"], p["shift"])
    x2, hn_packed, route = _outproj(x2d, attn.reshape(t, D_ATTN), ssm.reshape(t, D_SSM), p["attn_out_g"],
                                    p["wa"], p["ws"], p["norm2_g"], p["wr"], p["br"])
    visits, row_tok, row_slot = _route_tables(route, t)
    y_slots = _moe(visits, row_tok, row_slot, hn_packed, p["w_gate"], p["w_up"], p["w_down"])
    return _combine(x2, y_slots, route).reshape(b, l, d)


def kernel(x_prompt, x_sample, norm1_g, w_in, q_norm_g, k_norm_g, rpb, attn_out_g, conv_w, conv_b, a_log_f,
           a_log_b, dt_bias_f, dt_bias_b, d_skip, ssm_norm_g, w_out, norm2_g, w_router_group, b_router_group,
           w_router_expert, b_router_expert, w_gate, w_up, w_down):
    weights = (norm1_g, w_in, q_norm_g, k_norm_g, rpb, attn_out_g, conv_w, conv_b, a_log_f, a_log_b, dt_bias_f,
               dt_bias_b, d_skip, ssm_norm_g, w_out, norm2_g, w_router_group, b_router_group, w_router_expert,
               b_router_expert, w_gate, w_up, w_down)
    assert all(w.shape[0] == 1 for w in weights), "one layer of stacked weights expected"
    p = _prepare(*(w[0] for w in weights))
    return (_layer(x_prompt, p), _layer(x_sample, p))
```

```python
import jax
import jax.numpy as jnp
from jax import lax
from jax.experimental import pallas as pl
from jax.experimental.pallas import tpu as pltpu

F32 = jnp.float32
BF16 = jnp.bfloat16
U32 = jnp.uint32
I32 = jnp.int32

EPS = 1e-6
GRID_W = 64
N_HEADS_ATTN = 16
HEAD_DIM = 64
D_ATTN = N_HEADS_ATTN * HEAD_DIM
WIN_H = 8
WIN_W = 16
N_HEADS_SSM = 16
SSM_HEAD_DIM = 64
D_SSM = N_HEADS_SSM * SSM_HEAD_DIM
N_GROUPS_SSM = 2
HEADS_PER_GROUP = N_HEADS_SSM // N_GROUPS_SSM
D_GROUP = D_SSM // N_GROUPS_SSM
D_STATE = 128
D_CONV = 5
CHUNK = 128
D_BC = N_GROUPS_SSM * D_STATE
N_EXPERT_GROUPS = 4
EXPERTS_PER_GROUP = 8
N_EXPERTS = N_EXPERT_GROUPS * EXPERTS_PER_GROUP
TOP_K = 2
D_EXPERT = 512

LANES = 128
BF16_ROWS = 16
CONV_HALO = BF16_ROWS
NEG = -1e30
VMEM_LIMIT_BYTES = 56 * 1024 * 1024
MOE_BLOCK = 256
ATTN_ROW_UNROLL = 8
ATTN_BATCH_PER_STEP = 2
SSD_CHUNK_UNROLL = 4
MOE_DMA_UNROLL = 8
MOE_SPARE_SLOT = 2
MOE_COL_CHUNKS = 4
ROW_TILE = 8
OUTPROJ_SUB_ROWS = 256
INPROJ_ROWS = 512
INPROJ_COLS = 2816

COL_Q, COL_K, COL_V = 0, D_ATTN // LANES, 2 * D_ATTN // LANES
COL_Z = 3 * D_ATTN // D_GROUP
COL_XS = (3 * D_ATTN + D_SSM) // D_GROUP
COL_B = (3 * D_ATTN + 2 * D_SSM) // LANES
COL_C = COL_B + D_BC // LANES
D_PROJ_MAIN = 3 * D_ATTN + 2 * D_SSM + 2 * D_BC


def _cparams(n_axes):
    return pltpu.CompilerParams(dimension_semantics=("arbitrary",) * n_axes,
                                vmem_limit_bytes=VMEM_LIMIT_BYTES)


def _silu(x):
    return x * (0.5 * jnp.tanh(0.5 * x) + 0.5)


def _split3(x):
    hi = x.astype(BF16)
    r1 = x - hi.astype(F32)
    mid = r1.astype(BF16)
    lo = (r1 - mid.astype(F32)).astype(BF16)
    return hi, mid, lo


def _pack_bf16_pairs(x):
    n = x.shape[1] // 2
    u = lax.bitcast_convert_type(x.astype(BF16).astype(F32), U32)
    return (u[:, :n] >> 16) | u[:, n:]


def _unpack_lo(u):
    return lax.bitcast_convert_type(u << 16, F32)


def _unpack_hi(u):
    return lax.bitcast_convert_type(u & jnp.uint32(0xFFFF0000), F32)


def _store_tile_rows(ref, lead, packed):
    m = packed.shape[0]
    for c in range(ROW_TILE):
        ref[lead + (pl.ds(c, m, stride=ROW_TILE), slice(None))] = packed[:, c * LANES:(c + 1) * LANES]


def _load_tile_rows(ref, lead, m):
    return [ref[lead + (pl.ds(c, m, stride=ROW_TILE), slice(None))] for c in range(ROW_TILE)]


def _inproj_body(x_ref, g_ref, w_ref, wdt_ref, o_ref, dt_ref, hn_ref):
    tm = x_ref.shape[0]
    rows = min(tm, 256)

    @pl.when(pl.program_id(1) == 0)
    def _():
        def norm_rows(i, c):
            r0 = pl.multiple_of(i * rows, rows)
            x = x_ref[pl.ds(r0, rows), :]
            ms = jnp.mean(x * x, axis=-1, keepdims=True)
            hn_ref[pl.ds(r0, rows), :] = (x * lax.rsqrt(ms + EPS) * g_ref[...]).astype(BF16)
            return c
        lax.fori_loop(0, tm // rows, norm_rows, 0)
        dt_ref[...] = jnp.dot(hn_ref[...], wdt_ref[...], preferred_element_type=F32)

    o_ref[...] = jnp.dot(hn_ref[...], w_ref[...], preferred_element_type=F32).astype(BF16)


def _inproj(x2d, gain, w_main, w_dt):
    t, d = x2d.shape
    n = w_main.shape[1]
    ndt = w_dt.shape[1]
    tm = min(INPROJ_ROWS, t)
    tn = n
    return pl.pallas_call(
        _inproj_body,
        grid=(t // tm, n // tn),
        in_specs=[pl.BlockSpec((tm, d), lambda i, j: (i, 0)),
                  pl.BlockSpec((1, d), lambda i, j: (0, 0)),
                  pl.BlockSpec((d, tn), lambda i, j: (0, j), pipeline_mode=pl.Buffered(1)),
                  pl.BlockSpec((d, ndt), lambda i, j: (0, 0))],
        out_specs=[pl.BlockSpec((tm, tn), lambda i, j: (i, j)),
                   pl.BlockSpec((tm, ndt), lambda i, j: (i, 0))],
        out_shape=[jax.ShapeDtypeStruct((t, n), BF16), jax.ShapeDtypeStruct((t, ndt), F32)],
        scratch_shapes=[pltpu.VMEM((tm, d), BF16)],
        compiler_params=_cparams(2),
        name="inproj",
    )(x2d, gain, w_main, w_dt)


def _attn_body(q_ref, k_ref, v_ref, qg_ref, kg_ref, e_ref, bias_ref, o_ref, qs, ks, s_scr, m_scr):
    l = q_ref.shape[1]
    n_rows = l // GRID_W
    win_keys = WIN_H * GRID_W
    ch = min(l, 512)
    head_a = lax.broadcasted_iota(I32, (1, LANES), 1) < HEAD_DIM
    sel_a = jnp.where(head_a, 1.0, 0.0).astype(BF16)
    sel_b = jnp.where(head_a, 0.0, 1.0).astype(BF16)

    def key_start(r):
        return jnp.clip(r - WIN_H // 2, 0, n_rows - WIN_H)

    def one_batch(bb, carry):
        def norm_rows(i, c):
            r0 = pl.multiple_of(i * ch, ch)
            for src, gref, dst in ((q_ref, qg_ref, qs), (k_ref, kg_ref, ks)):
                x = src[bb, pl.ds(r0, ch), :].astype(F32)
                ssq = jnp.dot((x * x).astype(BF16), e_ref[...], preferred_element_type=F32)
                dst[pl.ds(r0, ch), :] = (x * lax.rsqrt(ssq * (1.0 / HEAD_DIM) + EPS) * gref[...]).astype(BF16)
            return c
        lax.fori_loop(0, l // ch, norm_rows, 0)

        def score_stage(r, slot):
            rs = key_start(r)
            q_r = qs[pl.ds(pl.multiple_of(r * GRID_W, GRID_W), GRID_W), :]
            qm = jnp.concatenate([q_r * sel_a, q_r * sel_b], axis=0)
            kb = ks[pl.ds(pl.multiple_of(rs * GRID_W, GRID_W), win_keys), :]
            s = lax.dot_general(qm, kb, (((1,), (1,)), ((), ())), preferred_element_type=F32)
            dr0 = rs - r + (WIN_H - 1)
            lane0 = pl.multiple_of((dr0 // 2) * LANES, LANES)
            s = s + bias_ref[0, dr0 % 2, :, pl.ds(lane0, win_keys)]
            s_scr[slot] = s
            m_scr[slot] = jnp.max(s, axis=-1, keepdims=True)

        def value_stage(r, slot):
            rs = key_start(r)
            vb = v_ref[bb, pl.ds(pl.multiple_of(rs * GRID_W, GRID_W), win_keys), :]
            p = jnp.exp(s_scr[slot] - m_scr[slot])
            den = jnp.sum(p, axis=-1, keepdims=True)
            o = jnp.dot(p.astype(BF16), vb, preferred_element_type=F32) * (1.0 / den)
            out = jnp.where(head_a, o[:GRID_W], o[GRID_W:])
            o_ref[bb, pl.ds(pl.multiple_of(r * GRID_W, GRID_W), GRID_W), :] = out.astype(BF16)

        score_stage(0, 0)

        def row_group(i, c):
            r = ATTN_ROW_UNROLL * i
            for j in range(ATTN_ROW_UNROLL):
                score_stage(jnp.minimum(r + j + 1, n_rows - 1), (j + 1) % 2)
                value_stage(r + j, j % 2)
            return c
        lax.fori_loop(0, n_rows // ATTN_ROW_UNROLL, row_group, 0)
        return carry
    lax.fori_loop(0, q_ref.shape[0], one_batch, 0)


def _attention(proj3, qg2, kg2, e_mat, bias):
    b, l, _ = proj3.shape
    n_pairs = N_HEADS_ATTN // 2
    nb = ATTN_BATCH_PER_STEP if b % ATTN_BATCH_PER_STEP == 0 else 1
    blk = (nb, l, LANES)
    return pl.pallas_call(
        _attn_body,
        grid=(n_pairs, b // nb),
        in_specs=[pl.BlockSpec(blk, lambda hp, bi: (bi, 0, COL_Q + hp)),
                  pl.BlockSpec(blk, lambda hp, bi: (bi, 0, COL_K + hp)),
                  pl.BlockSpec(blk, lambda hp, bi: (bi, 0, COL_V + hp)),
                  pl.BlockSpec((1, LANES), lambda hp, bi: (0, 0)),
                  pl.BlockSpec((1, LANES), lambda hp, bi: (0, 0)),
                  pl.BlockSpec((LANES, LANES), lambda hp, bi: (0, 0)),
                  pl.BlockSpec((1,) + bias.shape[1:], lambda hp, bi: (hp, 0, 0, 0))],
        out_specs=pl.BlockSpec(blk, lambda hp, bi: (bi, 0, hp)),
        out_shape=jax.ShapeDtypeStruct((b, l, D_ATTN), BF16),
        scratch_shapes=[pltpu.VMEM((l, LANES), BF16), pltpu.VMEM((l, LANES), BF16),
                        pltpu.VMEM((2, LANES, WIN_H * GRID_W), F32), pltpu.VMEM((2, LANES, 1), F32)],
        compiler_params=_cparams(2),
        name="nbr_attention",
    )(proj3, proj3, proj3, qg2, kg2, e_mat, bias)


def _attention_bias(rpb):
    n_dr = 2 * WIN_H - 1
    cols = jnp.arange(GRID_W, dtype=I32)
    col_start = jnp.clip(cols - WIN_W // 2, 0, GRID_W - WIN_W)
    keys = cols[None, :]
    valid = (keys >= col_start[:, None]) & (keys < col_start[:, None] + WIN_W)
    dc = jnp.clip(keys - cols[:, None] + (WIN_W - 1), 0, 2 * WIN_W - 2)
    tab = jnp.where(valid[None, None], rpb.astype(F32)[:, :, dc], NEG)
    flat = jnp.transpose(tab, (0, 2, 1, 3)).reshape(N_HEADS_ATTN // 2, 2 * GRID_W, n_dr * GRID_W)
    moved = jnp.concatenate([flat[:, :, GRID_W:], jnp.full(flat.shape[:2] + (GRID_W,), NEG, F32)], axis=2)
    return jnp.stack([flat, moved], axis=1)


def _ssd_body(z_ref, xs_ref, b_ref, c_ref, dt_ref, cw_ref, cb_ref, dtb_ref, arow_ref, dsk_ref, ng_ref,
              ef_ref, eb_ref, tri_ref, shift_ref, o_ref, xs_s, b_s, c_s, y_s, sf_s, sb_s, x_f, pre_s):
    l = xs_ref.shape[1]
    n_chunks = l // CHUNK
    hpg = HEADS_PER_GROUP
    ii = lax.broadcasted_iota(I32, (CHUNK, CHUNK), 0)
    jj = lax.broadcasted_iota(I32, (CHUNK, CHUNK), 1)
    causal = ii >= jj
    anti = ii <= jj
    head_a = lax.broadcasted_iota(I32, (1, LANES), 1) < SSM_HEAD_DIM

    def conv_chunk(r0):
        p0 = pl.multiple_of(jnp.maximum(r0 - CONV_HALO, 0), CONV_HALO)
        n0 = pl.multiple_of(jnp.minimum(r0 + CHUNK, l - CONV_HALO), CONV_HALO)
        has_prev = r0 > 0
        has_next = r0 + CHUNK < l
        parts = []
        for ref in (xs_ref, b_ref, c_ref):
            prev = ref[0, pl.ds(p0, CONV_HALO), :]
            nxt = ref[0, pl.ds(n0, CONV_HALO), :]
            parts.append(jnp.concatenate([jnp.where(has_prev, prev, jnp.zeros_like(prev)),
                                          ref[0, pl.ds(r0, CHUNK), :],
                                          jnp.where(has_next, nxt, jnp.zeros_like(nxt))], axis=0))
        ext = jnp.concatenate(parts, axis=1)
        shifted = jnp.dot(shift_ref[...], ext, preferred_element_type=F32)
        mid = D_CONV // 2
        acc = cb_ref[0] + ext[CONV_HALO:CONV_HALO + CHUNK].astype(F32) * cw_ref[0, mid:mid + 1, :]
        for n, k in enumerate(k for k in range(D_CONV) if k != mid):
            acc = acc + shifted[n * CHUNK:(n + 1) * CHUNK] * cw_ref[0, k:k + 1, :]
        return _silu(acc)

    def dt_terms(r0):
        raw = dt_ref[0, pl.ds(r0, CHUNK), :] + dtb_ref[0]
        dtv = jnp.maximum(raw, 0.0) + jnp.log(1.0 + jnp.exp(-jnp.abs(raw)))
        adt = dtv * arow_ref[0]
        hi, mid, lo = _split3(adt)
        cs3 = jnp.dot(tri_ref[...], jnp.concatenate([hi, mid, lo], axis=1), preferred_element_type=F32)
        cs = cs3[:, :LANES] + cs3[:, LANES:2 * LANES] + cs3[:, 2 * LANES:]
        return dtv, cs[:CHUNK], cs[CHUNK:]

    def expand_exact(row, e_ref):
        hi, mid, lo = _split3(jnp.broadcast_to(row, (8, LANES)))
        e = e_ref[0]
        r = (jnp.dot(hi, e, preferred_element_type=F32) + jnp.dot(mid, e, preferred_element_type=F32)
             + jnp.dot(lo, e, preferred_element_type=F32))
        return r[0:1]

    def state_terms(cc, bt, x, dtv, cs, tot, e_ref, s_ref):
        e = e_ref[0]
        expcs = jnp.dot(jnp.exp(cs).astype(BF16), e, preferred_element_type=F32)
        y_off = jnp.dot(cc, s_ref[...].astype(BF16), preferred_element_type=F32) * expcs
        scl = jnp.dot((dtv * jnp.exp(tot - cs)).astype(BF16), e, preferred_element_type=F32)
        states_t = jnp.dot(bt, (x * scl).astype(BF16), preferred_element_type=F32)
        s_ref[...] = s_ref[...] * expand_exact(jnp.exp(tot), e_ref) + states_t
        return y_off

    sf_s[...] = jnp.zeros_like(sf_s)
    sb_s[...] = jnp.zeros_like(sb_s)

    def prep_stage(c, slot):
        r0 = pl.multiple_of(c * CHUNK, CHUNK)
        xbc = conv_chunk(r0)
        x = xbc[:, :D_GROUP]
        bm = xbc[:, D_GROUP:D_GROUP + D_STATE]
        xs_s[pl.ds(r0, CHUNK), :] = x.astype(BF16)
        b_s[pl.ds(r0, CHUNK), :] = bm.astype(BF16)
        c_s[pl.ds(r0, CHUNK), :] = xbc[:, D_GROUP + D_STATE:].astype(BF16)
        x_f[slot] = x
        dtv, cs_f, cs_b = dt_terms(r0)
        for n, a in enumerate((dtv, cs_f, cs_b, cs_f.T, cs_b.T, dtv.T, bm.T)):
            pre_s[slot, n] = a

    def mix_stage(c, slot):
        r0 = pl.multiple_of(c * CHUNK, CHUNK)
        x = x_f[slot]
        xb = xs_s[pl.ds(r0, CHUNK), :]
        bb = b_s[pl.ds(r0, CHUNK), :]
        cc = c_s[pl.ds(r0, CHUNK), :]
        dtv, cs_f, cs_b, cst_f, cst_b, dtt, bmt = (pre_s[slot, n] for n in range(7))
        cb = lax.dot_general(cc, bb, (((1,), (1,)), ((), ())), preferred_element_type=F32)
        pieces = []
        for pair in range(hpg // 2):
            xp = xb[:, pair * LANES:(pair + 1) * LANES]
            ys = []
            for hh in range(2):
                h = 2 * pair + hh
                df = cs_f[:, h:h + 1] - cst_f[h:h + 1, :]
                db = cs_b[:, hpg + h:hpg + h + 1] - cst_b[hpg + h:hpg + h + 1, :]
                lf = jnp.exp(jnp.where(causal, df, NEG)) * dtt[h:h + 1, :]
                lb = jnp.exp(jnp.where(anti, db, NEG)) * dtt[hpg + h:hpg + h + 1, :]
                m = (cb * (lf + lb)).astype(BF16)
                ys.append(jnp.dot(m, xp, preferred_element_type=F32))
            pieces.append(jnp.where(head_a, ys[0], ys[1]))
        y = jnp.concatenate(pieces, axis=1) + x * dsk_ref[0]
        y = y + state_terms(cc, bmt.astype(BF16), x, dtv, cs_f, cs_f[CHUNK - 1:CHUNK, :], ef_ref, sf_s)
        y_s[pl.ds(r0, CHUNK), :] = y

    prep_stage(0, 0)

    def fwd_group(i, carry):
        c = SSD_CHUNK_UNROLL * i
        for j in range(SSD_CHUNK_UNROLL):
            prep_stage(jnp.minimum(c + j + 1, n_chunks - 1), (j + 1) % 2)
            mix_stage(c + j, j % 2)
        return carry
    lax.fori_loop(0, n_chunks // SSD_CHUNK_UNROLL, fwd_group, 0)

    def bwd_prep_stage(c, slot):
        r0 = pl.multiple_of(c * CHUNK, CHUNK)
        dtv, _, cs_b = dt_terms(r0)
        pre_s[slot, 0] = dtv
        pre_s[slot, 2] = cs_b
        pre_s[slot, 6] = b_s[pl.ds(r0, CHUNK), :].astype(F32).T

    def bwd_mix_stage(c, slot):
        r0 = pl.multiple_of(c * CHUNK, CHUNK)
        x = xs_s[pl.ds(r0, CHUNK), :].astype(F32)
        cc = c_s[pl.ds(r0, CHUNK), :]
        dtv, cs_b, bt = pre_s[slot, 0], pre_s[slot, 2], pre_s[slot, 6].astype(BF16)
        y = y_s[pl.ds(r0, CHUNK), :] + state_terms(cc, bt, x, dtv, cs_b, cs_b[0:1, :], eb_ref, sb_s)
        y = y * _silu(z_ref[0, pl.ds(r0, CHUNK), :].astype(F32))
        ms = jnp.mean(y * y, axis=-1, keepdims=True)
        o_ref[0, pl.ds(r0, CHUNK), :] = (y * lax.rsqrt(ms + EPS) * ng_ref[0]).astype(BF16)

    bwd_prep_stage(n_chunks - 1, 0)

    def bwd_group(i, carry):
        c = n_chunks - 1 - SSD_CHUNK_UNROLL * i
        for j in range(SSD_CHUNK_UNROLL):
            bwd_prep_stage(jnp.maximum(c - j - 1, 0), (j + 1) % 2)
            bwd_mix_stage(c - j, j % 2)
        return carry
    lax.fori_loop(0, n_chunks // SSD_CHUNK_UNROLL, bwd_group, 0)


def _ssd(proj3, dt3, cw, cb, dtb, arow, dsk, ng, ef, eb, tri, shift):
    b, l, _ = proj3.shape
    g = N_GROUPS_SSM
    wconv = D_GROUP + 2 * D_STATE
    per_group = lambda shape: pl.BlockSpec((1,) + shape, lambda bi, gi: (gi,) + (0,) * len(shape))
    return pl.pallas_call(
        _ssd_body,
        grid=(b, g),
        in_specs=[pl.BlockSpec((1, l, D_GROUP), lambda bi, gi: (bi, 0, COL_Z + gi)),
                  pl.BlockSpec((1, l, D_GROUP), lambda bi, gi: (bi, 0, COL_XS + gi)),
                  pl.BlockSpec((1, l, D_STATE), lambda bi, gi: (bi, 0, COL_B + gi)),
                  pl.BlockSpec((1, l, D_STATE), lambda bi, gi: (bi, 0, COL_C + gi)),
                  pl.BlockSpec((1, l, LANES), lambda bi, gi: (bi, 0, gi)),
                  per_group((D_CONV, wconv)), per_group((1, wconv)),
                  per_group((1, LANES)), per_group((1, LANES)),
                  per_group((1, D_GROUP)), per_group((1, D_GROUP)),
                  pl.BlockSpec((1, LANES, D_GROUP), lambda bi, gi: (0, 0, 0)),
                  pl.BlockSpec((1, LANES, D_GROUP), lambda bi, gi: (0, 0, 0)),
                  pl.BlockSpec((2 * CHUNK, CHUNK), lambda bi, gi: (0, 0)),
                  pl.BlockSpec(shift.shape, lambda bi, gi: (0, 0))],
        out_specs=pl.BlockSpec((1, l, D_GROUP), lambda bi, gi: (bi, 0, gi)),
        out_shape=jax.ShapeDtypeStruct((b, l, D_SSM), BF16),
        scratch_shapes=[pltpu.VMEM((l, D_GROUP), BF16), pltpu.VMEM((l, D_STATE), BF16),
                        pltpu.VMEM((l, D_STATE), BF16), pltpu.VMEM((l, D_GROUP), F32),
                        pltpu.VMEM((D_STATE, D_GROUP), F32), pltpu.VMEM((D_STATE, D_GROUP), F32),
                        pltpu.VMEM((2, CHUNK, D_GROUP), F32), pltpu.VMEM((2, 7, CHUNK, LANES), F32)],
        compiler_params=_cparams(2),
        name="ssd_mixer",
    )(proj3, proj3, proj3, proj3, dt3, cw, cb, dtb, arow, dsk, ng, ef, eb, tri, shift)


def _route_rows(lg):
    lane = lax.broadcasted_iota(I32, lg.shape, 1).astype(F32)
    n_g = float(N_EXPERT_GROUPS)
    n_e = float(EXPERTS_PER_GROUP)
    gl = jnp.where(lane < n_g, lg, NEG)
    gmax = jnp.max(gl, axis=-1, keepdims=True)
    g_idx = jnp.min(jnp.where(gl == gmax, lane, float(LANES)), axis=-1, keepdims=True)
    g_p = 1.0 / jnp.sum(jnp.exp(gl - gmax), axis=-1, keepdims=True)
    first = n_g + n_e * g_idx
    el = jnp.where((lane >= first) & (lane < first + n_e), lg, NEG)
    e1 = jnp.max(el, axis=-1, keepdims=True)
    i1 = jnp.min(jnp.where(el == e1, lane, float(LANES)), axis=-1, keepdims=True)
    el2 = jnp.where(lane == i1, NEG, el)
    e2 = jnp.max(el2, axis=-1, keepdims=True)
    i2 = jnp.min(jnp.where(el2 == e2, lane, float(LANES)), axis=-1, keepdims=True)
    r = jnp.exp(e2 - e1)
    gate1 = g_p / (1.0 + r)
    gate2 = gate1 * r
    return jnp.where(lane == 0.0, i1 - n_g,
                     jnp.where(lane == 1.0, i2 - n_g,
                               jnp.where(lane == 2.0, gate1, jnp.where(lane == 3.0, gate2, 0.0))))


def _outproj_body(x_ref, a_ref, s_ref, ag_ref, wa_ref, ws_ref, g2_ref, wr_ref, br_ref, x2_ref, hp_ref, rt_ref):
    tm = x_ref.shape[0]
    sub = min(tm, OUTPROJ_SUB_ROWS)
    for r0 in range(0, tm, sub):
        rows = slice(r0, r0 + sub)
        a = a_ref[rows, :].astype(F32)
        ms = jnp.mean(a * a, axis=-1, keepdims=True)
        an = (a * lax.rsqrt(ms + EPS) * ag_ref[...]).astype(BF16)
        y = (jnp.dot(an, wa_ref[...], preferred_element_type=F32)
             + jnp.dot(s_ref[rows, :], ws_ref[...], preferred_element_type=F32))
        x2 = x_ref[rows, :] + y
        x2_ref[rows, :] = x2
        ms2 = jnp.mean(x2 * x2, axis=-1, keepdims=True)
        hn = x2 * lax.rsqrt(ms2 + EPS) * g2_ref[...]
        hi = hn.astype(BF16)
        lo = (hn - hi.astype(F32)).astype(BF16)
        l1 = jnp.dot(hi, wr_ref[...], preferred_element_type=F32)
        l2 = jnp.dot(lo, wr_ref[:, :LANES], preferred_element_type=F32)
        rt_ref[rows, :] = _route_rows(l1[:, :LANES] + l1[:, LANES:] + l2 + br_ref[...])
        packed = _pack_bf16_pairs(hi.astype(F32))
        for c in range(ROW_TILE):
            hp_ref[pl.ds(r0 * ROW_TILE + c, sub, stride=ROW_TILE), :] = packed[:, c * LANES:(c + 1) * LANES]


def _outproj(x2d, attn2d, ssm2d, ag, wa, ws, g2, wr, br):
    t, d = x2d.shape
    tm = min(2 * OUTPROJ_SUB_ROWS, t)
    row = lambda w: pl.BlockSpec((tm, w), lambda i: (i, 0))
    full = lambda a: pl.BlockSpec(a.shape, lambda i: (0,) * a.ndim)
    return pl.pallas_call(
        _outproj_body,
        grid=(t // tm,),
        in_specs=[row(d), row(D_ATTN), row(D_SSM), full(ag), full(wa), full(ws), full(g2), full(wr), full(br)],
        out_specs=[row(d), pl.BlockSpec((tm * ROW_TILE, LANES), lambda i: (i, 0)), row(LANES)],
        out_shape=[jax.ShapeDtypeStruct((t, d), F32), jax.ShapeDtypeStruct((t * ROW_TILE, LANES), U32),
                   jax.ShapeDtypeStruct((t, LANES), F32)],
        compiler_params=_cparams(1),
        name="outproj_router",
    )(x2d, attn2d, ssm2d, ag, wa, ws, g2, wr, br)


def _moe_body(vblk_ref, vexp_ref, vlo_ref, vhi_ref, tok_ref, tok_next_ref, slot_ref, slot_prev_ref, x_hbm, wg_ref,
              wu_ref, wd_ref, y_hbm, xbuf, ybuf, wgu_bf, wd_bf, cached_ref, gsem, ssem):
    blk = xbuf.shape[1] // ROW_TILE
    d_half = wg_ref.shape[1] // 2
    v = pl.program_id(0)
    lo = vlo_ref[v]
    hi = vhi_ref[v]
    s = vblk_ref[v]
    expert = vexp_ref[v]
    n_blocks = y_hbm.shape[0] // (blk * ROW_TILE) - 1
    spare_row0 = n_blocks * blk
    slot = s % 2
    other = 1 - slot

    def tile_rows(i):
        return pl.ds(pl.multiple_of(i * ROW_TILE, ROW_TILE), ROW_TILE)

    def start_gather(table_ref, buf_slot):
        def body(i, c):
            pltpu.make_async_copy(x_hbm.at[tile_rows(table_ref[0, 0, i])], xbuf.at[buf_slot, tile_rows(i)],
                                  gsem.at[buf_slot]).start()
            return c
        lax.fori_loop(0, blk, body, 0, unroll=MOE_DMA_UNROLL)

    def wait_gather(buf_slot):
        pltpu.make_async_copy(x_hbm.at[pl.ds(0, blk * ROW_TILE)], xbuf.at[buf_slot], gsem.at[buf_slot]).wait()

    def start_scatter(buf_slot):
        def body(i, c):
            pltpu.make_async_copy(ybuf.at[buf_slot, tile_rows(i)], y_hbm.at[tile_rows(slot_ref[0, 0, i])],
                                  ssem.at[buf_slot]).start()
            return c
        lax.fori_loop(0, blk, body, 0, unroll=MOE_DMA_UNROLL)

    def wait_scatter(sem_slot):
        pltpu.make_async_copy(ybuf.at[0], y_hbm.at[pl.ds(0, blk * ROW_TILE)], ssem.at[sem_slot]).wait()

    @pl.when(v == 0)
    def _():
        cached_ref[0] = -1
        ybuf[...] = jnp.zeros_like(ybuf)

    @pl.when(hi > lo)
    def _():
        @pl.when(lo == 0)
        def _():
            @pl.when(s == 0)
            def _():
                start_gather(tok_ref, slot)
            wait_gather(slot)

            @pl.when(s >= 2)
            def _():
                wait_scatter(slot)

        @pl.when(cached_ref[0] != expert)
        def _():
            rows = 256

            def cast_rows(i, c):
                r0 = pl.multiple_of(i * rows, rows)
                wgu_bf[pl.ds(r0, rows), :D_EXPERT] = wg_ref[0, pl.ds(r0, rows), :].astype(BF16)
                wgu_bf[pl.ds(r0, rows), D_EXPERT:] = wu_ref[0, pl.ds(r0, rows), :].astype(BF16)
                return c
            lax.fori_loop(0, 2 * d_half // rows, cast_rows, 0)
            wd_bf[...] = wd_ref[0].astype(BF16)
            cached_ref[0] = expert

        prefetch = (lo == 0) & (s + 1 < n_blocks)
        dst = jnp.where(prefetch, other, MOE_SPARE_SLOT)
        drain = (lo == 0) & (s > 0)
        dsem = jnp.where(drain, other, MOE_SPARE_SLOT)

        def issue_rows(first, count):
            for i in range(first, first + count):
                pltpu.make_async_copy(x_hbm.at[tile_rows(tok_next_ref[0, 0, i])],
                                      xbuf.at[dst, pl.ds(i * ROW_TILE, ROW_TILE)], gsem.at[dst]).start()
                out_row = jnp.where(drain, slot_prev_ref[0, 0, i], spare_row0 + i)
                pltpu.make_async_copy(ybuf.at[other, pl.ds(i * ROW_TILE, ROW_TILE)], y_hbm.at[tile_rows(out_row)],
                                      ssem.at[dsem]).start()

        words = _load_tile_rows(xbuf, (slot,), blk)
        xlo = jnp.concatenate([_unpack_lo(w).astype(BF16) for w in words], axis=1)
        xhi = jnp.concatenate([_unpack_hi(w).astype(BF16) for w in words], axis=1)
        quarter = blk // 4
        gate_up = []
        for j in range(2):
            issue_rows(j * quarter, quarter)
            cols = slice(j * D_EXPERT, (j + 1) * D_EXPERT)
            gate_up.append(jnp.dot(xlo, wgu_bf[:d_half, cols], preferred_element_type=F32)
                           + jnp.dot(xhi, wgu_bf[d_half:, cols], preferred_element_type=F32))
        hid = (_silu(gate_up[0]) * gate_up[1]).astype(BF16)
        ow = wd_bf.shape[1] // MOE_COL_CHUNKS
        per_chunk = 2 * quarter // MOE_COL_CHUNKS
        outs = []
        for j in range(MOE_COL_CHUNKS):
            issue_rows(2 * quarter + j * per_chunk, per_chunk)
            outs.append(jnp.dot(hid, wd_bf[:, j * ow:(j + 1) * ow], preferred_element_type=F32))
        y = _pack_bf16_pairs(jnp.concatenate(outs, axis=1))

        @pl.when(jnp.logical_not(prefetch))
        def _():
            wait_gather(MOE_SPARE_SLOT)

        @pl.when(jnp.logical_not(drain))
        def _():
            wait_scatter(MOE_SPARE_SLOT)

        @pl.when(lo == 0)
        def _():
            _store_tile_rows(ybuf, (slot,), y)

        @pl.when(lo > 0)
        def _():
            row = lax.broadcasted_iota(I32, (blk, 1), 0)
            old = jnp.concatenate(_load_tile_rows(ybuf, (slot,), blk), axis=1)
            _store_tile_rows(ybuf, (slot,), jnp.where((row >= lo) & (row < hi), y, old))

        @pl.when((hi == blk) & (s == n_blocks - 1))
        def _():
            if n_blocks > 1:
                wait_scatter(other)
            start_scatter(slot)
            wait_scatter(slot)


def _moe(visits, row_tok, row_slot, hn_tiles, w_gate, w_up, w_down):
    blk = MOE_BLOCK
    n_assign = row_tok.shape[0]
    n_blocks = n_assign // blk
    n_visits = visits[0].shape[0]
    d = w_gate.shape[1]
    smem_rows = pl.BlockSpec((1, 1, blk), lambda v, vb, ve, vl, vh: (vb[v], 0, 0), memory_space=pltpu.SMEM)
    smem_next = pl.BlockSpec((1, 1, blk), lambda v, vb, ve, vl, vh: (jnp.minimum(vb[v] + 1, n_blocks - 1), 0, 0),
                             memory_space=pltpu.SMEM)
    smem_prev = pl.BlockSpec((1, 1, blk), lambda v, vb, ve, vl, vh: (jnp.maximum(vb[v] - 1, 0), 0, 0),
                             memory_space=pltpu.SMEM)
    by_expert = lambda shape: pl.BlockSpec((1,) + shape, lambda v, vb, ve, vl, vh: (ve[v], 0, 0))
    grid_spec = pltpu.PrefetchScalarGridSpec(
        num_scalar_prefetch=4,
        grid=(n_visits,),
        in_specs=[smem_rows, smem_next, smem_rows, smem_prev,
                  pl.BlockSpec(memory_space=pl.ANY),
                  by_expert((d, D_EXPERT)), by_expert((d, D_EXPERT)), by_expert((D_EXPERT, d))],
        out_specs=pl.BlockSpec(memory_space=pl.ANY),
        scratch_shapes=[pltpu.VMEM((MOE_SPARE_SLOT + 1, blk * ROW_TILE, LANES), U32),
                        pltpu.VMEM((2, blk * ROW_TILE, LANES), U32),
                        pltpu.VMEM((d, 2 * D_EXPERT), BF16), pltpu.VMEM((D_EXPERT, d), BF16),
                        pltpu.SMEM((1,), I32),
                        pltpu.SemaphoreType.DMA((MOE_SPARE_SLOT + 1,)),
                        pltpu.SemaphoreType.DMA((MOE_SPARE_SLOT + 1,))],
    )
    tok3 = row_tok.reshape(n_blocks, 1, blk)
    slot3 = row_slot.reshape(n_blocks, 1, blk)
    return pl.pallas_call(
        _moe_body,
        grid_spec=grid_spec,
        out_shape=jax.ShapeDtypeStruct(((n_assign + blk) * ROW_TILE, LANES), U32),
        compiler_params=_cparams(1),
        name="moe_experts",
    )(*visits, tok3, tok3, slot3, slot3, hn_tiles, w_gate, w_up, w_down)


def _route_tables(route, t):
    blk = MOE_BLOCK
    n_assign = t * TOP_K
    n_blocks = n_assign // blk
    flat_e = route[:, :TOP_K].astype(I32).T.reshape(n_assign)
    order = jnp.argsort(flat_e, stable=True).astype(I32)
    counts = jnp.sum((flat_e[:, None] == jnp.arange(N_EXPERTS, dtype=I32)[None, :]).astype(I32), axis=0)
    ends = jnp.cumsum(counts)
    starts = ends - counts
    cuts = jnp.sort(jnp.concatenate([jnp.arange(n_blocks, dtype=I32) * blk, starts[1:]]))
    nxt = jnp.concatenate([cuts[1:], jnp.full((1,), n_assign, I32)])
    vblk = jnp.minimum(cuts // blk, n_blocks - 1)
    vexp = jnp.minimum(jnp.sum((ends[None, :] <= cuts[:, None]).astype(I32), axis=1), N_EXPERTS - 1)
    vlo = cuts - vblk * blk
    vhi = jnp.maximum(jnp.minimum(nxt, (vblk + 1) * blk) - vblk * blk, vlo)
    return (vblk, vexp, vlo, vhi), order % t, order


def _combine_body(x2_ref, y0_ref, y1_ref, rt_ref, o_ref):
    tm = o_ref.shape[0]
    half = o_ref.shape[1] // 2
    g0 = rt_ref[:, TOP_K:TOP_K + 1]
    g1 = rt_ref[:, TOP_K + 1:TOP_K + 2]
    y0 = _load_tile_rows(y0_ref, (), tm)
    y1 = _load_tile_rows(y1_ref, (), tm)
    for c in range(ROW_TILE):
        lo_cols = slice(c * LANES, (c + 1) * LANES)
        hi_cols = slice(half + c * LANES, half + (c + 1) * LANES)
        o_ref[:, lo_cols] = x2_ref[:, lo_cols] + g0 * _unpack_lo(y0[c]) + g1 * _unpack_lo(y1[c])
        o_ref[:, hi_cols] = x2_ref[:, hi_cols] + g0 * _unpack_hi(y0[c]) + g1 * _unpack_hi(y1[c])


def _combine(x2, y_slots, route):
    t, d = x2.shape
    tm = min(512, t)
    nb = t // tm
    return pl.pallas_call(
        _combine_body,
        grid=(nb,),
        in_specs=[pl.BlockSpec((tm, d), lambda i: (i, 0)),
                  pl.BlockSpec((tm * ROW_TILE, LANES), lambda i: (i, 0)),
                  pl.BlockSpec((tm * ROW_TILE, LANES), lambda i: (i + nb, 0)),
                  pl.BlockSpec((tm, LANES), lambda i: (i, 0))],
        out_specs=pl.BlockSpec((tm, d), lambda i: (i, 0)),
        out_shape=jax.ShapeDtypeStruct((t, d), F32),
        compiler_params=_cparams(1),
        name="moe_combine",
    )(x2, y_slots, y_slots, route)


def _prepare(norm1_g, w_in, q_norm_g, k_norm_g, rpb, attn_out_g, conv_w, conv_b, a_log_f, a_log_b,
             dt_bias_f, dt_bias_b, d_skip, ssm_norm_g, w_out, norm2_g, w_router_group, b_router_group,
             w_router_expert, b_router_expert, w_gate, w_up, w_down):
    d = w_in.shape[0]
    hpg = HEADS_PER_GROUP
    p = {}
    p["norm1_g"] = norm1_g.reshape(1, d).astype(F32)
    p["w_main"] = w_in[:, :D_PROJ_MAIN].astype(BF16)
    w_dt = w_in[:, D_PROJ_MAIN:]
    zeros = jnp.zeros((d, LANES - 2 * hpg), w_in.dtype)
    per_group = lambda v, g: v[..., g * hpg:(g + 1) * hpg]
    p["w_dt"] = jnp.concatenate(
        [jnp.concatenate([per_group(w_dt[:, :N_HEADS_SSM], g), per_group(w_dt[:, N_HEADS_SSM:], g), zeros], axis=1)
         for g in range(N_GROUPS_SSM)], axis=1).astype(BF16)
    lane_rows = lambda f, bwd: jnp.stack(
        [jnp.concatenate([per_group(f, g), per_group(bwd, g), jnp.zeros((LANES - 2 * hpg,), F32)])
         for g in range(N_GROUPS_SSM)])[:, None, :]
    p["dtb"] = lane_rows(dt_bias_f.astype(F32), dt_bias_b.astype(F32))
    p["arow"] = lane_rows(-jnp.exp(a_log_f.astype(F32)), -jnp.exp(a_log_b.astype(F32)))
    scale = HEAD_DIM ** -0.5
    p["qg2"] = (jnp.tile(q_norm_g.astype(F32), 2) * scale).reshape(1, LANES)
    p["kg2"] = jnp.tile(k_norm_g.astype(F32), 2).reshape(1, LANES)
    lane = jnp.arange(LANES)
    p["e_mat"] = (lane[:, None] // HEAD_DIM == lane[None, :] // HEAD_DIM).astype(BF16)
    p["bias"] = _attention_bias(rpb)
    p["attn_out_g"] = attn_out_g.reshape(1, D_ATTN).astype(F32)
    cw = conv_w.reshape(D_CONV, -1).astype(F32)
    cbias = conv_b.reshape(1, -1).astype(F32)
    group_cols = lambda a, g: jnp.concatenate(
        [a[:, g * D_GROUP:(g + 1) * D_GROUP],
         a[:, D_SSM + g * D_STATE:D_SSM + (g + 1) * D_STATE],
         a[:, D_SSM + D_BC + g * D_STATE:D_SSM + D_BC + (g + 1) * D_STATE]], axis=1)
    p["cw"] = jnp.stack([group_cols(cw, g) for g in range(N_GROUPS_SSM)])
    p["cb"] = jnp.stack([group_cols(cbias, g) for g in range(N_GROUPS_SSM)])
    p["dsk"] = jnp.repeat(d_skip.astype(F32), SSM_HEAD_DIM).reshape(N_GROUPS_SSM, 1, D_GROUP)
    p["ng"] = ssm_norm_g.astype(F32).reshape(N_GROUPS_SSM, 1, D_GROUP)
    col_head = jnp.arange(D_GROUP) // SSM_HEAD_DIM
    p["ef"] = (lane[:, None] == col_head[None, :]).astype(BF16)[None]
    p["eb"] = (lane[:, None] == col_head[None, :] + hpg).astype(BF16)[None]
    pos = jnp.arange(CHUNK)
    p["tri"] = jnp.concatenate([pos[:, None] >= pos[None, :], pos[:, None] <= pos[None, :]], axis=0).astype(BF16)
    src = jnp.arange(CHUNK + 2 * CONV_HALO)
    p["shift"] = jnp.concatenate(
        [src[None, :] == pos[:, None] + (CONV_HALO - D_CONV // 2 + k) for k in range(D_CONV) if k != D_CONV // 2],
        axis=0).astype(BF16)
    p["wa"] = w_out[:D_ATTN].astype(BF16)
    p["ws"] = w_out[D_ATTN:].astype(BF16)
    p["norm2_g"] = norm2_g.reshape(1, d).astype(F32)
    n_r = N_EXPERT_GROUPS + N_EXPERTS
    wr = jnp.concatenate([w_router_group, w_router_expert, jnp.zeros((d, LANES - n_r), F32)], axis=1).astype(F32)
    wr_hi = wr.astype(BF16)
    wr_lo = (wr - wr_hi.astype(F32)).astype(BF16)
    p["wr"] = jnp.concatenate([wr_hi, wr_lo], axis=1)
    p["br"] = jnp.concatenate([b_router_group, b_router_expert, jnp.zeros((LANES - n_r,), F32)]).reshape(1, LANES)
    p["w_gate"], p["w_up"], p["w_down"] = w_gate, w_up, w_down
    return p


def _layer(x, p):
    b, l, d = x.shape
    t = b * l
    x2d = x.reshape(t, d)
    proj, dt = _inproj(x2d, p["norm1_g"], p["w_main"], p["w_dt"])
    proj3 = proj.reshape(b, l, D_PROJ_MAIN)
    attn = _attention(proj3, p["qg2"], p["kg2"], p["e_mat"], p["bias"])
    ssm = _ssd(proj3, dt.reshape(b, l, N_GROUPS_SSM * LANES), p["cw"], p["cb"], p["dtb"], p["arow"],
               p["dsk"], p["ng"], p["ef"], p["eb"], p["tri"], p["shift"])
    x2, hn_packed, route = _outproj(x2d, attn.reshape(t, D_ATTN), ssm.reshape(t, D_SSM), p["attn_out_g"],
                                    p["wa"], p["ws"], p["norm2_g"], p["wr"], p["br"])
    visits, row_tok, row_slot = _route_tables(route, t)
    y_slots = _moe(visits, row_tok, row_slot, hn_packed, p["w_gate"], p["w_up"], p["w_down"])
    return _combine(x2, y_slots, route).reshape(b, l, d)


def kernel(x_prompt, x_sample, norm1_g, w_in, q_norm_g, k_norm_g, rpb, attn_out_g, conv_w, conv_b, a_log_f,
           a_log_b, dt_bias_f, dt_bias_b, d_skip, ssm_norm_g, w_out, norm2_g, w_router_group, b_router_group,
           w_router_expert, b_router_expert, w_gate, w_up, w_down):
    weights = (norm1_g, w_in, q_norm_g, k_norm_g, rpb, attn_out_g, conv_w, conv_b, a_log_f, a_log_b, dt_bias_f,
               dt_bias_b, d_skip, ssm_norm_g, w_out, norm2_g, w_router_group, b_router_group, w_router_expert,
               b_router_expert, w_gate, w_up, w_down)
    assert all(w.shape[0] == 1 for w in weights), "one layer of stacked weights expected"
    p = _prepare(*(w[0] for w in weights))
    return (_layer(x_prompt, p), _layer(x_sample, p))
```

```python
import jax
import jax.numpy as jnp
from jax import lax
from jax.experimental import pallas as pl
from jax.experimental.pallas import tpu as pltpu

F32 = jnp.float32
BF16 = jnp.bfloat16
U32 = jnp.uint32
I32 = jnp.int32

EPS = 1e-6
GRID_W = 64
N_HEADS_ATTN = 16
HEAD_DIM = 64
D_ATTN = N_HEADS_ATTN * HEAD_DIM
WIN_H = 8
WIN_W = 16
N_HEADS_SSM = 16
SSM_HEAD_DIM = 64
D_SSM = N_HEADS_SSM * SSM_HEAD_DIM
N_GROUPS_SSM = 2
HEADS_PER_GROUP = N_HEADS_SSM // N_GROUPS_SSM
D_GROUP = D_SSM // N_GROUPS_SSM
D_STATE = 128
D_CONV = 5
CHUNK = 128
D_BC = N_GROUPS_SSM * D_STATE
N_EXPERT_GROUPS = 4
EXPERTS_PER_GROUP = 8
N_EXPERTS = N_EXPERT_GROUPS * EXPERTS_PER_GROUP
TOP_K = 2
D_EXPERT = 512

LANES = 128
BF16_ROWS = 16
CONV_HALO = BF16_ROWS
NEG = -1e30
VMEM_LIMIT_BYTES = 56 * 1024 * 1024
MOE_BLOCK = 256
ATTN_ROW_UNROLL = 8
SSD_CHUNK_UNROLL = 4
MOE_DMA_UNROLL = 8
MOE_SPARE_SLOT = 2
MOE_COL_CHUNKS = 4
ROW_TILE = 8
OUTPROJ_SUB_ROWS = 256
INPROJ_ROWS = 512
INPROJ_COLS = 2816

COL_Q, COL_K, COL_V = 0, D_ATTN // LANES, 2 * D_ATTN // LANES
COL_Z = 3 * D_ATTN // D_GROUP
COL_XS = (3 * D_ATTN + D_SSM) // D_GROUP
COL_B = (3 * D_ATTN + 2 * D_SSM) // LANES
COL_C = COL_B + D_BC // LANES
D_PROJ_MAIN = 3 * D_ATTN + 2 * D_SSM + 2 * D_BC


def _cparams(n_axes):
    return pltpu.CompilerParams(dimension_semantics=("arbitrary",) * n_axes,
                                vmem_limit_bytes=VMEM_LIMIT_BYTES)


def _silu(x):
    return x * (0.5 * jnp.tanh(0.5 * x) + 0.5)


def _split3(x):
    hi = x.astype(BF16)
    r1 = x - hi.astype(F32)
    mid = r1.astype(BF16)
    lo = (r1 - mid.astype(F32)).astype(BF16)
    return hi, mid, lo


def _pack_bf16_pairs(x):
    n = x.shape[1] // 2
    u = lax.bitcast_convert_type(x.astype(BF16).astype(F32), U32)
    return (u[:, :n] >> 16) | u[:, n:]


def _unpack_lo(u):
    return lax.bitcast_convert_type(u << 16, F32)


def _unpack_hi(u):
    return lax.bitcast_convert_type(u & jnp.uint32(0xFFFF0000), F32)


def _store_tile_rows(ref, lead, packed):
    m = packed.shape[0]
    for c in range(ROW_TILE):
        ref[lead + (pl.ds(c, m, stride=ROW_TILE), slice(None))] = packed[:, c * LANES:(c + 1) * LANES]


def _load_tile_rows(ref, lead, m):
    return [ref[lead + (pl.ds(c, m, stride=ROW_TILE), slice(None))] for c in range(ROW_TILE)]


def _inproj_body(x_ref, g_ref, w_ref, wdt_ref, o_ref, dt_ref, hn_ref):
    tm = x_ref.shape[0]
    rows = min(tm, 256)

    @pl.when(pl.program_id(1) == 0)
    def _():
        def norm_rows(i, c):
            r0 = pl.multiple_of(i * rows, rows)
            x = x_ref[pl.ds(r0, rows), :]
            ms = jnp.mean(x * x, axis=-1, keepdims=True)
            hn_ref[pl.ds(r0, rows), :] = (x * lax.rsqrt(ms + EPS) * g_ref[...]).astype(BF16)
            return c
        lax.fori_loop(0, tm // rows, norm_rows, 0)
        dt_ref[...] = jnp.dot(hn_ref[...], wdt_ref[...], preferred_element_type=F32)

    o_ref[...] = jnp.dot(hn_ref[...], w_ref[...], preferred_element_type=F32).astype(BF16)


def _inproj(x2d, gain, w_main, w_dt):
    t, d = x2d.shape
    n = w_main.shape[1]
    ndt = w_dt.shape[1]
    tm = min(INPROJ_ROWS, t)
    tn = n
    return pl.pallas_call(
        _inproj_body,
        grid=(t // tm, n // tn),
        in_specs=[pl.BlockSpec((tm, d), lambda i, j: (i, 0)),
                  pl.BlockSpec((1, d), lambda i, j: (0, 0)),
                  pl.BlockSpec((d, tn), lambda i, j: (0, j), pipeline_mode=pl.Buffered(1)),
                  pl.BlockSpec((d, ndt), lambda i, j: (0, 0))],
        out_specs=[pl.BlockSpec((tm, tn), lambda i, j: (i, j)),
                   pl.BlockSpec((tm, ndt), lambda i, j: (i, 0))],
        out_shape=[jax.ShapeDtypeStruct((t, n), BF16), jax.ShapeDtypeStruct((t, ndt), F32)],
        scratch_shapes=[pltpu.VMEM((tm, d), BF16)],
        compiler_params=_cparams(2),
        name="inproj",
    )(x2d, gain, w_main, w_dt)


def _attn_body(q_ref, k_ref, v_ref, qg_ref, kg_ref, e_ref, bias_ref, o_ref, qs, ks, s_scr, m_scr):
    l = q_ref.shape[1]
    n_rows = l // GRID_W
    win_keys = WIN_H * GRID_W
    ch = min(l, 512)
    head_a = lax.broadcasted_iota(I32, (1, LANES), 1) < HEAD_DIM
    sel_a = jnp.where(head_a, 1.0, 0.0).astype(BF16)
    sel_b = jnp.where(head_a, 0.0, 1.0).astype(BF16)

    def norm_rows(i, c):
        r0 = pl.multiple_of(i * ch, ch)
        for src, gref, dst in ((q_ref, qg_ref, qs), (k_ref, kg_ref, ks)):
            x = src[0, pl.ds(r0, ch), :].astype(F32)
            ssq = jnp.dot((x * x).astype(BF16), e_ref[...], preferred_element_type=F32)
            dst[pl.ds(r0, ch), :] = (x * lax.rsqrt(ssq * (1.0 / HEAD_DIM) + EPS) * gref[...]).astype(BF16)
        return c
    lax.fori_loop(0, l // ch, norm_rows, 0)

    def key_start(r):
        return jnp.clip(r - WIN_H // 2, 0, n_rows - WIN_H)

    def score_stage(r, slot):
        rs = key_start(r)
        q_r = qs[pl.ds(pl.multiple_of(r * GRID_W, GRID_W), GRID_W), :]
        qm = jnp.concatenate([q_r * sel_a, q_r * sel_b], axis=0)
        kb = ks[pl.ds(pl.multiple_of(rs * GRID_W, GRID_W), win_keys), :]
        s = lax.dot_general(qm, kb, (((1,), (1,)), ((), ())), preferred_element_type=F32)
        dr0 = rs - r + (WIN_H - 1)
        lane0 = pl.multiple_of((dr0 // 2) * LANES, LANES)
        s = s + bias_ref[0, dr0 % 2, :, pl.ds(lane0, win_keys)]
        s_scr[slot] = s
        m_scr[slot] = jnp.max(s, axis=-1, keepdims=True)

    def value_stage(r, slot):
        rs = key_start(r)
        vb = v_ref[0, pl.ds(pl.multiple_of(rs * GRID_W, GRID_W), win_keys), :]
        p = jnp.exp(s_scr[slot] - m_scr[slot])
        den = jnp.sum(p, axis=-1, keepdims=True)
        o = jnp.dot(p.astype(BF16), vb, preferred_element_type=F32) * (1.0 / den)
        out = jnp.where(head_a, o[:GRID_W], o[GRID_W:])
        o_ref[0, pl.ds(pl.multiple_of(r * GRID_W, GRID_W), GRID_W), :] = out.astype(BF16)

    score_stage(0, 0)

    def row_group(i, c):
        r = ATTN_ROW_UNROLL * i
        for j in range(ATTN_ROW_UNROLL):
            score_stage(jnp.minimum(r + j + 1, n_rows - 1), (j + 1) % 2)
            value_stage(r + j, j % 2)
        return c
    lax.fori_loop(0, n_rows // ATTN_ROW_UNROLL, row_group, 0)


def _attention(proj3, qg2, kg2, e_mat, bias):
    b, l, _ = proj3.shape
    n_pairs = N_HEADS_ATTN // 2
    blk = (1, l, LANES)
    return pl.pallas_call(
        _attn_body,
        grid=(n_pairs, b),
        in_specs=[pl.BlockSpec(blk, lambda hp, bi: (bi, 0, COL_Q + hp)),
                  pl.BlockSpec(blk, lambda hp, bi: (bi, 0, COL_K + hp)),
                  pl.BlockSpec(blk, lambda hp, bi: (bi, 0, COL_V + hp)),
                  pl.BlockSpec((1, LANES), lambda hp, bi: (0, 0)),
                  pl.BlockSpec((1, LANES), lambda hp, bi: (0, 0)),
                  pl.BlockSpec((LANES, LANES), lambda hp, bi: (0, 0)),
                  pl.BlockSpec((1,) + bias.shape[1:], lambda hp, bi: (hp, 0, 0, 0))],
        out_specs=pl.BlockSpec(blk, lambda hp, bi: (bi, 0, hp)),
        out_shape=jax.ShapeDtypeStruct((b, l, D_ATTN), BF16),
        scratch_shapes=[pltpu.VMEM((l, LANES), BF16), pltpu.VMEM((l, LANES), BF16),
                        pltpu.VMEM((2, LANES, WIN_H * GRID_W), F32), pltpu.VMEM((2, LANES, 1), F32)],
        compiler_params=_cparams(2),
        name="nbr_attention",
    )(proj3, proj3, proj3, qg2, kg2, e_mat, bias)


def _attention_bias(rpb):
    n_dr = 2 * WIN_H - 1
    cols = jnp.arange(GRID_W, dtype=I32)
    col_start = jnp.clip(cols - WIN_W // 2, 0, GRID_W - WIN_W)
    keys = cols[None, :]
    valid = (keys >= col_start[:, None]) & (keys < col_start[:, None] + WIN_W)
    dc = jnp.clip(keys - cols[:, None] + (WIN_W - 1), 0, 2 * WIN_W - 2)
    tab = jnp.where(valid[None, None], rpb.astype(F32)[:, :, dc], NEG)
    flat = jnp.transpose(tab, (0, 2, 1, 3)).reshape(N_HEADS_ATTN // 2, 2 * GRID_W, n_dr * GRID_W)
    moved = jnp.concatenate([flat[:, :, GRID_W:], jnp.full(flat.shape[:2] + (GRID_W,), NEG, F32)], axis=2)
    return jnp.stack([flat, moved], axis=1)


def _ssd_body(z_ref, xs_ref, b_ref, c_ref, dt_ref, cw_ref, cb_ref, dtb_ref, arow_ref, dsk_ref, ng_ref,
              ef_ref, eb_ref, tri_ref, shift_ref, o_ref, xs_s, b_s, c_s, y_s, sf_s, sb_s, x_f, pre_s):
    l = xs_ref.shape[1]
    n_chunks = l // CHUNK
    hpg = HEADS_PER_GROUP
    ii = lax.broadcasted_iota(I32, (CHUNK, CHUNK), 0)
    jj = lax.broadcasted_iota(I32, (CHUNK, CHUNK), 1)
    causal = ii >= jj
    anti = ii <= jj
    head_a = lax.broadcasted_iota(I32, (1, LANES), 1) < SSM_HEAD_DIM

    def conv_chunk(r0):
        p0 = pl.multiple_of(jnp.maximum(r0 - CONV_HALO, 0), CONV_HALO)
        n0 = pl.multiple_of(jnp.minimum(r0 + CHUNK, l - CONV_HALO), CONV_HALO)
        has_prev = r0 > 0
        has_next = r0 + CHUNK < l
        parts = []
        for ref in (xs_ref, b_ref, c_ref):
            prev = ref[0, pl.ds(p0, CONV_HALO), :]
            nxt = ref[0, pl.ds(n0, CONV_HALO), :]
            parts.append(jnp.concatenate([jnp.where(has_prev, prev, jnp.zeros_like(prev)),
                                          ref[0, pl.ds(r0, CHUNK), :],
                                          jnp.where(has_next, nxt, jnp.zeros_like(nxt))], axis=0))
        ext = jnp.concatenate(parts, axis=1)
        shifted = jnp.dot(shift_ref[...], ext, preferred_element_type=F32)
        mid = D_CONV // 2
        acc = cb_ref[0] + ext[CONV_HALO:CONV_HALO + CHUNK].astype(F32) * cw_ref[0, mid:mid + 1, :]
        for n, k in enumerate(k for k in range(D_CONV) if k != mid):
            acc = acc + shifted[n * CHUNK:(n + 1) * CHUNK] * cw_ref[0, k:k + 1, :]
        return _silu(acc)

    def dt_terms(r0):
        raw = dt_ref[0, pl.ds(r0, CHUNK), :] + dtb_ref[0]
        dtv = jnp.maximum(raw, 0.0) + jnp.log(1.0 + jnp.exp(-jnp.abs(raw)))
        adt = dtv * arow_ref[0]
        hi, mid, lo = _split3(adt)
        cs3 = jnp.dot(tri_ref[...], jnp.concatenate([hi, mid, lo], axis=1), preferred_element_type=F32)
        cs = cs3[:, :LANES] + cs3[:, LANES:2 * LANES] + cs3[:, 2 * LANES:]
        return dtv, cs[:CHUNK], cs[CHUNK:]

    def expand_exact(row, e_ref):
        hi, mid, lo = _split3(jnp.broadcast_to(row, (8, LANES)))
        e = e_ref[0]
        r = (jnp.dot(hi, e, preferred_element_type=F32) + jnp.dot(mid, e, preferred_element_type=F32)
             + jnp.dot(lo, e, preferred_element_type=F32))
        return r[0:1]

    def state_terms(cc, bt, x, dtv, cs, tot, e_ref, s_ref):
        e = e_ref[0]
        expcs = jnp.dot(jnp.exp(cs).astype(BF16), e, preferred_element_type=F32)
        y_off = jnp.dot(cc, s_ref[...].astype(BF16), preferred_element_type=F32) * expcs
        scl = jnp.dot((dtv * jnp.exp(tot - cs)).astype(BF16), e, preferred_element_type=F32)
        states_t = jnp.dot(bt, (x * scl).astype(BF16), preferred_element_type=F32)
        s_ref[...] = s_ref[...] * expand_exact(jnp.exp(tot), e_ref) + states_t
        return y_off

    sf_s[...] = jnp.zeros_like(sf_s)
    sb_s[...] = jnp.zeros_like(sb_s)

    def prep_stage(c, slot):
        r0 = pl.multiple_of(c * CHUNK, CHUNK)
        xbc = conv_chunk(r0)
        x = xbc[:, :D_GROUP]
        bm = xbc[:, D_GROUP:D_GROUP + D_STATE]
        xs_s[pl.ds(r0, CHUNK), :] = x.astype(BF16)
        b_s[pl.ds(r0, CHUNK), :] = bm.astype(BF16)
        c_s[pl.ds(r0, CHUNK), :] = xbc[:, D_GROUP + D_STATE:].astype(BF16)
        x_f[slot] = x
        dtv, cs_f, cs_b = dt_terms(r0)
        for n, a in enumerate((dtv, cs_f, cs_b, cs_f.T, cs_b.T, dtv.T, bm.T)):
            pre_s[slot, n] = a

    def mix_stage(c, slot):
        r0 = pl.multiple_of(c * CHUNK, CHUNK)
        x = x_f[slot]
        xb = xs_s[pl.ds(r0, CHUNK), :]
        bb = b_s[pl.ds(r0, CHUNK), :]
        cc = c_s[pl.ds(r0, CHUNK), :]
        dtv, cs_f, cs_b, cst_f, cst_b, dtt, bmt = (pre_s[slot, n] for n in range(7))
        cb = lax.dot_general(cc, bb, (((1,), (1,)), ((), ())), preferred_element_type=F32)
        pieces = []
        for pair in range(hpg // 2):
            xp = xb[:, pair * LANES:(pair + 1) * LANES]
            ys = []
            for hh in range(2):
                h = 2 * pair + hh
                df = cs_f[:, h:h + 1] - cst_f[h:h + 1, :]
                db = cs_b[:, hpg + h:hpg + h + 1] - cst_b[hpg + h:hpg + h + 1, :]
                lf = jnp.exp(jnp.where(causal, df, NEG)) * dtt[h:h + 1, :]
                lb = jnp.exp(jnp.where(anti, db, NEG)) * dtt[hpg + h:hpg + h + 1, :]
                m = (cb * (lf + lb)).astype(BF16)
                ys.append(jnp.dot(m, xp, preferred_element_type=F32))
            pieces.append(jnp.where(head_a, ys[0], ys[1]))
        y = jnp.concatenate(pieces, axis=1) + x * dsk_ref[0]
        y = y + state_terms(cc, bmt.astype(BF16), x, dtv, cs_f, cs_f[CHUNK - 1:CHUNK, :], ef_ref, sf_s)
        y_s[pl.ds(r0, CHUNK), :] = y

    prep_stage(0, 0)

    def fwd_group(i, carry):
        c = SSD_CHUNK_UNROLL * i
        for j in range(SSD_CHUNK_UNROLL):
            prep_stage(jnp.minimum(c + j + 1, n_chunks - 1), (j + 1) % 2)
            mix_stage(c + j, j % 2)
        return carry
    lax.fori_loop(0, n_chunks // SSD_CHUNK_UNROLL, fwd_group, 0)

    def bwd_prep_stage(c, slot):
        r0 = pl.multiple_of(c * CHUNK, CHUNK)
        dtv, _, cs_b = dt_terms(r0)
        pre_s[slot, 0] = dtv
        pre_s[slot, 2] = cs_b
        pre_s[slot, 6] = b_s[pl.ds(r0, CHUNK), :].astype(F32).T

    def bwd_mix_stage(c, slot):
        r0 = pl.multiple_of(c * CHUNK, CHUNK)
        x = xs_s[pl.ds(r0, CHUNK), :].astype(F32)
        cc = c_s[pl.ds(r0, CHUNK), :]
        dtv, cs_b, bt = pre_s[slot, 0], pre_s[slot, 2], pre_s[slot, 6].astype(BF16)
        y = y_s[pl.ds(r0, CHUNK), :] + state_terms(cc, bt, x, dtv, cs_b, cs_b[0:1, :], eb_ref, sb_s)
        y = y * _silu(z_ref[0, pl.ds(r0, CHUNK), :].astype(F32))
        ms = jnp.mean(y * y, axis=-1, keepdims=True)
        o_ref[0, pl.ds(r0, CHUNK), :] = (y * lax.rsqrt(ms + EPS) * ng_ref[0]).astype(BF16)

    bwd_prep_stage(n_chunks - 1, 0)

    def bwd_group(i, carry):
        c = n_chunks - 1 - SSD_CHUNK_UNROLL * i
        for j in range(SSD_CHUNK_UNROLL):
            bwd_prep_stage(jnp.maximum(c - j - 1, 0), (j + 1) % 2)
            bwd_mix_stage(c - j, j % 2)
        return carry
    lax.fori_loop(0, n_chunks // SSD_CHUNK_UNROLL, bwd_group, 0)


def _ssd(proj3, dt3, cw, cb, dtb, arow, dsk, ng, ef, eb, tri, shift):
    b, l, _ = proj3.shape
    g = N_GROUPS_SSM
    wconv = D_GROUP + 2 * D_STATE
    per_group = lambda shape: pl.BlockSpec((1,) + shape, lambda bi, gi: (gi,) + (0,) * len(shape))
    return pl.pallas_call(
        _ssd_body,
        grid=(b, g),
        in_specs=[pl.BlockSpec((1, l, D_GROUP), lambda bi, gi: (bi, 0, COL_Z + gi)),
                  pl.BlockSpec((1, l, D_GROUP), lambda bi, gi: (bi, 0, COL_XS + gi)),
                  pl.BlockSpec((1, l, D_STATE), lambda bi, gi: (bi, 0, COL_B + gi)),
                  pl.BlockSpec((1, l, D_STATE), lambda bi, gi: (bi, 0, COL_C + gi)),
                  pl.BlockSpec((1, l, LANES), lambda bi, gi: (bi, 0, gi)),
                  per_group((D_CONV, wconv)), per_group((1, wconv)),
                  per_group((1, LANES)), per_group((1, LANES)),
                  per_group((1, D_GROUP)), per_group((1, D_GROUP)),
                  pl.BlockSpec((1, LANES, D_GROUP), lambda bi, gi: (0, 0, 0)),
                  pl.BlockSpec((1, LANES, D_GROUP), lambda bi, gi: (0, 0, 0)),
                  pl.BlockSpec((2 * CHUNK, CHUNK), lambda bi, gi: (0, 0)),
                  pl.BlockSpec(shift.shape, lambda bi, gi: (0, 0))],
        out_specs=pl.BlockSpec((1, l, D_GROUP), lambda bi, gi: (bi, 0, gi)),
        out_shape=jax.ShapeDtypeStruct((b, l, D_SSM), BF16),
        scratch_shapes=[pltpu.VMEM((l, D_GROUP), BF16), pltpu.VMEM((l, D_STATE), BF16),
                        pltpu.VMEM((l, D_STATE), BF16), pltpu.VMEM((l, D_GROUP), F32),
                        pltpu.VMEM((D_STATE, D_GROUP), F32), pltpu.VMEM((D_STATE, D_GROUP), F32),
                        pltpu.VMEM((2, CHUNK, D_GROUP), F32), pltpu.VMEM((2, 7, CHUNK, LANES), F32)],
        compiler_params=_cparams(2),
        name="ssd_mixer",
    )(proj3, proj3, proj3, proj3, dt3, cw, cb, dtb, arow, dsk, ng, ef, eb, tri, shift)


def _route_rows(lg):
    lane = lax.broadcasted_iota(I32, lg.shape, 1).astype(F32)
    n_g = float(N_EXPERT_GROUPS)
    n_e = float(EXPERTS_PER_GROUP)
    gl = jnp.where(lane < n_g, lg, NEG)
    gmax = jnp.max(gl, axis=-1, keepdims=True)
    g_idx = jnp.min(jnp.where(gl == gmax, lane, float(LANES)), axis=-1, keepdims=True)
    g_p = 1.0 / jnp.sum(jnp.exp(gl - gmax), axis=-1, keepdims=True)
    first = n_g + n_e * g_idx
    el = jnp.where((lane >= first) & (lane < first + n_e), lg, NEG)
    e1 = jnp.max(el, axis=-1, keepdims=True)
    i1 = jnp.min(jnp.where(el == e1, lane, float(LANES)), axis=-1, keepdims=True)
    el2 = jnp.where(lane == i1, NEG, el)
    e2 = jnp.max(el2, axis=-1, keepdims=True)
    i2 = jnp.min(jnp.where(el2 == e2, lane, float(LANES)), axis=-1, keepdims=True)
    r = jnp.exp(e2 - e1)
    gate1 = g_p / (1.0 + r)
    gate2 = gate1 * r
    return jnp.where(lane == 0.0, i1 - n_g,
                     jnp.where(lane == 1.0, i2 - n_g,
                               jnp.where(lane == 2.0, gate1, jnp.where(lane == 3.0, gate2, 0.0))))


def _outproj_body(x_ref, a_ref, s_ref, ag_ref, wa_ref, ws_ref, g2_ref, wr_ref, br_ref, x2_ref, hp_ref, rt_ref):
    tm = x_ref.shape[0]
    sub = min(tm, OUTPROJ_SUB_ROWS)
    for r0 in range(0, tm, sub):
        rows = slice(r0, r0 + sub)
        a = a_ref[rows, :].astype(F32)
        ms = jnp.mean(a * a, axis=-1, keepdims=True)
        an = (a * lax.rsqrt(ms + EPS) * ag_ref[...]).astype(BF16)
        y = (jnp.dot(an, wa_ref[...], preferred_element_type=F32)
             + jnp.dot(s_ref[rows, :], ws_ref[...], preferred_element_type=F32))
        x2 = x_ref[rows, :] + y
        x2_ref[rows, :] = x2
        ms2 = jnp.mean(x2 * x2, axis=-1, keepdims=True)
        hn = x2 * lax.rsqrt(ms2 + EPS) * g2_ref[...]
        hi = hn.astype(BF16)
        lo = (hn - hi.astype(F32)).astype(BF16)
        l1 = jnp.dot(hi, wr_ref[...], preferred_element_type=F32)
        l2 = jnp.dot(lo, wr_ref[:, :LANES], preferred_element_type=F32)
        rt_ref[rows, :] = _route_rows(l1[:, :LANES] + l1[:, LANES:] + l2 + br_ref[...])
        packed = _pack_bf16_pairs(hi.astype(F32))
        for c in range(ROW_TILE):
            hp_ref[pl.ds(r0 * ROW_TILE + c, sub, stride=ROW_TILE), :] = packed[:, c * LANES:(c + 1) * LANES]


def _outproj(x2d, attn2d, ssm2d, ag, wa, ws, g2, wr, br):
    t, d = x2d.shape
    tm = min(2 * OUTPROJ_SUB_ROWS, t)
    row = lambda w: pl.BlockSpec((tm, w), lambda i: (i, 0))
    full = lambda a: pl.BlockSpec(a.shape, lambda i: (0,) * a.ndim)
    return pl.pallas_call(
        _outproj_body,
        grid=(t // tm,),
        in_specs=[row(d), row(D_ATTN), row(D_SSM), full(ag), full(wa), full(ws), full(g2), full(wr), full(br)],
        out_specs=[row(d), pl.BlockSpec((tm * ROW_TILE, LANES), lambda i: (i, 0)), row(LANES)],
        out_shape=[jax.ShapeDtypeStruct((t, d), F32), jax.ShapeDtypeStruct((t * ROW_TILE, LANES), U32),
                   jax.ShapeDtypeStruct((t, LANES), F32)],
        compiler_params=_cparams(1),
        name="outproj_router",
    )(x2d, attn2d, ssm2d, ag, wa, ws, g2, wr, br)


def _moe_body(vblk_ref, vexp_ref, vlo_ref, vhi_ref, tok_ref, tok_next_ref, slot_ref, slot_prev_ref, x_hbm, wg_ref,
              wu_ref, wd_ref, y_hbm, xbuf, ybuf, wgu_bf, wd_bf, cached_ref, gsem, ssem):
    blk = xbuf.shape[1] // ROW_TILE
    d_half = wg_ref.shape[1] // 2
    v = pl.program_id(0)
    lo = vlo_ref[v]
    hi = vhi_ref[v]
    s = vblk_ref[v]
    expert = vexp_ref[v]
    n_blocks = y_hbm.shape[0] // (blk * ROW_TILE) - 1
    spare_row0 = n_blocks * blk
    slot = s % 2
    other = 1 - slot

    def tile_rows(i):
        return pl.ds(pl.multiple_of(i * ROW_TILE, ROW_TILE), ROW_TILE)

    def start_gather(table_ref, buf_slot):
        def body(i, c):
            pltpu.make_async_copy(x_hbm.at[tile_rows(table_ref[0, 0, i])], xbuf.at[buf_slot, tile_rows(i)],
                                  gsem.at[buf_slot]).start()
            return c
        lax.fori_loop(0, blk, body, 0, unroll=MOE_DMA_UNROLL)

    def wait_gather(buf_slot):
        pltpu.make_async_copy(x_hbm.at[pl.ds(0, blk * ROW_TILE)], xbuf.at[buf_slot], gsem.at[buf_slot]).wait()

    def start_scatter(buf_slot):
        def body(i, c):
            pltpu.make_async_copy(ybuf.at[buf_slot, tile_rows(i)], y_hbm.at[tile_rows(slot_ref[0, 0, i])],
                                  ssem.at[buf_slot]).start()
            return c
        lax.fori_loop(0, blk, body, 0, unroll=MOE_DMA_UNROLL)

    def wait_scatter(sem_slot):
        pltpu.make_async_copy(ybuf.at[0], y_hbm.at[pl.ds(0, blk * ROW_TILE)], ssem.at[sem_slot]).wait()

    @pl.when(v == 0)
    def _():
        cached_ref[0] = -1
        ybuf[...] = jnp.zeros_like(ybuf)

    @pl.when(hi > lo)
    def _():
        @pl.when(lo == 0)
        def _():
            @pl.when(s == 0)
            def _():
                start_gather(tok_ref, slot)
            wait_gather(slot)

            @pl.when(s >= 2)
            def _():
                wait_scatter(slot)

        @pl.when(cached_ref[0] != expert)
        def _():
            rows = 256

            def cast_rows(i, c):
                r0 = pl.multiple_of(i * rows, rows)
                wgu_bf[pl.ds(r0, rows), :D_EXPERT] = wg_ref[0, pl.ds(r0, rows), :].astype(BF16)
                wgu_bf[pl.ds(r0, rows), D_EXPERT:] = wu_ref[0, pl.ds(r0, rows), :].astype(BF16)
                return c
            lax.fori_loop(0, 2 * d_half // rows, cast_rows, 0)
            wd_bf[...] = wd_ref[0].astype(BF16)
            cached_ref[0] = expert

        prefetch = (lo == 0) & (s + 1 < n_blocks)
        dst = jnp.where(prefetch, other, MOE_SPARE_SLOT)
        drain = (lo == 0) & (s > 0)
        dsem = jnp.where(drain, other, MOE_SPARE_SLOT)

        def issue_rows(first, count):
            for i in range(first, first + count):
                pltpu.make_async_copy(x_hbm.at[tile_rows(tok_next_ref[0, 0, i])],
                                      xbuf.at[dst, pl.ds(i * ROW_TILE, ROW_TILE)], gsem.at[dst]).start(priority=i % 2)
                out_row = jnp.where(drain, slot_prev_ref[0, 0, i], spare_row0 + i)
                pltpu.make_async_copy(ybuf.at[other, pl.ds(i * ROW_TILE, ROW_TILE)], y_hbm.at[tile_rows(out_row)],
                                      ssem.at[dsem]).start(priority=i % 2)

        words = _load_tile_rows(xbuf, (slot,), blk)
        xlo = jnp.concatenate([_unpack_lo(w).astype(BF16) for w in words], axis=1)
        xhi = jnp.concatenate([_unpack_hi(w).astype(BF16) for w in words], axis=1)
        quarter = blk // 4
        gate_up = []
        for j in range(2):
            issue_rows(j * quarter, quarter)
            cols = slice(j * D_EXPERT, (j + 1) * D_EXPERT)
            gate_up.append(jnp.dot(xlo, wgu_bf[:d_half, cols], preferred_element_type=F32)
                           + jnp.dot(xhi, wgu_bf[d_half:, cols], preferred_element_type=F32))
        hid = (_silu(gate_up[0]) * gate_up[1]).astype(BF16)
        ow = wd_bf.shape[1] // MOE_COL_CHUNKS
        per_chunk = 2 * quarter // MOE_COL_CHUNKS
        outs = []
        for j in range(MOE_COL_CHUNKS):
            issue_rows(2 * quarter + j * per_chunk, per_chunk)
            outs.append(jnp.dot(hid, wd_bf[:, j * ow:(j + 1) * ow], preferred_element_type=F32))
        y = _pack_bf16_pairs(jnp.concatenate(outs, axis=1))

        @pl.when(jnp.logical_not(prefetch))
        def _():
            wait_gather(MOE_SPARE_SLOT)

        @pl.when(jnp.logical_not(drain))
        def _():
            wait_scatter(MOE_SPARE_SLOT)

        @pl.when(lo == 0)
        def _():
            _store_tile_rows(ybuf, (slot,), y)

        @pl.when(lo > 0)
        def _():
            row = lax.broadcasted_iota(I32, (blk, 1), 0)
            old = jnp.concatenate(_load_tile_rows(ybuf, (slot,), blk), axis=1)
            _store_tile_rows(ybuf, (slot,), jnp.where((row >= lo) & (row < hi), y, old))

        @pl.when((hi == blk) & (s == n_blocks - 1))
        def _():
            if n_blocks > 1:
                wait_scatter(other)
            start_scatter(slot)
            wait_scatter(slot)


def _moe(visits, row_tok, row_slot, hn_tiles, w_gate, w_up, w_down):
    blk = MOE_BLOCK
    n_assign = row_tok.shape[0]
    n_blocks = n_assign // blk
    n_visits = visits[0].shape[0]
    d = w_gate.shape[1]
    smem_rows = pl.BlockSpec((1, 1, blk), lambda v, vb, ve, vl, vh: (vb[v], 0, 0), memory_space=pltpu.SMEM)
    smem_next = pl.BlockSpec((1, 1, blk), lambda v, vb, ve, vl, vh: (jnp.minimum(vb[v] + 1, n_blocks - 1), 0, 0),
                             memory_space=pltpu.SMEM)
    smem_prev = pl.BlockSpec((1, 1, blk), lambda v, vb, ve, vl, vh: (jnp.maximum(vb[v] - 1, 0), 0, 0),
                             memory_space=pltpu.SMEM)
    by_expert = lambda shape: pl.BlockSpec((1,) + shape, lambda v, vb, ve, vl, vh: (ve[v], 0, 0))
    grid_spec = pltpu.PrefetchScalarGridSpec(
        num_scalar_prefetch=4,
        grid=(n_visits,),
        in_specs=[smem_rows, smem_next, smem_rows, smem_prev,
                  pl.BlockSpec(memory_space=pl.ANY),
                  by_expert((d, D_EXPERT)), by_expert((d, D_EXPERT)), by_expert((D_EXPERT, d))],
        out_specs=pl.BlockSpec(memory_space=pl.ANY),
        scratch_shapes=[pltpu.VMEM((MOE_SPARE_SLOT + 1, blk * ROW_TILE, LANES), U32),
                        pltpu.VMEM((2, blk * ROW_TILE, LANES), U32),
                        pltpu.VMEM((d, 2 * D_EXPERT), BF16), pltpu.VMEM((D_EXPERT, d), BF16),
                        pltpu.SMEM((1,), I32),
                        pltpu.SemaphoreType.DMA((MOE_SPARE_SLOT + 1,)),
                        pltpu.SemaphoreType.DMA((MOE_SPARE_SLOT + 1,))],
    )
    tok3 = row_tok.reshape(n_blocks, 1, blk)
    slot3 = row_slot.reshape(n_blocks, 1, blk)
    return pl.pallas_call(
        _moe_body,
        grid_spec=grid_spec,
        out_shape=jax.ShapeDtypeStruct(((n_assign + blk) * ROW_TILE, LANES), U32),
        compiler_params=_cparams(1),
        name="moe_experts",
    )(*visits, tok3, tok3, slot3, slot3, hn_tiles, w_gate, w_up, w_down)


def _route_tables(route, t):
    blk = MOE_BLOCK
    n_assign = t * TOP_K
    n_blocks = n_assign // blk
    flat_e = route[:, :TOP_K].astype(I32).T.reshape(n_assign)
    order = jnp.argsort(flat_e, stable=True).astype(I32)
    counts = jnp.sum((flat_e[:, None] == jnp.arange(N_EXPERTS, dtype=I32)[None, :]).astype(I32), axis=0)
    ends = jnp.cumsum(counts)
    starts = ends - counts
    cuts = jnp.sort(jnp.concatenate([jnp.arange(n_blocks, dtype=I32) * blk, starts[1:]]))
    nxt = jnp.concatenate([cuts[1:], jnp.full((1,), n_assign, I32)])
    vblk = jnp.minimum(cuts // blk, n_blocks - 1)
    vexp = jnp.minimum(jnp.sum((ends[None, :] <= cuts[:, None]).astype(I32), axis=1), N_EXPERTS - 1)
    vlo = cuts - vblk * blk
    vhi = jnp.maximum(jnp.minimum(nxt, (vblk + 1) * blk) - vblk * blk, vlo)
    return (vblk, vexp, vlo, vhi), order % t, order


def _combine_body(x2_ref, y0_ref, y1_ref, rt_ref, o_ref):
    tm = o_ref.shape[0]
    half = o_ref.shape[1] // 2
    g0 = rt_ref[:, TOP_K:TOP_K + 1]
    g1 = rt_ref[:, TOP_K + 1:TOP_K + 2]
    y0 = _load_tile_rows(y0_ref, (), tm)
    y1 = _load_tile_rows(y1_ref, (), tm)
    for c in range(ROW_TILE):
        lo_cols = slice(c * LANES, (c + 1) * LANES)
        hi_cols = slice(half + c * LANES, half + (c + 1) * LANES)
        o_ref[:, lo_cols] = x2_ref[:, lo_cols] + g0 * _unpack_lo(y0[c]) + g1 * _unpack_lo(y1[c])
        o_ref[:, hi_cols] = x2_ref[:, hi_cols] + g0 * _unpack_hi(y0[c]) + g1 * _unpack_hi(y1[c])


def _combine(x2, y_slots, route):
    t, d = x2.shape
    tm = min(512, t)
    nb = t // tm
    return pl.pallas_call(
        _combine_body,
        grid=(nb,),
        in_specs=[pl.BlockSpec((tm, d), lambda i: (i, 0)),
                  pl.BlockSpec((tm * ROW_TILE, LANES), lambda i: (i, 0)),
                  pl.BlockSpec((tm * ROW_TILE, LANES), lambda i: (i + nb, 0)),
                  pl.BlockSpec((tm, LANES), lambda i: (i, 0))],
        out_specs=pl.BlockSpec((tm, d), lambda i: (i, 0)),
        out_shape=jax.ShapeDtypeStruct((t, d), F32),
        compiler_params=_cparams(1),
        name="moe_combine",
    )(x2, y_slots, y_slots, route)


def _prepare(norm1_g, w_in, q_norm_g, k_norm_g, rpb, attn_out_g, conv_w, conv_b, a_log_f, a_log_b,
             dt_bias_f, dt_bias_b, d_skip, ssm_norm_g, w_out, norm2_g, w_router_group, b_router_group,
             w_router_expert, b_router_expert, w_gate, w_up, w_down):
    d = w_in.shape[0]
    hpg = HEADS_PER_GROUP
    p = {}
    p["norm1_g"] = norm1_g.reshape(1, d).astype(F32)
    p["w_main"] = w_in[:, :D_PROJ_MAIN].astype(BF16)
    w_dt = w_in[:, D_PROJ_MAIN:]
    zeros = jnp.zeros((d, LANES - 2 * hpg), w_in.dtype)
    per_group = lambda v, g: v[..., g * hpg:(g + 1) * hpg]
    p["w_dt"] = jnp.concatenate(
        [jnp.concatenate([per_group(w_dt[:, :N_HEADS_SSM], g), per_group(w_dt[:, N_HEADS_SSM:], g), zeros], axis=1)
         for g in range(N_GROUPS_SSM)], axis=1).astype(BF16)
    lane_rows = lambda f, bwd: jnp.stack(
        [jnp.concatenate([per_group(f, g), per_group(bwd, g), jnp.zeros((LANES - 2 * hpg,), F32)])
         for g in range(N_GROUPS_SSM)])[:, None, :]
    p["dtb"] = lane_rows(dt_bias_f.astype(F32), dt_bias_b.astype(F32))
    p["arow"] = lane_rows(-jnp.exp(a_log_f.astype(F32)), -jnp.exp(a_log_b.astype(F32)))
    scale = HEAD_DIM ** -0.5
    p["qg2"] = (jnp.tile(q_norm_g.astype(F32), 2) * scale).reshape(1, LANES)
    p["kg2"] = jnp.tile(k_norm_g.astype(F32), 2).reshape(1, LANES)
    lane = jnp.arange(LANES)
    p["e_mat"] = (lane[:, None] // HEAD_DIM == lane[None, :] // HEAD_DIM).astype(BF16)
    p["bias"] = _attention_bias(rpb)
    p["attn_out_g"] = attn_out_g.reshape(1, D_ATTN).astype(F32)
    cw = conv_w.reshape(D_CONV, -1).astype(F32)
    cbias = conv_b.reshape(1, -1).astype(F32)
    group_cols = lambda a, g: jnp.concatenate(
        [a[:, g * D_GROUP:(g + 1) * D_GROUP],
         a[:, D_SSM + g * D_STATE:D_SSM + (g + 1) * D_STATE],
         a[:, D_SSM + D_BC + g * D_STATE:D_SSM + D_BC + (g + 1) * D_STATE]], axis=1)
    p["cw"] = jnp.stack([group_cols(cw, g) for g in range(N_GROUPS_SSM)])
    p["cb"] = jnp.stack([group_cols(cbias, g) for g in range(N_GROUPS_SSM)])
    p["dsk"] = jnp.repeat(d_skip.astype(F32), SSM_HEAD_DIM).reshape(N_GROUPS_SSM, 1, D_GROUP)
    p["ng"] = ssm_norm_g.astype(F32).reshape(N_GROUPS_SSM, 1, D_GROUP)
    col_head = jnp.arange(D_GROUP) // SSM_HEAD_DIM
    p["ef"] = (lane[:, None] == col_head[None, :]).astype(BF16)[None]
    p["eb"] = (lane[:, None] == col_head[None, :] + hpg).astype(BF16)[None]
    pos = jnp.arange(CHUNK)
    p["tri"] = jnp.concatenate([pos[:, None] >= pos[None, :], pos[:, None] <= pos[None, :]], axis=0).astype(BF16)
    src = jnp.arange(CHUNK + 2 * CONV_HALO)
    p["shift"] = jnp.concatenate(
        [src[None, :] == pos[:, None] + (CONV_HALO - D_CONV // 2 + k) for k in range(D_CONV) if k != D_CONV // 2],
        axis=0).astype(BF16)
    p["wa"] = w_out[:D_ATTN].astype(BF16)
    p["ws"] = w_out[D_ATTN:].astype(BF16)
    p["norm2_g"] = norm2_g.reshape(1, d).astype(F32)
    n_r = N_EXPERT_GROUPS + N_EXPERTS
    wr = jnp.concatenate([w_router_group, w_router_expert, jnp.zeros((d, LANES - n_r), F32)], axis=1).astype(F32)
    wr_hi = wr.astype(BF16)
    wr_lo = (wr - wr_hi.astype(F32)).astype(BF16)
    p["wr"] = jnp.concatenate([wr_hi, wr_lo], axis=1)
    p["br"] = jnp.concatenate([b_router_group, b_router_expert, jnp.zeros((LANES - n_r,), F32)]).reshape(1, LANES)
    p["w_gate"], p["w_up"], p["w_down"] = w_gate, w_up, w_down
    return p


def _layer(x, p):
    b, l, d = x.shape
    t = b * l
    x2d = x.reshape(t, d)
    proj, dt = _inproj(x2d, p["norm1_g"], p["w_main"], p["w_dt"])
    proj3 = proj.reshape(b, l, D_PROJ_MAIN)
    attn = _attention(proj3, p["qg2"], p["kg2"], p["e_mat"], p["bias"])
    ssm = _ssd(proj3, dt.reshape(b, l, N_GROUPS_SSM * LANES), p["cw"], p["cb"], p["dtb"], p["arow"],
               p["dsk"], p["ng"], p["ef"], p["eb"], p["tri"], p["shift"])
    x2, hn_packed, route = _outproj(x2d, attn.reshape(t, D_ATTN), ssm.reshape(t, D_SSM), p["attn_out_g"],
                                    p["wa"], p["ws"], p["norm2_g"], p["wr"], p["br"])
    visits, row_tok, row_slot = _route_tables(route, t)
    y_slots = _moe(visits, row_tok, row_slot, hn_packed, p["w_gate"], p["w_up"], p["w_down"])
    return _combine(x2, y_slots, route).reshape(b, l, d)


def kernel(x_prompt, x_sample, norm1_g, w_in, q_norm_g, k_norm_g, rpb, attn_out_g, conv_w, conv_b, a_log_f,
           a_log_b, dt_bias_f, dt_bias_b, d_skip, ssm_norm_g, w_out, norm2_g, w_router_group, b_router_group,
           w_router_expert, b_router_expert, w_gate, w_up, w_down):
    weights = (norm1_g, w_in, q_norm_g, k_norm_g, rpb, attn_out_g, conv_w, conv_b, a_log_f, a_log_b, dt_bias_f,
               dt_bias_b, d_skip, ssm_norm_g, w_out, norm2_g, w_router_group, b_router_group, w_router_expert,
               b_router_expert, w_gate, w_up, w_down)
    assert all(w.shape[0] == 1 for w in weights), "one layer of stacked weights expected"
    p = _prepare(*(w[0] for w in weights))
    return (_layer(x_prompt, p), _layer(x_sample, p))
```
